```python
import jax, jax.numpy as jnp
from jax import lax
import numpy as np

D_MODEL = 1024
BATCH = 4
SEQ = 4096
DEPTH = 4
DEC_BATCH = 8
DEC_SEQ = 32
PAST_LEN = 2048

CHUNK = 64
N_MIX_GROUPS = 4
GROUP_W = D_MODEL // N_MIX_GROUPS
A_HEADS = 4
A_HEAD_DIM = GROUP_W // A_HEADS
Q_BLOCK = 128
CONV_W = 31
POOL_WINDOWS = (2, 4, 8, 16)
POOL_GROUPS = len(POOL_WINDOWS)
POOL_CH = GROUP_W // POOL_GROUPS
POOL_HIST = max(POOL_WINDOWS) - 1
SG_CHUNK = 128
SG_GROUPS = 4
SG_CH = GROUP_W // SG_GROUPS
P_DIM = 256
D_FF = 2816
N_EXPERTS = 8
TOP_K = 2
D_EXPERT = 2816
N_DENSE = (DEPTH + 1) // 2
N_MOE = DEPTH // 2
EPS = 1e-6

OFF_Q = 0
OFF_K = OFF_Q + GROUP_W
OFF_V = OFF_K + GROUP_W
OFF_F = OFF_V + GROUP_W
OFF_CONV = OFF_F + A_HEADS
OFF_POOL = OFF_CONV + 2 * GROUP_W
OFF_SG = OFF_POOL + GROUP_W
D_IN = OFF_SG + 2 * GROUP_W

kernel_name = 'fox_conv_pool_sgu_hybrid_stream_step'


def rms_norm(x, g):
    xf = x.astype(jnp.float32)
    y = xf * lax.rsqrt(jnp.mean(xf * xf, axis=-1, keepdims=True) + EPS)
    return (y * g.astype(jnp.float32)).astype(x.dtype)


def layer_norm(x, g, b):
    xf = x.astype(jnp.float32)
    mu = jnp.mean(xf, axis=-1, keepdims=True)
    var = jnp.mean(jnp.square(xf - mu), axis=-1, keepdims=True)
    y = (xf - mu) * lax.rsqrt(var + EPS) * g.astype(jnp.float32) + b.astype(jnp.float32)
    return y.astype(x.dtype)


def fox_prompt(q, k, v, logf):
    b_, s_, h_, d_ = q.shape
    nb = s_ // Q_BLOCK
    c = jnp.cumsum(logf, axis=1).transpose(0, 2, 1)
    kpos = jnp.arange(s_)
    qb = q.reshape(b_, nb, Q_BLOCK, h_, d_).transpose(1, 0, 2, 3, 4)
    cb = c.reshape(b_, h_, nb, Q_BLOCK).transpose(2, 0, 1, 3)

    def block(args):
        i, qi, ci = args
        qpos = i * Q_BLOCK + jnp.arange(Q_BLOCK)
        s = jnp.einsum('bqhd,bkhd->bhqk', qi, k).astype(jnp.float32) * (d_ ** -0.5)
        s = s + ci[..., :, None] - c[..., None, :]
        s = jnp.where(kpos[None, :] <= qpos[:, None], s, -jnp.inf)
        p = jax.nn.softmax(s, axis=-1).astype(v.dtype)
        return jnp.einsum('bhqk,bkhd->bqhd', p, v)

    out = lax.map(block, (jnp.arange(nb), qb, cb))
    return out.transpose(1, 0, 2, 3, 4).reshape(b_, s_, h_ * d_)


def fox_sample(q, k_new, v_new, logf_new, ck, cv, clogf):
    b_, l_, h_, d_ = q.shape
    p_ = ck.shape[1]
    k = jnp.concatenate([ck.astype(k_new.dtype), k_new], axis=1)
    v = jnp.concatenate([cv.astype(v_new.dtype), v_new], axis=1)
    c = jnp.cumsum(jnp.concatenate([clogf.astype(jnp.float32), logf_new], axis=1), axis=1)
    c = c.transpose(0, 2, 1)
    s = jnp.einsum('bqhd,bkhd->bhqk', q, k).astype(jnp.float32) * (d_ ** -0.5)
    s = s + c[..., p_:, None] - c[..., None, :]
    qpos = p_ + jnp.arange(l_)
    kpos = jnp.arange(p_ + l_)
    s = jnp.where(kpos[None, :] <= qpos[:, None], s, -jnp.inf)
    p = jax.nn.softmax(s, axis=-1).astype(v.dtype)
    return jnp.einsum('bhqk,bkhd->bqhd', p, v).reshape(b_, l_, h_ * d_)


def conv_module(a, g, prev, dw_w, dw_b, ln_g, ln_b, pw_w):
    u = a * jax.nn.sigmoid(g)
    xp = jnp.concatenate([prev.astype(u.dtype), u], axis=1)
    y = lax.conv_general_dilated(xp, dw_w.astype(u.dtype)[:, None, :], (1,), 'VALID',
                                 dimension_numbers=('NWC', 'WIO', 'NWC'),
                                 feature_group_count=GROUP_W) + dw_b
    y = jax.nn.silu(layer_norm(y, ln_g, ln_b))
    return y @ pw_w, xp[:, -(CONV_W - 1):]


def pool_mixer(xin, prev, pos0, lin_w, scale):
    b_, l_, _ = xin.shape
    hist = prev.shape[1]
    xc = jnp.concatenate([prev.astype(xin.dtype), xin], axis=1).astype(jnp.float32)
    cs = jnp.pad(jnp.cumsum(xc, axis=1), ((0, 0), (1, 0), (0, 0)))
    pos = (pos0 + jnp.arange(l_)).astype(jnp.float32)
    outs = []
    for gi, w in enumerate(POOL_WINDOWS):
        lo, hi = gi * POOL_CH, (gi + 1) * POOL_CH
        wsum = cs[:, hist + 1:hist + 1 + l_, lo:hi] - cs[:, hist + 1 - w:hist + 1 - w + l_, lo:hi]
        cnt = jnp.minimum(pos + 1.0, float(w))[None, :, None]
        outs.append(wsum / cnt)
    d = (jnp.concatenate(outs, axis=-1) - xc[:, hist:]).astype(xin.dtype)
    d = d.reshape(b_, l_, POOL_GROUPS, POOL_CH)
    y = jnp.einsum('blgc,gcd->blgd', d, lin_w).reshape(b_, l_, GROUP_W) * scale
    return y, xc[:, -POOL_HIST:].astype(xin.dtype)


def spatial_gating(u, v, sg_w, sg_b):
    b_, l_, _ = u.shape
    t_ = min(l_, SG_CHUNK)
    n_ = l_ // t_
    u = jax.nn.gelu(u)
    v = jax.nn.gelu(v)
    w = jnp.tril(sg_w[:, :t_, :t_])
    vb = v.reshape(b_, n_, t_, SG_GROUPS, SG_CH)
    m = jnp.einsum('gts,bnsgc->bntgc', w, vb) + sg_b[:, :t_].T[None, None, :, :, None]
    return u * m.reshape(b_, l_, GROUP_W), v


def swiglu(x, wg, wu, wd):
    return (jax.nn.silu(x @ wg) * (x @ wu)) @ wd


def moe_swiglu(x, router, wg, wu, wd):
    logits = (x @ router).astype(jnp.float32)
    top_v, top_i = lax.top_k(logits, TOP_K)
    gates = jax.nn.softmax(top_v, axis=-1)
    dense_g = jnp.sum(jax.nn.one_hot(top_i, N_EXPERTS, dtype=jnp.float32) * gates[..., None], axis=-2)
    dense_g = dense_g.astype(x.dtype)
    out = jnp.zeros_like(x)
    for e in range(N_EXPERTS):
        out = out + dense_g[..., e:e + 1] * swiglu(x, wg[e], wu[e], wd[e])
    return out


def trunk_layer(h, p_l, pos0, att_cache, conv_prev, pool_prev, lw):
    b_, l_, _ = h.shape
    hn = rms_norm(h, lw['norm1_g'])
    z = hn @ lw['w_in']
    q = z[..., OFF_Q:OFF_K].reshape(b_, l_, A_HEADS, A_HEAD_DIM)
    k = z[..., OFF_K:OFF_V].reshape(b_, l_, A_HEADS, A_HEAD_DIM)
    v = z[..., OFF_V:OFF_F].reshape(b_, l_, A_HEADS, A_HEAD_DIM)
    logf = jax.nn.log_sigmoid(z[..., OFF_F:OFF_CONV].astype(jnp.float32) + lw['b_f'].astype(jnp.float32))
    if att_cache is None:
        a = fox_prompt(q, k, v, logf)
    else:
        a = fox_sample(q, k, v, logf, att_cache[0], att_cache[1], att_cache[2])
    b, conv_state = conv_module(z[..., OFF_CONV:OFF_CONV + GROUP_W], z[..., OFF_CONV + GROUP_W:OFF_POOL],
                                conv_prev, lw['conv_dw_w'], lw['conv_dw_b'], lw['conv_ln_g'],
                                lw['conv_ln_b'], lw['conv_pw_w'])
    c, pool_state = pool_mixer(z[..., OFF_POOL:OFF_SG], pool_prev, pos0, lw['pool_w'], lw['pool_scale'])
    d, sg_v = spatial_gating(z[..., OFF_SG:OFF_SG + GROUP_W], z[..., OFF_SG + GROUP_W:D_IN],
                             lw['sg_w'], lw['sg_b'])
    h = h + jnp.concatenate([a, b, c, d], axis=-1) @ lw['w_o']
    hn2 = rms_norm(h, lw['norm2_g'])
    if 'ffn' in lw:
        f = swiglu(hn2, lw['ffn'][0], lw['ffn'][1], lw['ffn'][2])
    else:
        f = moe_swiglu(hn2, lw['moe'][0], lw['moe'][1], lw['moe'][2], lw['moe'][3])
    h = h + f
    h = h + jax.nn.sigmoid(h @ lw['ple_gate_w']) * (p_l @ lw['ple_w'])
    return h, k, v, logf, conv_state, pool_state, sg_v


def setup_inputs(seed: int = 0) -> dict:
    key = jax.random.key(seed)
    ks = list(jax.random.split(key, 40))
    f32 = jnp.float32

    def nrm(i, shape, scale):
        return scale * jax.random.normal(ks[i], shape, f32)

    return {
        'x_prompt': nrm(0, (BATCH, SEQ, D_MODEL), 1.0),
        'x_sample': nrm(1, (DEC_BATCH, DEC_SEQ, D_MODEL), 1.0),
        'p_prompt': nrm(2, (DEPTH, BATCH, SEQ, P_DIM), 1.0),
        'p_sample': nrm(3, (DEPTH, DEC_BATCH, DEC_SEQ, P_DIM), 1.0),
        'cache_k': nrm(4, (DEPTH, DEC_BATCH, PAST_LEN, A_HEADS, A_HEAD_DIM), 1.0),
        'cache_v': nrm(5, (DEPTH, DEC_BATCH, PAST_LEN, A_HEADS, A_HEAD_DIM), 1.0),
        'cache_logf': jax.nn.log_sigmoid(2.5 + nrm(6, (DEPTH, DEC_BATCH, PAST_LEN, A_HEADS), 1.0)),
        'state_conv': nrm(7, (DEPTH, DEC_BATCH, CONV_W - 1, GROUP_W), 0.5),
        'state_pool': nrm(8, (DEPTH, DEC_BATCH, POOL_HIST, GROUP_W), 1.0),
        'norm1_g': 1.0 + nrm(9, (DEPTH, D_MODEL), 0.02),
        'w_in': nrm(10, (DEPTH, D_MODEL, D_IN), D_MODEL ** -0.5),
        'b_f': jax.random.uniform(ks[11], (DEPTH, A_HEADS), f32, 1.0, 4.0),
        'conv_dw_w': nrm(12, (DEPTH, CONV_W, GROUP_W), CONV_W ** -0.5),
        'conv_dw_b': nrm(13, (DEPTH, GROUP_W), 0.02),
        'conv_ln_g': 1.0 + nrm(14, (DEPTH, GROUP_W), 0.02),
        'conv_ln_b': nrm(15, (DEPTH, GROUP_W), 0.02),
        'conv_pw_w': nrm(16, (DEPTH, GROUP_W, GROUP_W), GROUP_W ** -0.5),
        'pool_w': nrm(17, (DEPTH, POOL_GROUPS, POOL_CH, POOL_CH), POOL_CH ** -0.5),
        'pool_scale': 1.0 + nrm(18, (DEPTH, GROUP_W), 0.02),
        'sg_w': nrm(19, (DEPTH, SG_GROUPS, SG_CHUNK, SG_CHUNK), 0.05),
        'sg_b': 1.0 + nrm(20, (DEPTH, SG_GROUPS, SG_CHUNK), 0.02),
        'w_o': nrm(21, (DEPTH, D_MODEL, D_MODEL), D_MODEL ** -0.5),
        'norm2_g': 1.0 + nrm(22, (DEPTH, D_MODEL), 0.02),
        'ffn_w_gate': nrm(23, (N_DENSE, D_MODEL, D_FF), D_MODEL ** -0.5),
        'ffn_w_up': nrm(24, (N_DENSE, D_MODEL, D_FF), D_MODEL ** -0.5),
        'ffn_w_down': nrm(25, (N_DENSE, D_FF, D_MODEL), D_FF ** -0.5),
        'moe_router': nrm(26, (N_MOE, D_MODEL, N_EXPERTS), D_MODEL ** -0.5),
        'moe_w_gate': nrm(27, (N_MOE, N_EXPERTS, D_MODEL, D_EXPERT), D_MODEL ** -0.5),
        'moe_w_up': nrm(28, (N_MOE, N_EXPERTS, D_MODEL, D_EXPERT), D_MODEL ** -0.5),
        'moe_w_down': nrm(29, (N_MOE, N_EXPERTS, D_EXPERT, D_MODEL), D_EXPERT ** -0.5),
        'ple_w': nrm(30, (DEPTH, P_DIM, D_MODEL), P_DIM ** -0.5),
        'ple_gate_w': nrm(31, (DEPTH, D_MODEL, D_MODEL), D_MODEL ** -0.5),
        'final_g': 1.0 + nrm(32, (D_MODEL,), 0.02),
    }


def reference(x_prompt, x_sample, p_prompt, p_sample, cache_k, cache_v, cache_logf, state_conv, state_pool,
              norm1_g, w_in, b_f, conv_dw_w, conv_dw_b, conv_ln_g, conv_ln_b, conv_pw_w, pool_w, pool_scale,
              sg_w, sg_b, w_o, norm2_g, ffn_w_gate, ffn_w_up, ffn_w_down, moe_router, moe_w_gate, moe_w_up,
              moe_w_down, ple_w, ple_gate_w, final_g):
    bp = x_prompt.shape[0]
    past = cache_k.shape[2]
    conv_zero = jnp.zeros((bp, CONV_W - 1, GROUP_W), x_prompt.dtype)
    pool_zero = jnp.zeros((bp, POOL_HIST, GROUP_W), x_prompt.dtype)
    hp, hs = x_prompt, x_sample
    kp_l, vp_l, fp_l, cp_l, pp_l = [], [], [], [], []
    ks_l, vs_l, fs_l, cs_l, ps_l, gs_l = [], [], [], [], [], []
    for l in range(DEPTH):
        lw = {'norm1_g': norm1_g[l], 'w_in': w_in[l], 'b_f': b_f[l], 'conv_dw_w': conv_dw_w[l],
              'conv_dw_b': conv_dw_b[l], 'conv_ln_g': conv_ln_g[l], 'conv_ln_b': conv_ln_b[l],
              'conv_pw_w': conv_pw_w[l], 'pool_w': pool_w[l], 'pool_scale': pool_scale[l],
              'sg_w': sg_w[l], 'sg_b': sg_b[l], 'w_o': w_o[l], 'norm2_g': norm2_g[l],
              'ple_w': ple_w[l], 'ple_gate_w': ple_gate_w[l]}
        j = l // 2
        if l % 2 == 0:
            lw['ffn'] = (ffn_w_gate[j], ffn_w_up[j], ffn_w_down[j])
        else:
            lw['moe'] = (moe_router[j], moe_w_gate[j], moe_w_up[j], moe_w_down[j])
        hp, kp, vp, fp, cp, pp, _ = trunk_layer(hp, p_prompt[l], 0, None, conv_zero, pool_zero, lw)
        hs, k_s, v_s, f_s, c_s, p_s, g_s = trunk_layer(
            hs, p_sample[l], past, (cache_k[l], cache_v[l], cache_logf[l]), state_conv[l], state_pool[l], lw)
        kp_l.append(kp); vp_l.append(vp); fp_l.append(fp); cp_l.append(cp); pp_l.append(pp)
        ks_l.append(k_s); vs_l.append(v_s); fs_l.append(f_s); cs_l.append(c_s); ps_l.append(p_s); gs_l.append(g_s)
    y_prompt = rms_norm(hp, final_g)
    y_sample = rms_norm(hs, final_g)
    new_k_prompt = jnp.stack(kp_l)
    new_v_prompt = jnp.stack(vp_l)
    new_logf_prompt = jnp.stack(fp_l)
    new_conv_prompt = jnp.stack(cp_l)
    new_pool_prompt = jnp.stack(pp_l)
    new_k_sample = jnp.stack(ks_l)
    new_v_sample = jnp.stack(vs_l)
    new_logf_sample = jnp.stack(fs_l)
    new_conv_sample = jnp.stack(cs_l)
    new_pool_sample = jnp.stack(ps_l)
    new_sgv_sample = jnp.stack(gs_l)
    return (y_prompt, y_sample, new_k_prompt, new_v_prompt, new_logf_prompt, new_conv_prompt, new_pool_prompt,
            new_k_sample, new_v_sample, new_logf_sample, new_conv_sample, new_pool_sample, new_sgv_sample)
```

```python
import functools

import numpy as np
import jax
import jax.numpy as jnp
from jax import lax
from jax.experimental import pallas as pl
from jax.experimental.pallas import tpu as pltpu

F32 = jnp.float32
BF16 = jnp.bfloat16
EPS = 1e-6

D_MODEL = 1024
GROUP_W = 256
N_HEADS = 4
HEAD_DIM = 64
CONV_W = 31
POOL_WINDOWS = (2, 4, 8, 16)
POOL_HIST = 15
SG_GROUPS = 4
N_EXPERTS = 8
LANES = 128
AUG_PER_HEAD = 6
VMEM_LIMIT = 56 * 1024 * 1024


def _params(sem, vmem=VMEM_LIMIT):
    return pltpu.CompilerParams(dimension_semantics=sem, vmem_limit_bytes=vmem)


def _dot(a, b):
    return jnp.dot(a, b, preferred_element_type=F32)


def _split3(x):
    hi = x.astype(BF16)
    r1 = x - hi.astype(F32)
    mid = r1.astype(BF16)
    lo = (r1 - mid.astype(F32)).astype(BF16)
    return jnp.concatenate([hi, mid, lo], axis=1)


def _running_sum(x, tril, carry_ref):
    cs = _dot(tril, _split3(x))
    c = carry_ref[...] + (cs[:, :LANES] + cs[:, LANES:2 * LANES] + cs[:, 2 * LANES:])
    n = x.shape[0]
    carry_ref[...] = c[n - 1:n, :]
    return c


def _placement(sign_k):
    pq = np.zeros((3 * LANES, 2 * LANES), np.float32)
    pk = np.zeros((3 * LANES, 2 * LANES), np.float32)
    oq = np.zeros((1, 2 * LANES), np.float32)
    ok = np.zeros((1, 2 * LANES), np.float32)
    for h in range(N_HEADS):
        p, j = divmod(h, 2)
        for i in range(3):
            pq[i * LANES + h, p * LANES + j * AUG_PER_HEAD + i] = 1.0
            oq[0, p * LANES + j * AUG_PER_HEAD + 3 + i] = 1.0
            pk[i * LANES + h, p * LANES + j * AUG_PER_HEAD + 3 + i] = sign_k
            ok[0, p * LANES + j * AUG_PER_HEAD + i] = 1.0
    return pq, pk, oq, ok


def _in_proj_kernel(h_ref, g_ref, w_ref, bf_ref, c0_ref, tril_ref, pq_ref, pk_ref, oq_ref, ok_ref,
                    qs_ref, kf_ref, vf_ref, kb_ref, vb_ref, lf_ref, aq_ref, ak_ref, zc_ref, carry_ref):
    @pl.when(pl.program_id(1) == 0)
    def _():
        carry_ref[...] = c0_ref[0]

    x = h_ref[...]
    hn = x * lax.rsqrt(jnp.mean(x * x, axis=-1, keepdims=True) + EPS) * g_ref[...]
    hb = hn.astype(BF16)
    qs_ref[...] = (_dot(hb, w_ref[:, 0:256]) * (HEAD_DIM ** -0.5)).astype(BF16)
    k = _dot(hb, w_ref[:, 256:512])
    kf_ref[...] = k
    kb_ref[...] = k.astype(BF16)
    v = _dot(hb, w_ref[:, 512:768])
    vf_ref[...] = v
    vb_ref[...] = v.astype(BF16)
    zc_ref[...] = _dot(hb, w_ref[:, 768:2048])
    zf = _dot(hb, w_ref[:, 2048:2176]) + bf_ref[...]
    lf = jnp.minimum(zf, 0.0) - jnp.log1p(jnp.exp(-jnp.abs(zf)))
    lane = lax.broadcasted_iota(jnp.int32, lf.shape, 1)
    lf = jnp.where(lane < N_HEADS, lf, 0.0)
    lf_ref[...] = lf
    cp = _split3(_running_sum(lf, tril_ref[...], carry_ref))
    aq_ref[...] = (_dot(cp, pq_ref[...]) + oq_ref[...]).astype(BF16)
    ak_ref[...] = (_dot(cp, pk_ref[...]) + ok_ref[...]).astype(BF16)


def _in_proj(h, g1, w, bfp, c0, nb, seq, tm):
    t = nb * seq
    ns = seq // tm
    tril = jnp.asarray(np.tril(np.ones((tm, tm), np.float32)), BF16)
    pq, pk, oq, ok = _placement(-1.0)
    row = lambda b, s: (b * ns + s, 0)
    const = lambda b, s: (0, 0)
    blk = lambda w_: pl.BlockSpec((tm, w_), row)
    full = lambda a: pl.BlockSpec(a.shape, const)
    consts = [jnp.asarray(pq, BF16), jnp.asarray(pk, BF16), jnp.asarray(oq), jnp.asarray(ok)]
    outs = [(256, BF16), (256, F32), (256, F32), (256, BF16), (256, BF16), (LANES, F32),
            (256, BF16), (256, BF16), (1280, F32)]
    return pl.pallas_call(
        _in_proj_kernel,
        grid=(nb, ns),
        in_specs=[blk(D_MODEL), full(g1), full(w), full(bfp),
                  pl.BlockSpec((1, 1, LANES), lambda b, s: (b, 0, 0)), full(tril)] + [full(a) for a in consts],
        out_specs=[blk(w_) for w_, _ in outs],
        out_shape=[jax.ShapeDtypeStruct((t, w_), dt) for w_, dt in outs],
        scratch_shapes=[pltpu.VMEM((1, LANES), F32)],
        compiler_params=_params(("arbitrary", "arbitrary")),
        name="in_proj",
    )(h, g1, w, bfp, c0, tril, *consts)


def _cache_sum_kernel(lf_ref, tril_ref, pk_ref, ok_ref, ak_ref, tot_ref, carry_ref):
    @pl.when(pl.program_id(1) == 0)
    def _():
        carry_ref[...] = jnp.zeros_like(carry_ref)

    cp = _split3(_running_sum(lf_ref[...], tril_ref[...], carry_ref))
    ak_ref[...] = (_dot(cp, pk_ref[...]) + ok_ref[...]).astype(BF16)
    tot_ref[0] = carry_ref[...]


def _cache_sum(lfp, nb, past, tm):
    ns = past // tm
    tril = jnp.asarray(np.tril(np.ones((tm, tm), np.float32)), BF16)
    _, pk, _, ok = _placement(-1.0)
    pk, ok = jnp.asarray(pk, BF16), jnp.asarray(ok)
    const = lambda b, s: (0, 0)
    return pl.pallas_call(
        _cache_sum_kernel,
        grid=(nb, ns),
        in_specs=[pl.BlockSpec((tm, LANES), lambda b, s: (b * ns + s, 0)), pl.BlockSpec(tril.shape, const),
                  pl.BlockSpec(pk.shape, const), pl.BlockSpec(ok.shape, const)],
        out_specs=[pl.BlockSpec((tm, 256), lambda b, s: (b * ns + s, 0)),
                   pl.BlockSpec((1, 1, LANES), lambda b, s: (b, 0, 0))],
        out_shape=[jax.ShapeDtypeStruct((nb * past, 256), BF16), jax.ShapeDtypeStruct((nb, 1, LANES), F32)],
        scratch_shapes=[pltpu.VMEM((1, LANES), F32)],
        compiler_params=_params(("arbitrary", "arbitrary")),
        name="cache_sum",
    )(lfp, tril, pk, ok)


def _attn_kernel(qi_tab, ki_tab, vi_tab, fl_tab, q_ref, aq_ref, k_ref, ak_ref, v_ref, o_ref, m_ref, l_ref, acc_ref,
                 *, tq, tk, past):
    step = pl.program_id(2)
    qi = qi_tab[step]
    ki = ki_tab[step]
    fl = fl_tab[step]
    second = (fl & 1) != 0
    masked_tile = (fl & 8) != 0

    @pl.when((fl & 2) != 0)
    def _():
        m_ref[...] = jnp.full_like(m_ref, -jnp.inf)
        l_ref[...] = jnp.zeros_like(l_ref)
        acc_ref[...] = jnp.zeros_like(acc_ref)

    lane = lax.broadcasted_iota(jnp.int32, (1, LANES), 1)

    def scores(j, masked):
        zero = jnp.zeros((), BF16)
        qm = jnp.where((lane >= HEAD_DIM * j) & (lane < HEAD_DIM * (j + 1)), q_ref[...], zero)
        am = jnp.where((lane >= AUG_PER_HEAD * j) & (lane < AUG_PER_HEAD * (j + 1)), aq_ref[...], zero)
        lhs = jnp.concatenate([qm, am], axis=1)
        rhs = jnp.concatenate([k_ref[...], ak_ref[...]], axis=1)
        s = lax.dot_general(lhs, rhs, (((1,), (1,)), ((), ())), preferred_element_type=F32)
        if masked:
            row = lax.broadcasted_iota(jnp.int32, (tq, tk), 0)
            col = lax.broadcasted_iota(jnp.int32, (tq, tk), 1)
            s = jnp.where((ki * tk + col) <= (past + qi * tq + row), s, -jnp.inf)
        return s

    def stats(masked):
        for j in range(2):
            s = scores(j, masked)
            m_old = m_ref[j]
            m_new = jnp.maximum(m_old, jnp.max(s, axis=1, keepdims=True))
            l_ref[j] = jnp.exp(m_old - m_new) * l_ref[j] + jnp.sum(jnp.exp(s - m_new), axis=1, keepdims=True)
            m_ref[j] = m_new

    def accumulate(masked):
        v = v_ref[...]
        for j in range(2):
            p = jnp.exp(scores(j, masked) - m_ref[j]) * (1.0 / l_ref[j])
            acc_ref[j] += _dot(p.astype(BF16), v)

    for is_second, fn in ((False, stats), (True, accumulate)):
        for masked in (False, True):
            pl.when((second == is_second) & (masked_tile == masked))(functools.partial(fn, masked))

    @pl.when((fl & 4) != 0)
    def _():
        o_ref[...] = jnp.where(lane < HEAD_DIM, acc_ref[0], acc_ref[1]).astype(BF16)


def _attention(qs, aq, kb, ak, vb, nb, lq, lk, tq, tk):
    past = lk - lq
    nq, nk = lq // tq, lk // tk
    steps = []
    for qi in range(nq):
        kmax = min((past + (qi + 1) * tq - 1) // tk, nk - 1)
        for second in (0, 1):
            for ki in range(kmax + 1):
                masked = ki * tk + tk - 1 > past + qi * tq
                first = second == 0 and ki == 0
                last = second == 1 and ki == kmax
                steps.append((qi, ki, ki if second else 0, second | (first << 1) | (last << 2) | (masked << 3)))
    tabs = [jnp.asarray(np.array([st[i] for st in steps], np.int32)) for i in range(4)]
    qmap = lambda b, p, s, qt, kt, vt, ft: (b * nq + qt[s], p)
    kmap = lambda b, p, s, qt, kt, vt, ft: (b * nk + kt[s], p)
    vmap = lambda b, p, s, qt, kt, vt, ft: (b * nk + vt[s], p)
    grid_spec = pltpu.PrefetchScalarGridSpec(
        num_scalar_prefetch=4,
        grid=(nb, 2, len(steps)),
        in_specs=[pl.BlockSpec((tq, LANES), qmap), pl.BlockSpec((tq, LANES), qmap),
                  pl.BlockSpec((tk, LANES), kmap), pl.BlockSpec((tk, LANES), kmap), pl.BlockSpec((tk, LANES), vmap)],
        out_specs=pl.BlockSpec((tq, LANES), qmap),
        scratch_shapes=[pltpu.VMEM((2, tq, 1), F32), pltpu.VMEM((2, tq, 1), F32), pltpu.VMEM((2, tq, LANES), F32)],
    )
    return pl.pallas_call(
        functools.partial(_attn_kernel, tq=tq, tk=tk, past=past),
        grid_spec=grid_spec,
        out_shape=jax.ShapeDtypeStruct((nb * lq, 256), BF16),
        compiler_params=_params(("arbitrary", "arbitrary", "arbitrary")),
        name="attn",
    )(*tabs, qs, aq, kb, ak, vb)


def _gelu(x):
    return 0.5 * x * (1.0 + jnp.tanh(0.7978845608028654 * (x + 0.044715 * (x * x * x))))


def _mixers_kernel(ca_ref, cg_ref, px_ref, su_ref, sv_ref, cprev_ref, pprev_ref, dw_ref, dwb_ref, lng_ref, lnb_ref,
                   pw_ref, plw_ref, psc_ref, sgw_ref, sgb_ref,
                   bcd_ref, cst_ref, pst_ref, sgv_ref, xc_ref, xp_ref, *, ts, tc, pos0):
    s = pl.program_id(1)

    @pl.when(s == 0)
    def _():
        xc_ref[0:32, :] = cprev_ref[0].astype(BF16).astype(F32)
        xp_ref[0:16, :] = pprev_ref[0]

    @pl.when(s > 0)
    def _():
        xc_ref[0:32, :] = xc_ref[ts:ts + 32, :]
        xp_ref[0:16, :] = xp_ref[ts:ts + 16, :]

    g = cg_ref[...]
    u = ca_ref[...] * (1.0 / (1.0 + jnp.exp(-g)))
    xc_ref[32:32 + ts, :] = u.astype(BF16).astype(F32)
    y = jnp.zeros((ts, GROUP_W), F32)
    for j in range(CONV_W):
        y = y + xc_ref[2 + j:2 + j + ts, :] * dw_ref[j:j + 1, :]
    y = y + dwb_ref[...]
    mu = jnp.mean(y, axis=-1, keepdims=True)
    var = jnp.mean(jnp.square(y - mu), axis=-1, keepdims=True)
    y = (y - mu) * lax.rsqrt(var + EPS) * lng_ref[...] + lnb_ref[...]
    y = y * (1.0 / (1.0 + jnp.exp(-y)))
    bcd_ref[:, 0:256] = _dot(y.astype(BF16), pw_ref[...]).astype(BF16)
    cst_ref[0] = u[ts - 32:ts, :]

    x = px_ref[...]
    xp_ref[16:16 + ts, :] = x
    lane = lax.broadcasted_iota(jnp.int32, (ts, GROUP_W), 1)
    pos = (pos0 + s * ts + lax.broadcasted_iota(jnp.int32, (ts, 1), 0)).astype(F32)
    acc = x
    mean = jnp.zeros((ts, GROUP_W), F32)
    wi = 0
    for i in range(1, POOL_WINDOWS[-1] + 1):
        if i > 1:
            acc = acc + xp_ref[16 - (i - 1):16 - (i - 1) + ts, :]
        if i == POOL_WINDOWS[wi]:
            cnt = jnp.minimum(pos + 1.0, float(i))
            mean = jnp.where((lane >= 64 * wi) & (lane < 64 * (wi + 1)), acc / cnt, mean)
            wi += 1
    dlt = (mean - x).astype(BF16)
    bcd_ref[:, 256:512] = (_dot(dlt, plw_ref[...]) * psc_ref[...]).astype(BF16)
    pst_ref[0] = xp_ref[ts:ts + 16, :]

    r = lax.broadcasted_iota(jnp.int32, (tc, tc), 0)
    c = lax.broadcasted_iota(jnp.int32, (tc, tc), 1)
    lane_c = lax.broadcasted_iota(jnp.int32, (tc, GROUP_W), 1)
    wts = [jnp.where(c <= r, sgw_ref[gi], 0.0).astype(BF16) for gi in range(SG_GROUPS)]
    for ci in range(ts // tc):
        uu = _gelu(su_ref[ci * tc:(ci + 1) * tc, :])
        vv = _gelu(sv_ref[ci * tc:(ci + 1) * tc, :])
        if sgv_ref is not None:
            sgv_ref[ci * tc:(ci + 1) * tc, :] = vv
        m = sgb_ref[...]
        for gi in range(SG_GROUPS):
            vg = jnp.where((lane_c >= 64 * gi) & (lane_c < 64 * (gi + 1)), vv, 0.0).astype(BF16)
            m = m + _dot(wts[gi], vg)
        bcd_ref[ci * tc:(ci + 1) * tc, 512:768] = (uu * m).astype(BF16)


def _mixers_kernel_nosgv(*refs, **kw):
    _mixers_kernel(*refs[:19], None, *refs[19:], **kw)


def _mixers(zc, cprev, pprev, lw, nb, seq, ts, tc, pos0, want_sgv):
    t = nb * seq
    ns = seq // ts
    zblk = lambda j: pl.BlockSpec((ts, GROUP_W), lambda b, s: (b * ns + s, j))
    const2 = lambda a: pl.BlockSpec(a.shape, lambda b, s: (0,) * a.ndim)
    perb = lambda r: pl.BlockSpec((1, r, GROUP_W), lambda b, s: (b, 0, 0))
    small = [lw['dw'], lw['dwb'], lw['lng'], lw['lnb'], lw['pw'], lw['plw'], lw['psc'], lw['sgw'], lw['sgb']]
    out_specs = [pl.BlockSpec((ts, 768), lambda b, s: (b * ns + s, 0)), perb(32), perb(16)]
    out_shape = [jax.ShapeDtypeStruct((t, 768), BF16), jax.ShapeDtypeStruct((nb, 32, GROUP_W), F32),
                 jax.ShapeDtypeStruct((nb, 16, GROUP_W), F32)]
    if want_sgv:
        out_specs.append(pl.BlockSpec((ts, GROUP_W), lambda b, s: (b * ns + s, 0)))
        out_shape.append(jax.ShapeDtypeStruct((t, GROUP_W), F32))
    body = _mixers_kernel if want_sgv else _mixers_kernel_nosgv
    return pl.pallas_call(
        functools.partial(body, ts=ts, tc=tc, pos0=pos0),
        grid=(nb, ns),
        in_specs=[zblk(j) for j in range(5)] + [perb(32), perb(16)] + [const2(a) for a in small],
        out_specs=out_specs,
        out_shape=out_shape,
        scratch_shapes=[pltpu.VMEM((32 + ts, GROUP_W), F32), pltpu.VMEM((16 + ts, GROUP_W), F32)],
        compiler_params=_params(("arbitrary", "arbitrary")),
        name="mixers",
    )(zc, zc, zc, zc, zc, cprev, pprev, *small)


def _out_proj_kernel(a_ref, bcd_ref, h_ref, wo_ref, g_ref, *rest, moe):
    if moe:
        router_ref, h1_ref, hn_ref, dg_ref = rest
    else:
        h1_ref, hn_ref = rest
    mix = _dot(a_ref[...], wo_ref[0:256, :]) + _dot(bcd_ref[...], wo_ref[256:1024, :])
    h1 = h_ref[...] + mix
    hn = h1 * lax.rsqrt(jnp.mean(h1 * h1, axis=-1, keepdims=True) + EPS) * g_ref[...]
    h1_ref[...] = h1
    hn_ref[...] = hn.astype(BF16)
    if moe:
        logits = _dot(hn.astype(BF16), router_ref[...])
        lane = lax.broadcasted_iota(jnp.int32, logits.shape, 1)
        logits = jnp.where(lane < N_EXPERTS, logits, -jnp.inf)
        v1 = jnp.max(logits, axis=1, keepdims=True)
        i1 = jnp.min(jnp.where(logits == v1, lane, LANES), axis=1, keepdims=True)
        rest_l = jnp.where(lane == i1, -jnp.inf, logits)
        v2 = jnp.max(rest_l, axis=1, keepdims=True)
        i2 = jnp.min(jnp.where(rest_l == v2, lane, LANES), axis=1, keepdims=True)
        e = jnp.exp(v2 - v1)
        g1 = 1.0 / (1.0 + e)
        dg_ref[...] = jnp.where(lane == i1, g1, jnp.where(lane == i2, e * g1, 0.0))


def _out_proj(a, bcd, h, wo, g2, router, tm):
    t = h.shape[0]
    moe = router is not None
    row = lambda i: (i, 0)
    const = lambda i: (0, 0)
    ins = [a, bcd, h, wo, g2] + ([router] if moe else [])
    in_specs = [pl.BlockSpec((tm, 256), row), pl.BlockSpec((tm, 768), row), pl.BlockSpec((tm, D_MODEL), row),
                pl.BlockSpec(wo.shape, const), pl.BlockSpec(g2.shape, const)]
    out_specs = [pl.BlockSpec((tm, D_MODEL), row), pl.BlockSpec((tm, D_MODEL), row)]
    out_shape = [jax.ShapeDtypeStruct((t, D_MODEL), F32), jax.ShapeDtypeStruct((t, D_MODEL), BF16)]
    if moe:
        in_specs.append(pl.BlockSpec(router.shape, const))
        out_specs.append(pl.BlockSpec((tm, LANES), row))
        out_shape.append(jax.ShapeDtypeStruct((t, LANES), F32))
    return pl.pallas_call(
        functools.partial(_out_proj_kernel, moe=moe),
        grid=(t // tm,),
        in_specs=in_specs, out_specs=out_specs, out_shape=out_shape,
        compiler_params=_params(("arbitrary",)),
        name="out_proj",
    )(*ins)


def _ffn_kernel(te_ref, tr_ref, x_ref, wg_ref, wu_ref, wd_ref, y_ref):
    x = x_ref[...]
    gate = _dot(x, wg_ref[0])
    up = _dot(x, wu_ref[0])
    act = (gate * (1.0 / (1.0 + jnp.exp(-gate))) * up).astype(BF16)
    y_ref[...] = _dot(act, wd_ref[0])


def _ffn(xs, wg, wu, wd, tile_e, tile_r, n_out_rows, tm):
    d_ff = wg.shape[-1]
    grid_spec = pltpu.PrefetchScalarGridSpec(
        num_scalar_prefetch=2,
        grid=(len(tile_e),),
        in_specs=[pl.BlockSpec((tm, D_MODEL), lambda i, te, tr: (tr[i], 0)),
                  pl.BlockSpec((1, D_MODEL, d_ff), lambda i, te, tr: (te[i], 0, 0)),
                  pl.BlockSpec((1, D_MODEL, d_ff), lambda i, te, tr: (te[i], 0, 0)),
                  pl.BlockSpec((1, d_ff, D_MODEL), lambda i, te, tr: (te[i], 0, 0))],
        out_specs=pl.BlockSpec((tm, D_MODEL), lambda i, te, tr: (i, 0)),
    )
    return pl.pallas_call(
        _ffn_kernel,
        grid_spec=grid_spec,
        out_shape=jax.ShapeDtypeStruct((n_out_rows, D_MODEL), F32),
        compiler_params=_params(("arbitrary",)),
        name="ffn",
    )(jnp.asarray(tile_e, jnp.int32), jnp.asarray(tile_r, jnp.int32), xs, wg, wu, wd)


def _ple_kernel(*refs, n_exp, final):
    h1_ref, y_ref = refs[0], refs[1]
    k = 2
    if n_exp:
        dg_ref = refs[k]
        k += 1
    p_ref, gw_ref, pw_ref = refs[k:k + 3]
    k += 3
    if final:
        fg_ref = refs[k]
        k += 1
    h_ref = refs[k]
    h2 = h1_ref[...]
    if n_exp:
        dg = dg_ref[...]
        lane = lax.broadcasted_iota(jnp.int32, dg.shape, 1)
        for e in range(n_exp):
            ge = jnp.sum(jnp.where(lane == e, dg, 0.0), axis=1, keepdims=True)
            h2 = h2 + ge * y_ref[e]
    else:
        h2 = h2 + y_ref[0]
    gate = _dot(h2.astype(BF16), gw_ref[...])
    pe = _dot(p_ref[...].astype(BF16), pw_ref[...])
    h3 = h2 + (1.0 / (1.0 + jnp.exp(-gate))) * pe
    h_ref[...] = h3
    if final:
        refs[k + 1][...] = h3 * lax.rsqrt(jnp.mean(h3 * h3, axis=-1, keepdims=True) + EPS) * fg_ref[...]


def _ple(h1, ys, dg, p, gw, pw, fg, tm):
    t = h1.shape[0]
    n_exp = 0 if dg is None else ys.shape[0]
    final = fg is not None
    row = lambda i: (i, 0)
    const = lambda i: (0, 0)
    ins = [h1, ys]
    in_specs = [pl.BlockSpec((tm, D_MODEL), row), pl.BlockSpec((ys.shape[0], tm, D_MODEL), lambda i: (0, i, 0))]
    if n_exp:
        ins.append(dg)
        in_specs.append(pl.BlockSpec((tm, LANES), row))
    ins += [p, gw, pw]
    in_specs += [pl.BlockSpec((tm, 256), row), pl.BlockSpec(gw.shape, const), pl.BlockSpec(pw.shape, const)]
    out_specs = [pl.BlockSpec((tm, D_MODEL), row)]
    out_shape = [jax.ShapeDtypeStruct((t, D_MODEL), F32)]
    if final:
        ins.append(fg)
        in_specs.append(pl.BlockSpec(fg.shape, const))
        out_specs.append(pl.BlockSpec((tm, D_MODEL), row))
        out_shape.append(jax.ShapeDtypeStruct((t, D_MODEL), F32))
    return pl.pallas_call(
        functools.partial(_ple_kernel, n_exp=n_exp, final=final),
        grid=(t // tm,),
        in_specs=in_specs, out_specs=out_specs, out_shape=out_shape,
        compiler_params=_params(("arbitrary",)),
        name="ple",
    )(*ins)


def _tile(n, pref):
    return pref if n % pref == 0 else n


def _layer(h, p_l, lw, nb, seq, cache, cprev, pprev, final_g):
    t = nb * seq
    tm = _tile(seq, 512)
    if cache is None:
        c0 = jnp.zeros((nb, 1, LANES), F32)
        past = 0
    else:
        ck, cv, clf = cache
        past = ck.shape[1]
        akc, c0 = _cache_sum(clf, nb, past, _tile(past, 512))
    qs, kf, vf, kb, vb, lf, aq, ak, zc = _in_proj(h, lw['g1'], lw['w_in'], lw['bf'], c0, nb, seq, tm)
    if cache is None:
        a = _attention(qs, aq, kb, ak, vb, nb, seq, seq, tm, tm)
    else:
        lk = past + seq
        cat = lambda c_, n_: jnp.concatenate([c_.reshape(nb, past, 256), n_.reshape(nb, seq, 256)], axis=1
                                             ).reshape(nb * lk, 256)
        a = _attention(qs, aq, cat(ck, kb), cat(akc, ak), cat(cv, vb), nb, seq, lk, seq, lk)
    tc = min(seq, 128)
    outs = _mixers(zc, cprev, pprev, lw, nb, seq, tm, tc, past, cache is not None)
    bcd, cst, pst = outs[:3]
    sgv = outs[3] if cache is not None else None
    tmo = _tile(t, 512)
    moe = 'router' in lw
    res = _out_proj(a, bcd, h, lw['w_o'], lw['g2'], lw.get('router'), tmo)
    h1, hn = res[0], res[1]
    tf = _tile(t, 256)
    nt = t // tf
    if moe:
        tile_e = np.repeat(np.arange(N_EXPERTS), nt)
        tile_r = np.tile(np.arange(nt), N_EXPERTS)
        ys = _ffn(hn, lw['wg'], lw['wu'], lw['wd'], tile_e, tile_r, N_EXPERTS * t, tf).reshape(N_EXPERTS, t, D_MODEL)
        dg = res[2]
    else:
        ys = _ffn(hn, lw['wg'], lw['wu'], lw['wd'], np.zeros(nt, np.int64), np.arange(nt), t, tf).reshape(1, t, D_MODEL)
        dg = None
    hres = _ple(h1, ys, dg, p_l, lw['gw'], lw['plw_e'], final_g, _tile(t, 256))
    return hres, kf, vf, lf[:, :N_HEADS], cst[:, 2:], pst[:, 1:], sgv


def kernel(x_prompt, x_sample, p_prompt, p_sample, cache_k, cache_v, cache_logf, state_conv, state_pool, norm1_g, w_in, b_f, conv_dw_w, conv_dw_b, conv_ln_g, conv_ln_b, conv_pw_w, pool_w, pool_scale, sg_w, sg_b, w_o, norm2_g, ffn_w_gate, ffn_w_up, ffn_w_down, moe_router, moe_w_gate, moe_w_up, moe_w_down, ple_w, ple_gate_w, final_g):
    depth = w_in.shape[0]
    bp, sp, _ = x_prompt.shape
    bs, ss, _ = x_sample.shape
    past = cache_k.shape[2]
    hp = x_prompt.reshape(bp * sp, D_MODEL)
    hs = x_sample.reshape(bs * ss, D_MODEL)
    fg = final_g.reshape(1, D_MODEL)
    outs_p, outs_s = [], []
    for l in range(depth):
        j = l // 2
        wl = w_in[l]
        w_r = jnp.concatenate([wl[:, 0:768], wl[:, 772:2052], jnp.pad(wl[:, 768:772], ((0, 0), (0, LANES - N_HEADS)))],
                              axis=1).astype(BF16)
        plw = jnp.zeros((GROUP_W, GROUP_W), F32)
        for gi in range(len(POOL_WINDOWS)):
            plw = plw.at[64 * gi:64 * (gi + 1), 64 * gi:64 * (gi + 1)].set(pool_w[l, gi])
        lw = {
            'g1': norm1_g[l].reshape(1, D_MODEL), 'w_in': w_r,
            'bf': jnp.pad(b_f[l], (0, LANES - N_HEADS)).reshape(1, LANES),
            'dw': jnp.pad(conv_dw_w[l], ((0, 1), (0, 0))), 'dwb': conv_dw_b[l].reshape(1, GROUP_W),
            'lng': conv_ln_g[l].reshape(1, GROUP_W), 'lnb': conv_ln_b[l].reshape(1, GROUP_W),
            'pw': conv_pw_w[l].astype(BF16), 'plw': plw.astype(BF16), 'psc': pool_scale[l].reshape(1, GROUP_W),
            'w_o': w_o[l].astype(BF16), 'g2': norm2_g[l].reshape(1, D_MODEL),
            'gw': ple_gate_w[l].astype(BF16), 'plw_e': ple_w[l].astype(BF16),
        }
        if l % 2 == 0:
            lw.update(wg=ffn_w_gate[j][None].astype(BF16), wu=ffn_w_up[j][None].astype(BF16),
                      wd=ffn_w_down[j][None].astype(BF16))
        else:
            lw.update(wg=moe_w_gate[j].astype(BF16), wu=moe_w_up[j].astype(BF16), wd=moe_w_down[j].astype(BF16),
                      router=jnp.pad(moe_router[j], ((0, 0), (0, LANES - N_EXPERTS))).astype(BF16))
        last = fg if l == depth - 1 else None

        def sg_params(tc):
            return {'sgw': sg_w[l][:, :tc, :tc],
                    'sgb': jnp.repeat(sg_b[l][:, :tc].T, GROUP_W // SG_GROUPS, axis=1)}

        lwp = dict(lw, **sg_params(min(sp, 128)))
        res = _layer(hp, p_prompt[l].reshape(bp * sp, -1), lwp, bp, sp, None,
                     jnp.zeros((bp, 32, GROUP_W), F32), jnp.zeros((bp, 16, GROUP_W), F32), last)
        hp_res, outs = res[0], res[1:]
        outs_p.append(outs)
        lws = dict(lw, **sg_params(min(ss, 128)))
        cache = (cache_k[l].reshape(bs, past, 256).astype(BF16), cache_v[l].reshape(bs, past, 256).astype(BF16),
                 jnp.pad(cache_logf[l], ((0, 0), (0, 0), (0, LANES - N_HEADS))).reshape(bs * past, LANES))
        res = _layer(hs, p_sample[l].reshape(bs * ss, -1), lws, bs, ss, cache,
                     jnp.pad(state_conv[l], ((0, 0), (2, 0), (0, 0))), jnp.pad(state_pool[l], ((0, 0), (1, 0), (0, 0))),
                     last)
        hs_res, outs = res[0], res[1:]
        outs_s.append(outs)
        if last is None:
            hp, hs = hp_res[0], hs_res[0]
        else:
            y_prompt, y_sample = hp_res[1], hs_res[1]

    def stack(outs, i, shape):
        return jnp.stack([o[i] for o in outs]).reshape((depth,) + shape)

    return (y_prompt.reshape(bp, sp, D_MODEL), y_sample.reshape(bs, ss, D_MODEL),
            stack(outs_p, 0, (bp, sp, N_HEADS, HEAD_DIM)), stack(outs_p, 1, (bp, sp, N_HEADS, HEAD_DIM)),
            stack(outs_p, 2, (bp, sp, N_HEADS)), stack(outs_p, 3, (bp, CONV_W - 1, GROUP_W)),
            stack(outs_p, 4, (bp, POOL_HIST, GROUP_W)),
            stack(outs_s, 0, (bs, ss, N_HEADS, HEAD_DIM)), stack(outs_s, 1, (bs, ss, N_HEADS, HEAD_DIM)),
            stack(outs_s, 2, (bs, ss, N_HEADS)), stack(outs_s, 3, (bs, CONV_W - 1, GROUP_W)),
            stack(outs_s, 4, (bs, POOL_HIST, GROUP_W)), stack(outs_s, 5, (bs, ss, GROUP_W)))
```

```python
import functools

import numpy as np
import jax
import jax.numpy as jnp
from jax import lax
from jax.experimental import pallas as pl
from jax.experimental.pallas import tpu as pltpu

F32 = jnp.float32
BF16 = jnp.bfloat16
EPS = 1e-6

D_MODEL = 1024
GROUP_W = 256
N_HEADS = 4
HEAD_DIM = 64
CONV_W = 31
POOL_WINDOWS = (2, 4, 8, 16)
POOL_HIST = 15
SG_GROUPS = 4
N_EXPERTS = 8
LANES = 128
AUG_PER_HEAD = 6
VMEM_LIMIT = 56 * 1024 * 1024


def _params(sem, vmem=VMEM_LIMIT):
    return pltpu.CompilerParams(dimension_semantics=sem, vmem_limit_bytes=vmem)


def _dot(a, b):
    return jnp.dot(a, b, preferred_element_type=F32)


def _split3(x):
    hi = x.astype(BF16)
    r1 = x - hi.astype(F32)
    mid = r1.astype(BF16)
    lo = (r1 - mid.astype(F32)).astype(BF16)
    return jnp.concatenate([hi, mid, lo], axis=1)


def _running_sum(x, tril, carry_ref):
    cs = _dot(tril, _split3(x))
    c = carry_ref[...] + (cs[:, :LANES] + cs[:, LANES:2 * LANES] + cs[:, 2 * LANES:])
    n = x.shape[0]
    carry_ref[...] = c[n - 1:n, :]
    return c


def _placement(sign_k):
    pq = np.zeros((3 * LANES, 2 * LANES), np.float32)
    pk = np.zeros((3 * LANES, 2 * LANES), np.float32)
    oq = np.zeros((1, 2 * LANES), np.float32)
    ok = np.zeros((1, 2 * LANES), np.float32)
    for h in range(N_HEADS):
        p, j = divmod(h, 2)
        for i in range(3):
            pq[i * LANES + h, p * LANES + j * AUG_PER_HEAD + i] = 1.0
            oq[0, p * LANES + j * AUG_PER_HEAD + 3 + i] = 1.0
            pk[i * LANES + h, p * LANES + j * AUG_PER_HEAD + 3 + i] = sign_k
            ok[0, p * LANES + j * AUG_PER_HEAD + i] = 1.0
    return pq, pk, oq, ok


def _in_proj_kernel(h_ref, g_ref, w_ref, bf_ref, c0_ref, tril_ref, pq_ref, pk_ref, oq_ref, ok_ref,
                    qs_ref, kf_ref, vf_ref, kb_ref, vb_ref, lf_ref, aq_ref, ak_ref, zc_ref, carry_ref):
    @pl.when(pl.program_id(1) == 0)
    def _():
        carry_ref[...] = c0_ref[0]

    x = h_ref[...]
    hn = x * lax.rsqrt(jnp.mean(x * x, axis=-1, keepdims=True) + EPS) * g_ref[...]
    hb = hn.astype(BF16)
    qs_ref[...] = (_dot(hb, w_ref[:, 0:256]) * (HEAD_DIM ** -0.5)).astype(BF16)
    k = _dot(hb, w_ref[:, 256:512])
    kf_ref[...] = k
    kb_ref[...] = k.astype(BF16)
    v = _dot(hb, w_ref[:, 512:768])
    vf_ref[...] = v
    vb_ref[...] = v.astype(BF16)
    zc_ref[...] = _dot(hb, w_ref[:, 768:2048])
    zf = _dot(hb, w_ref[:, 2048:2176]) + bf_ref[...]
    lf = jnp.minimum(zf, 0.0) - jnp.log1p(jnp.exp(-jnp.abs(zf)))
    lane = lax.broadcasted_iota(jnp.int32, lf.shape, 1)
    lf = jnp.where(lane < N_HEADS, lf, 0.0)
    lf_ref[...] = lf
    cp = _split3(_running_sum(lf, tril_ref[...], carry_ref))
    aq_ref[...] = (_dot(cp, pq_ref[...]) + oq_ref[...]).astype(BF16)
    ak_ref[...] = (_dot(cp, pk_ref[...]) + ok_ref[...]).astype(BF16)


def _in_proj(h, g1, w, bfp, c0, nb, seq, tm):
    t = nb * seq
    ns = seq // tm
    tril = jnp.asarray(np.tril(np.ones((tm, tm), np.float32)), BF16)
    pq, pk, oq, ok = _placement(-1.0)
    row = lambda b, s: (b * ns + s, 0)
    const = lambda b, s: (0, 0)
    blk = lambda w_: pl.BlockSpec((tm, w_), row)
    full = lambda a: pl.BlockSpec(a.shape, const)
    consts = [jnp.asarray(pq, BF16), jnp.asarray(pk, BF16), jnp.asarray(oq), jnp.asarray(ok)]
    outs = [(256, BF16), (256, F32), (256, F32), (256, BF16), (256, BF16), (LANES, F32),
            (256, BF16), (256, BF16), (1280, F32)]
    return pl.pallas_call(
        _in_proj_kernel,
        grid=(nb, ns),
        in_specs=[blk(D_MODEL), full(g1), full(w), full(bfp),
                  pl.BlockSpec((1, 1, LANES), lambda b, s: (b, 0, 0)), full(tril)] + [full(a) for a in consts],
        out_specs=[blk(w_) for w_, _ in outs],
        out_shape=[jax.ShapeDtypeStruct((t, w_), dt) for w_, dt in outs],
        scratch_shapes=[pltpu.VMEM((1, LANES), F32)],
        compiler_params=_params(("arbitrary", "arbitrary")),
        name="in_proj",
    )(h, g1, w, bfp, c0, tril, *consts)


def _cache_sum_kernel(lf_ref, tril_ref, pk_ref, ok_ref, ak_ref, tot_ref, carry_ref):
    @pl.when(pl.program_id(1) == 0)
    def _():
        carry_ref[...] = jnp.zeros_like(carry_ref)

    cp = _split3(_running_sum(lf_ref[...], tril_ref[...], carry_ref))
    ak_ref[...] = (_dot(cp, pk_ref[...]) + ok_ref[...]).astype(BF16)
    tot_ref[0] = carry_ref[...]


def _cache_sum(lfp, nb, past, tm):
    ns = past // tm
    tril = jnp.asarray(np.tril(np.ones((tm, tm), np.float32)), BF16)
    _, pk, _, ok = _placement(-1.0)
    pk, ok = jnp.asarray(pk, BF16), jnp.asarray(ok)
    const = lambda b, s: (0, 0)
    return pl.pallas_call(
        _cache_sum_kernel,
        grid=(nb, ns),
        in_specs=[pl.BlockSpec((tm, LANES), lambda b, s: (b * ns + s, 0)), pl.BlockSpec(tril.shape, const),
                  pl.BlockSpec(pk.shape, const), pl.BlockSpec(ok.shape, const)],
        out_specs=[pl.BlockSpec((tm, 256), lambda b, s: (b * ns + s, 0)),
                   pl.BlockSpec((1, 1, LANES), lambda b, s: (b, 0, 0))],
        out_shape=[jax.ShapeDtypeStruct((nb * past, 256), BF16), jax.ShapeDtypeStruct((nb, 1, LANES), F32)],
        scratch_shapes=[pltpu.VMEM((1, LANES), F32)],
        compiler_params=_params(("arbitrary", "arbitrary")),
        name="cache_sum",
    )(lfp, tril, pk, ok)


NEG_BIG = -1e30


def _head_lhs(q, aq, j, lane):
    zero = jnp.zeros((), BF16)
    qm = jnp.where((lane >= HEAD_DIM * j) & (lane < HEAD_DIM * (j + 1)), q, zero)
    am = jnp.where((lane >= AUG_PER_HEAD * j) & (lane < AUG_PER_HEAD * (j + 1)), aq, zero)
    return jnp.concatenate([qm, am], axis=1)


def _scores(lhs, k, ak):
    rhs = jnp.concatenate([k, ak], axis=1)
    return lax.dot_general(lhs, rhs, (((1,), (1,)), ((), ())), preferred_element_type=F32)


def _attn_kernel(q_ref, aq_ref, k_ref, ak_ref, v_ref, o_ref, m_sc, l_sc, acc_sc, *, t):
    qi = pl.program_id(1)
    lane = lax.broadcasted_iota(jnp.int32, (1, LANES), 1)
    row = lax.broadcasted_iota(jnp.int32, (t, t), 0)
    col = lax.broadcasted_iota(jnp.int32, (t, t), 1)
    nct = t // LANES
    for p in range(2):
        cs = slice(p * LANES, (p + 1) * LANES)
        for j in range(2):
            lhs = _head_lhs(q_ref[:, cs], aq_ref[:, cs], j, lane)

            def scores(ki, masked):
                rows = pl.ds(pl.multiple_of(ki * t, t), t)
                s = _scores(lhs, k_ref[rows, cs], ak_ref[rows, cs])
                if masked:
                    s = jnp.where(col <= row, s, -jnp.inf)
                return [s[:, c * LANES:(c + 1) * LANES] for c in range(nct)]

            def stats(ki, masked):
                parts = scores(ki, masked)
                m_old = m_sc[...]
                m_new = functools.reduce(jnp.maximum, parts, m_old)
                tot = l_sc[...] * jnp.exp(m_old - m_new)
                for part in parts:
                    tot = tot + jnp.exp(part - m_new)
                l_sc[...] = tot
                m_sc[...] = m_new

            def accumulate(ki, masked):
                parts = scores(ki, masked)
                m_rep = m_sc[...]
                r_rep = l_sc[...]
                prob = jnp.concatenate([(jnp.exp(part - m_rep) * r_rep).astype(BF16) for part in parts], axis=1)
                acc_sc[j] += _dot(prob, v_ref[pl.ds(pl.multiple_of(ki * t, t), t), cs])

            def sweep(fn):
                def body(ki, carry):
                    fn(ki, False)
                    return carry
                lax.fori_loop(0, qi, body, 0)
                fn(qi, True)

            m_sc[...] = jnp.full_like(m_sc, NEG_BIG)
            l_sc[...] = jnp.zeros_like(l_sc)
            sweep(stats)
            m_lane = m_sc[...]
            m_row = jnp.max(m_lane, axis=1, keepdims=True)
            l_row = jnp.sum(l_sc[...] * jnp.exp(m_lane - m_row), axis=1, keepdims=True)
            m_sc[...] = jnp.broadcast_to(m_row, m_lane.shape)
            l_sc[...] = jnp.broadcast_to(1.0 / l_row, m_lane.shape)
            acc_sc[j] = jnp.zeros((t, LANES), F32)
            sweep(accumulate)
        o_ref[:, cs] = jnp.where(lane < HEAD_DIM, acc_sc[0], acc_sc[1]).astype(BF16)


def _attention(qs, aq, kb, ak, vb, nb, seq, t):
    assert seq % t == 0 and t % LANES == 0
    nq = seq // t
    qblk = pl.BlockSpec((t, 256), lambda b, i: (b * nq + i, 0))
    kblk = pl.BlockSpec((seq, 256), lambda b, i: (b, 0))
    return pl.pallas_call(
        functools.partial(_attn_kernel, t=t),
        grid=(nb, nq),
        in_specs=[qblk, qblk, kblk, kblk, kblk],
        out_specs=qblk,
        out_shape=jax.ShapeDtypeStruct((nb * seq, 256), BF16),
        scratch_shapes=[pltpu.VMEM((t, LANES), F32), pltpu.VMEM((t, LANES), F32), pltpu.VMEM((2, t, LANES), F32)],
        compiler_params=_params(("arbitrary", "arbitrary")),
        name="attn",
    )(qs, aq, kb, ak, vb)


def _attn_rows_kernel(q_ref, aq_ref, k_ref, ak_ref, v_ref, o_ref, *, past):
    lq, lk = q_ref.shape[0], k_ref.shape[0]
    lane = lax.broadcasted_iota(jnp.int32, (1, LANES), 1)
    row = lax.broadcasted_iota(jnp.int32, (lq, lk), 0)
    col = lax.broadcasted_iota(jnp.int32, (lq, lk), 1)
    for p in range(2):
        cs = slice(p * LANES, (p + 1) * LANES)
        outs = []
        for j in range(2):
            s = _scores(_head_lhs(q_ref[:, cs], aq_ref[:, cs], j, lane), k_ref[:, cs], ak_ref[:, cs])
            s = jnp.where(col <= past + row, s, -jnp.inf)
            e = jnp.exp(s - jnp.max(s, axis=1, keepdims=True))
            prob = (e * (1.0 / jnp.sum(e, axis=1, keepdims=True))).astype(BF16)
            outs.append(_dot(prob, v_ref[:, cs]))
        o_ref[:, cs] = jnp.where(lane < HEAD_DIM, outs[0], outs[1]).astype(BF16)


def _attention_rows(qs, aq, kb, ak, vb, nb, lq, lk):
    qblk = pl.BlockSpec((lq, 256), lambda b: (b, 0))
    kblk = pl.BlockSpec((lk, 256), lambda b: (b, 0))
    return pl.pallas_call(
        functools.partial(_attn_rows_kernel, past=lk - lq),
        grid=(nb,),
        in_specs=[qblk, qblk, kblk, kblk, kblk],
        out_specs=qblk,
        out_shape=jax.ShapeDtypeStruct((nb * lq, 256), BF16),
        compiler_params=_params(("arbitrary",)),
        name="attn_rows",
    )(qs, aq, kb, ak, vb)


def _gelu(x):
    return 0.5 * x * (1.0 + jnp.tanh(0.7978845608028654 * (x + 0.044715 * (x * x * x))))


def _mixers_kernel(ca_ref, cg_ref, px_ref, su_ref, sv_ref, cprev_ref, pprev_ref, dw_ref, dwb_ref, lng_ref, lnb_ref,
                   pw_ref, plw_ref, psc_ref, sgw_ref, sgb_ref,
                   bcd_ref, cst_ref, pst_ref, sgv_ref, xc_ref, xp_ref, *, ts, tc, pos0):
    s = pl.program_id(1)

    @pl.when(s == 0)
    def _():
        xc_ref[0:32, :] = cprev_ref[0].astype(BF16).astype(F32)
        xp_ref[0:16, :] = pprev_ref[0]

    @pl.when(s > 0)
    def _():
        xc_ref[0:32, :] = xc_ref[ts:ts + 32, :]
        xp_ref[0:16, :] = xp_ref[ts:ts + 16, :]

    g = cg_ref[...]
    u = ca_ref[...] * (1.0 / (1.0 + jnp.exp(-g)))
    xc_ref[32:32 + ts, :] = u.astype(BF16).astype(F32)
    y = jnp.zeros((ts, GROUP_W), F32)
    for j in range(CONV_W):
        y = y + xc_ref[2 + j:2 + j + ts, :] * dw_ref[j:j + 1, :]
    y = y + dwb_ref[...]
    mu = jnp.mean(y, axis=-1, keepdims=True)
    var = jnp.mean(jnp.square(y - mu), axis=-1, keepdims=True)
    y = (y - mu) * lax.rsqrt(var + EPS) * lng_ref[...] + lnb_ref[...]
    y = y * (1.0 / (1.0 + jnp.exp(-y)))
    bcd_ref[:, 0:256] = _dot(y.astype(BF16), pw_ref[...]).astype(BF16)
    cst_ref[0] = u[ts - 32:ts, :]

    x = px_ref[...]
    xp_ref[16:16 + ts, :] = x
    lane = lax.broadcasted_iota(jnp.int32, (ts, GROUP_W), 1)
    pos = (pos0 + s * ts + lax.broadcasted_iota(jnp.int32, (ts, 1), 0)).astype(F32)
    acc = x
    mean = jnp.zeros((ts, GROUP_W), F32)
    wi = 0
    for i in range(1, POOL_WINDOWS[-1] + 1):
        if i > 1:
            acc = acc + xp_ref[16 - (i - 1):16 - (i - 1) + ts, :]
        if i == POOL_WINDOWS[wi]:
            cnt = jnp.minimum(pos + 1.0, float(i))
            mean = jnp.where((lane >= 64 * wi) & (lane < 64 * (wi + 1)), acc / cnt, mean)
            wi += 1
    dlt = (mean - x).astype(BF16)
    bcd_ref[:, 256:512] = (_dot(dlt, plw_ref[...]) * psc_ref[...]).astype(BF16)
    pst_ref[0] = xp_ref[ts:ts + 16, :]

    r = lax.broadcasted_iota(jnp.int32, (tc, tc), 0)
    c = lax.broadcasted_iota(jnp.int32, (tc, tc), 1)
    lane_c = lax.broadcasted_iota(jnp.int32, (tc, GROUP_W), 1)
    wts = [jnp.where(c <= r, sgw_ref[gi], 0.0).astype(BF16) for gi in range(SG_GROUPS)]
    for ci in range(ts // tc):
        uu = _gelu(su_ref[ci * tc:(ci + 1) * tc, :])
        vv = _gelu(sv_ref[ci * tc:(ci + 1) * tc, :])
        if sgv_ref is not None:
            sgv_ref[ci * tc:(ci + 1) * tc, :] = vv
        m = sgb_ref[...]
        for gi in range(SG_GROUPS):
            vg = jnp.where((lane_c >= 64 * gi) & (lane_c < 64 * (gi + 1)), vv, 0.0).astype(BF16)
            m = m + _dot(wts[gi], vg)
        bcd_ref[ci * tc:(ci + 1) * tc, 512:768] = (uu * m).astype(BF16)


def _mixers_kernel_nosgv(*refs, **kw):
    _mixers_kernel(*refs[:19], None, *refs[19:], **kw)


def _mixers(zc, cprev, pprev, lw, nb, seq, ts, tc, pos0, want_sgv):
    t = nb * seq
    ns = seq // ts
    zblk = lambda j: pl.BlockSpec((ts, GROUP_W), lambda b, s: (b * ns + s, j))
    const2 = lambda a: pl.BlockSpec(a.shape, lambda b, s: (0,) * a.ndim)
    perb = lambda r: pl.BlockSpec((1, r, GROUP_W), lambda b, s: (b, 0, 0))
    small = [lw['dw'], lw['dwb'], lw['lng'], lw['lnb'], lw['pw'], lw['plw'], lw['psc'], lw['sgw'], lw['sgb']]
    out_specs = [pl.BlockSpec((ts, 768), lambda b, s: (b * ns + s, 0)), perb(32), perb(16)]
    out_shape = [jax.ShapeDtypeStruct((t, 768), BF16), jax.ShapeDtypeStruct((nb, 32, GROUP_W), F32),
                 jax.ShapeDtypeStruct((nb, 16, GROUP_W), F32)]
    if want_sgv:
        out_specs.append(pl.BlockSpec((ts, GROUP_W), lambda b, s: (b * ns + s, 0)))
        out_shape.append(jax.ShapeDtypeStruct((t, GROUP_W), F32))
    body = _mixers_kernel if want_sgv else _mixers_kernel_nosgv
    return pl.pallas_call(
        functools.partial(body, ts=ts, tc=tc, pos0=pos0),
        grid=(nb, ns),
        in_specs=[zblk(j) for j in range(5)] + [perb(32), perb(16)] + [const2(a) for a in small],
        out_specs=out_specs,
        out_shape=out_shape,
        scratch_shapes=[pltpu.VMEM((32 + ts, GROUP_W), F32), pltpu.VMEM((16 + ts, GROUP_W), F32)],
        compiler_params=_params(("arbitrary", "arbitrary")),
        name="mixers",
    )(zc, zc, zc, zc, zc, cprev, pprev, *small)


R_GATE, R_EXPERT, R_RANK = 0, 2, 4


def _out_proj_kernel(a_ref, bcd_ref, h_ref, wo_ref, g_ref, *rest, moe):
    if moe:
        router_ref, stril_ref, h1_ref, hn_ref, route_ref, cnt_ref, carry_ref = rest
    else:
        h1_ref, hn_ref = rest
    mix = _dot(a_ref[...], wo_ref[0:256, :]) + _dot(bcd_ref[...], wo_ref[256:1024, :])
    h1 = h_ref[...] + mix
    hn = h1 * lax.rsqrt(jnp.mean(h1 * h1, axis=-1, keepdims=True) + EPS) * g_ref[...]
    h1_ref[...] = h1
    hn_ref[...] = hn.astype(hn_ref.dtype)
    if moe:
        @pl.when(pl.program_id(0) == 0)
        def _():
            carry_ref[...] = jnp.zeros_like(carry_ref)

        logits = _dot(hn.astype(BF16), router_ref[...])
        lane = lax.broadcasted_iota(jnp.int32, logits.shape, 1)
        logits = jnp.where(lane < N_EXPERTS, logits, -jnp.inf)
        v1 = jnp.max(logits, axis=1, keepdims=True)
        i1 = jnp.min(jnp.where(logits == v1, lane, LANES), axis=1, keepdims=True)
        rest_l = jnp.where(lane == i1, -jnp.inf, logits)
        v2 = jnp.max(rest_l, axis=1, keepdims=True)
        i2 = jnp.min(jnp.where(rest_l == v2, lane, LANES), axis=1, keepdims=True)
        e = jnp.exp(v2 - v1)
        g1 = 1.0 / (1.0 + e)
        oh1, oh2 = lane == i1, lane == i2
        oh = jnp.where(oh1 | oh2, 1.0, 0.0)
        before = _dot(stril_ref[...], oh.astype(BF16)) + carry_ref[...]
        r1 = jnp.sum(jnp.where(oh1, before, 0.0), axis=1, keepdims=True)
        r2 = jnp.sum(jnp.where(oh2, before, 0.0), axis=1, keepdims=True)
        carry_ref[...] += jnp.sum(oh, axis=0, keepdims=True)
        cnt_ref[...] = carry_ref[...]
        rec = jnp.zeros_like(logits)
        for ln, val in ((R_GATE, g1), (R_GATE + 1, e * g1), (R_EXPERT, i1.astype(F32)), (R_EXPERT + 1, i2.astype(F32)),
                        (R_RANK, r1), (R_RANK + 1, r2)):
            rec = jnp.where(lane == ln, val, rec)
        route_ref[...] = rec


def _out_proj(a, bcd, h, wo, g2, router, tm):
    t = h.shape[0]
    moe = router is not None
    row = lambda i: (i, 0)
    const = lambda i: (0, 0)
    ins = [a, bcd, h, wo, g2]
    in_specs = [pl.BlockSpec((tm, 256), row), pl.BlockSpec((tm, 768), row), pl.BlockSpec((tm, D_MODEL), row),
                pl.BlockSpec(wo.shape, const), pl.BlockSpec(g2.shape, const)]
    out_specs = [pl.BlockSpec((tm, D_MODEL), row), pl.BlockSpec((tm, D_MODEL), row)]
    out_shape = [jax.ShapeDtypeStruct((t, D_MODEL), F32), jax.ShapeDtypeStruct((t, D_MODEL), F32 if moe else BF16)]
    scratch = []
    if moe:
        stril = jnp.asarray(np.tril(np.ones((tm, tm), np.float32), -1), BF16)
        ins += [router, stril]
        in_specs += [pl.BlockSpec(router.shape, const), pl.BlockSpec(stril.shape, const)]
        out_specs += [pl.BlockSpec((tm, LANES), row), pl.BlockSpec((1, LANES), const)]
        out_shape += [jax.ShapeDtypeStruct((t, LANES), F32), jax.ShapeDtypeStruct((1, LANES), F32)]
        scratch = [pltpu.VMEM((1, LANES), F32)]
    return pl.pallas_call(
        functools.partial(_out_proj_kernel, moe=moe),
        grid=(t // tm,),
        in_specs=in_specs, out_specs=out_specs, out_shape=out_shape, scratch_shapes=scratch,
        compiler_params=_params(("arbitrary",)),
        name="out_proj",
    )(*ins)


def _row_copy(src_ref, src_row, dst_ref, dst_row, sem):
    return pltpu.make_async_copy(src_ref.at[pl.ds(src_row, 1)], dst_ref.at[pl.ds(dst_row, 1)], sem)


def _dispatch_kernel(pos_ref, x_ref, xs_ref, sem, *, tm):
    def issue(r, carry):
        for k in range(2):
            _row_copy(x_ref, r, xs_ref, pos_ref[0, 0, k * tm + r], sem).start()
        return carry

    def drain(r, carry):
        for k in range(2):
            _row_copy(x_ref, r, xs_ref, pos_ref[0, 0, k * tm + r], sem).wait()
        return carry

    lax.fori_loop(0, tm, issue, 0)
    lax.fori_loop(0, tm, drain, 0)


def _dispatch(hn, pos_tiles, n_rows, tm):
    t = hn.shape[0]
    return pl.pallas_call(
        functools.partial(_dispatch_kernel, tm=tm),
        grid=(t // tm,),
        in_specs=[pl.BlockSpec((1, 1, 2 * tm), lambda i: (i, 0, 0), memory_space=pltpu.SMEM),
                  pl.BlockSpec((tm, D_MODEL), lambda i: (i, 0))],
        out_specs=pl.BlockSpec(memory_space=pl.ANY),
        out_shape=jax.ShapeDtypeStruct((n_rows, D_MODEL), F32),
        scratch_shapes=[pltpu.SemaphoreType.DMA],
        compiler_params=_params(("arbitrary",)),
        name="dispatch",
    )(pos_tiles, hn)


def _ffn_kernel(wt_ref, we_ref, lo_ref, hi_ref, first_ref, x_ref, wg_ref, wu_ref, wd_ref, y_ref):
    w = pl.program_id(0)
    lo, hi = lo_ref[w], hi_ref[w]

    @pl.when(hi > lo)
    def _():
        rowi = lax.broadcasted_iota(jnp.int32, (x_ref.shape[0], 1), 0)
        x = jnp.where((rowi >= lo) & (rowi < hi), x_ref[...], jnp.zeros((), x_ref.dtype)).astype(BF16)
        gate = _dot(x, wg_ref[0])
        up = _dot(x, wu_ref[0])
        act = (gate * (1.0 / (1.0 + jnp.exp(-gate))) * up).astype(BF16)
        y = _dot(act, wd_ref[0])

        @pl.when(first_ref[w] == 1)
        def _():
            y_ref[...] = y

        @pl.when(first_ref[w] == 0)
        def _():
            y_ref[...] += y


def _ffn(xs, wg, wu, wd, tabs, tm):
    d_ff = wg.shape[-1]
    xmap = lambda w, wt, we, lo, hi, fi: (wt[w], 0)
    wmap = lambda w, wt, we, lo, hi, fi: (we[w], 0, 0)
    grid_spec = pltpu.PrefetchScalarGridSpec(
        num_scalar_prefetch=5,
        grid=(tabs[0].shape[0],),
        in_specs=[pl.BlockSpec((tm, D_MODEL), xmap),
                  pl.BlockSpec((1, D_MODEL, d_ff), wmap), pl.BlockSpec((1, D_MODEL, d_ff), wmap),
                  pl.BlockSpec((1, d_ff, D_MODEL), wmap)],
        out_specs=pl.BlockSpec((tm, D_MODEL), xmap),
    )
    return pl.pallas_call(
        _ffn_kernel,
        grid_spec=grid_spec,
        out_shape=jax.ShapeDtypeStruct((xs.shape[0], D_MODEL), F32),
        compiler_params=_params(("arbitrary",)),
        name="ffn",
    )(*tabs, xs, wg, wu, wd)


def _work_items(counts, n_rows, tm):
    i32 = jnp.int32
    ends = jnp.cumsum(counts)
    starts = ends - counts
    t_first = starts // tm
    n_e = jnp.where(counts > 0, (ends - 1) // tm - t_first + 1, 0)
    w_end = jnp.cumsum(n_e)
    w_start = w_end - n_e
    n_work = n_rows // tm + counts.shape[0] - 1
    w = jnp.arange(n_work, dtype=i32)
    valid = w < w_end[-1]
    e = jnp.minimum(jnp.searchsorted(w_end, w, side='right'), counts.shape[0] - 1).astype(i32)
    tile = jnp.where(valid, t_first[e] + (w - w_start[e]), n_rows // tm - 1).astype(i32)
    lo = jnp.where(valid, jnp.maximum(starts[e] - tile * tm, 0), 0).astype(i32)
    hi = jnp.where(valid, jnp.minimum(ends[e] - tile * tm, tm), 0).astype(i32)
    e = jnp.where(valid, e, e[jnp.maximum(w_end[-1] - 1, 0)]).astype(i32)
    first = jnp.concatenate([jnp.ones((1,), i32), (tile[1:] != tile[:-1]).astype(i32)])
    return (tile, e, lo, hi, first), starts


def _ple_kernel(*refs, moe, final, tm):
    refs = list(refs)
    if moe:
        pos_ref = refs.pop(0)
    h1_ref, y_ref = refs[0], refs[1]
    k = 2
    if moe:
        route_ref = refs[k]
        k += 1
    p_ref, gw_ref, pw_ref = refs[k:k + 3]
    k += 3
    if final:
        fg_ref = refs[k]
        k += 1
    h_ref = refs[k]
    if moe:
        ysc_ref, sem = refs[-2], refs[-1]

        def issue(r, carry):
            for s in range(2):
                _row_copy(y_ref, pos_ref[0, 0, s * tm + r], ysc_ref.at[s], r, sem).start()
            return carry

        def drain(r, carry):
            for s in range(2):
                _row_copy(y_ref, pos_ref[0, 0, s * tm + r], ysc_ref.at[s], r, sem).wait()
            return carry

        lax.fori_loop(0, tm, issue, 0)
        lax.fori_loop(0, tm, drain, 0)
        rec = route_ref[...]
        lane = lax.broadcasted_iota(jnp.int32, rec.shape, 1)
        g1 = jnp.sum(jnp.where(lane == R_GATE, rec, 0.0), axis=1, keepdims=True)
        g2 = jnp.sum(jnp.where(lane == R_GATE + 1, rec, 0.0), axis=1, keepdims=True)
        h2 = h1_ref[...] + (g1 * ysc_ref[0] + g2 * ysc_ref[1])
    else:
        h2 = h1_ref[...] + y_ref[...]
    gate = _dot(h2.astype(BF16), gw_ref[...])
    pe = _dot(p_ref[...].astype(BF16), pw_ref[...])
    h3 = h2 + (1.0 / (1.0 + jnp.exp(-gate))) * pe
    h_ref[...] = h3
    if final:
        refs[k + 1][...] = h3 * lax.rsqrt(jnp.mean(h3 * h3, axis=-1, keepdims=True) + EPS) * fg_ref[...]


def _ple(h1, ys, route, pos_tiles, p, gw, pw, fg, tm):
    t = h1.shape[0]
    moe = route is not None
    final = fg is not None
    row = lambda i: (i, 0)
    const = lambda i: (0, 0)
    ins, in_specs, scratch = [], [], []
    if moe:
        ins += [pos_tiles, h1, ys, route]
        in_specs += [pl.BlockSpec((1, 1, 2 * tm), lambda i: (i, 0, 0), memory_space=pltpu.SMEM),
                     pl.BlockSpec((tm, D_MODEL), row), pl.BlockSpec(memory_space=pl.ANY), pl.BlockSpec((tm, LANES), row)]
        scratch = [pltpu.VMEM((2, tm, D_MODEL), F32), pltpu.SemaphoreType.DMA]
    else:
        ins += [h1, ys]
        in_specs += [pl.BlockSpec((tm, D_MODEL), row), pl.BlockSpec((tm, D_MODEL), row)]
    ins += [p, gw, pw]
    in_specs += [pl.BlockSpec((tm, 256), row), pl.BlockSpec(gw.shape, const), pl.BlockSpec(pw.shape, const)]
    out_specs = [pl.BlockSpec((tm, D_MODEL), row)]
    out_shape = [jax.ShapeDtypeStruct((t, D_MODEL), F32)]
    if final:
        ins.append(fg)
        in_specs.append(pl.BlockSpec(fg.shape, const))
        out_specs.append(pl.BlockSpec((tm, D_MODEL), row))
        out_shape.append(jax.ShapeDtypeStruct((t, D_MODEL), F32))
    return pl.pallas_call(
        functools.partial(_ple_kernel, moe=moe, final=final, tm=tm),
        grid=(t // tm,),
        in_specs=in_specs, out_specs=out_specs, out_shape=out_shape, scratch_shapes=scratch,
        compiler_params=_params(("arbitrary",)),
        name="ple",
    )(*ins)


def _tile(n, pref):
    return pref if n % pref == 0 else n


def _layer(h, p_l, lw, nb, seq, cache, cprev, pprev, final_g):
    t = nb * seq
    tm = _tile(seq, 512)
    if cache is None:
        c0 = jnp.zeros((nb, 1, LANES), F32)
        past = 0
    else:
        ck, cv, clf = cache
        past = ck.shape[1]
        akc, c0 = _cache_sum(clf, nb, past, _tile(past, 512))
    qs, kf, vf, kb, vb, lf, aq, ak, zc = _in_proj(h, lw['g1'], lw['w_in'], lw['bf'], c0, nb, seq, tm)
    if cache is None:
        a = _attention(qs, aq, kb, ak, vb, nb, seq, tm)
    else:
        lk = past + seq
        cat = lambda c_, n_: jnp.concatenate([c_.reshape(nb, past, 256), n_.reshape(nb, seq, 256)], axis=1
                                             ).reshape(nb * lk, 256)
        a = _attention_rows(qs, aq, cat(ck, kb), cat(akc, ak), cat(cv, vb), nb, seq, lk)
    tc = min(seq, 128)
    outs = _mixers(zc, cprev, pprev, lw, nb, seq, tm, tc, past, cache is not None)
    bcd, cst, pst = outs[:3]
    sgv = outs[3] if cache is not None else None
    tmo = _tile(t, 512)
    moe = 'router' in lw
    res = _out_proj(a, bcd, h, lw['w_o'], lw['g2'], lw.get('router'), tmo)
    h1, hn = res[0], res[1]
    tf = _tile(t, 256)
    nt = t // tf
    if moe:
        route, cnt = res[2], res[3]
        tabs, starts = _work_items(cnt[0, :N_EXPERTS].astype(jnp.int32), 2 * t, tf)
        experts = route[:, R_EXPERT:R_EXPERT + 2].astype(jnp.int32)
        pos = starts[experts] + route[:, R_RANK:R_RANK + 2].astype(jnp.int32)

        def pos_tiles(tm_):
            return pos.reshape(t // tm_, tm_, 2).transpose(0, 2, 1).reshape(t // tm_, 1, 2 * tm_)

        xs = _dispatch(hn, pos_tiles(tmo), 2 * t, tmo)
        ys = _ffn(xs, lw['wg'], lw['wu'], lw['wd'], tabs, tf)
        hres = _ple(h1, ys, route, pos_tiles(tf), p_l, lw['gw'], lw['plw_e'], final_g, tf)
    else:
        zeros = jnp.zeros((nt,), jnp.int32)
        tabs = (jnp.arange(nt, dtype=jnp.int32), zeros, zeros, zeros + tf, zeros + 1)
        ys = _ffn(hn, lw['wg'], lw['wu'], lw['wd'], tabs, tf)
        hres = _ple(h1, ys, None, None, p_l, lw['gw'], lw['plw_e'], final_g, tf)
    return hres, kf, vf, lf[:, :N_HEADS], cst[:, 2:], pst[:, 1:], sgv


def kernel(x_prompt, x_sample, p_prompt, p_sample, cache_k, cache_v, cache_logf, state_conv, state_pool, norm1_g, w_in, b_f, conv_dw_w, conv_dw_b, conv_ln_g, conv_ln_b, conv_pw_w, pool_w, pool_scale, sg_w, sg_b, w_o, norm2_g, ffn_w_gate, ffn_w_up, ffn_w_down, moe_router, moe_w_gate, moe_w_up, moe_w_down, ple_w, ple_gate_w, final_g):
    depth = w_in.shape[0]
    bp, sp, _ = x_prompt.shape
    bs, ss, _ = x_sample.shape
    past = cache_k.shape[2]
    hp = x_prompt.reshape(bp * sp, D_MODEL)
    hs = x_sample.reshape(bs * ss, D_MODEL)
    fg = final_g.reshape(1, D_MODEL)
    outs_p, outs_s = [], []
    for l in range(depth):
        j = l // 2
        wl = w_in[l]
        w_r = jnp.concatenate([wl[:, 0:768], wl[:, 772:2052], jnp.pad(wl[:, 768:772], ((0, 0), (0, LANES - N_HEADS)))],
                              axis=1).astype(BF16)
        plw = jnp.zeros((GROUP_W, GROUP_W), F32)
        for gi in range(len(POOL_WINDOWS)):
            plw = plw.at[64 * gi:64 * (gi + 1), 64 * gi:64 * (gi + 1)].set(pool_w[l, gi])
        lw = {
            'g1': norm1_g[l].reshape(1, D_MODEL), 'w_in': w_r,
            'bf': jnp.pad(b_f[l], (0, LANES - N_HEADS)).reshape(1, LANES),
            'dw': jnp.pad(conv_dw_w[l], ((0, 1), (0, 0))), 'dwb': conv_dw_b[l].reshape(1, GROUP_W),
            'lng': conv_ln_g[l].reshape(1, GROUP_W), 'lnb': conv_ln_b[l].reshape(1, GROUP_W),
            'pw': conv_pw_w[l].astype(BF16), 'plw': plw.astype(BF16), 'psc': pool_scale[l].reshape(1, GROUP_W),
            'w_o': w_o[l].astype(BF16), 'g2': norm2_g[l].reshape(1, D_MODEL),
            'gw': ple_gate_w[l].astype(BF16), 'plw_e': ple_w[l].astype(BF16),
        }
        if l % 2 == 0:
            lw.update(wg=ffn_w_gate[j][None].astype(BF16), wu=ffn_w_up[j][None].astype(BF16),
                      wd=ffn_w_down[j][None].astype(BF16))
        else:
            lw.update(wg=moe_w_gate[j].astype(BF16), wu=moe_w_up[j].astype(BF16), wd=moe_w_down[j].astype(BF16),
                      router=jnp.pad(moe_router[j], ((0, 0), (0, LANES - N_EXPERTS))).astype(BF16))
        last = fg if l == depth - 1 else None

        def sg_params(tc):
            return {'sgw': sg_w[l][:, :tc, :tc],
                    'sgb': jnp.repeat(sg_b[l][:, :tc].T, GROUP_W // SG_GROUPS, axis=1)}

        lwp = dict(lw, **sg_params(min(sp, 128)))
        res = _layer(hp, p_prompt[l].reshape(bp * sp, -1), lwp, bp, sp, None,
                     jnp.zeros((bp, 32, GROUP_W), F32), jnp.zeros((bp, 16, GROUP_W), F32), last)
        hp_res, outs = res[0], res[1:]
        outs_p.append(outs)
        lws = dict(lw, **sg_params(min(ss, 128)))
        cache = (cache_k[l].reshape(bs, past, 256).astype(BF16), cache_v[l].reshape(bs, past, 256).astype(BF16),
                 jnp.pad(cache_logf[l], ((0, 0), (0, 0), (0, LANES - N_HEADS))).reshape(bs * past, LANES))
        res = _layer(hs, p_sample[l].reshape(bs * ss, -1), lws, bs, ss, cache,
                     jnp.pad(state_conv[l], ((0, 0), (2, 0), (0, 0))), jnp.pad(state_pool[l], ((0, 0), (1, 0), (0, 0))),
                     last)
        hs_res, outs = res[0], res[1:]
        outs_s.append(outs)
        if last is None:
            hp, hs = hp_res[0], hs_res[0]
        else:
            y_prompt, y_sample = hp_res[1], hs_res[1]

    def stack(outs, i, shape):
        return jnp.stack([o[i] for o in outs]).reshape((depth,) + shape)

    return (y_prompt.reshape(bp, sp, D_MODEL), y_sample.reshape(bs, ss, D_MODEL),
            stack(outs_p, 0, (bp, sp, N_HEADS, HEAD_DIM)), stack(outs_p, 1, (bp, sp, N_HEADS, HEAD_DIM)),
            stack(outs_p, 2, (bp, sp, N_HEADS)), stack(outs_p, 3, (bp, CONV_W - 1, GROUP_W)),
            stack(outs_p, 4, (bp, POOL_HIST, GROUP_W)),
            stack(outs_s, 0, (bs, ss, N_HEADS, HEAD_DIM)), stack(outs_s, 1, (bs, ss, N_HEADS, HEAD_DIM)),
            stack(outs_s, 2, (bs, ss, N_HEADS)), stack(outs_s, 3, (bs, CONV_W - 1, GROUP_W)),
            stack(outs_s, 4, (bs, POOL_HIST, GROUP_W)), stack(outs_s, 5, (bs, ss, GROUP_W)))
```

```python
import functools
from typing import NamedTuple

import numpy as np
import jax
import jax.numpy as jnp
from jax import lax
from jax.experimental import pallas as pl
from jax.experimental.pallas import tpu as pltpu

F32 = jnp.float32
BF16 = jnp.bfloat16
EPS = 1e-6

D_MODEL = 1024
GROUP_W = 256
N_HEADS = 4
HEAD_DIM = 64
CONV_W = 31
POOL_WINDOWS = (2, 4, 8, 16)
POOL_HIST = 15
SG_GROUPS = 4
N_EXPERTS = 8
LANES = 128
AUG_PER_HEAD = 6
VMEM_LIMIT = 56 * 1024 * 1024


def _params(sem, vmem=VMEM_LIMIT):
    return pltpu.CompilerParams(dimension_semantics=sem, vmem_limit_bytes=vmem)


def _dot(a, b):
    return jnp.dot(a, b, preferred_element_type=F32)


class _LW(NamedTuple):
    arr: jax.Array
    idx: int


def _wspec(w):
    nd = w.arr.ndim - 1
    return pl.BlockSpec((None,) + w.arr.shape[1:], lambda *_: (w.idx,) + (0,) * nd)


def _split3(x):
    hi = x.astype(BF16)
    r1 = x - hi.astype(F32)
    mid = r1.astype(BF16)
    lo = (r1 - mid.astype(F32)).astype(BF16)
    return jnp.concatenate([hi, mid, lo], axis=1)


def _running_sum(x, tril, carry_ref):
    cs = _dot(tril, _split3(x))
    c = carry_ref[...] + (cs[:, :LANES] + cs[:, LANES:2 * LANES] + cs[:, 2 * LANES:])
    n = x.shape[0]
    carry_ref[...] = c[n - 1:n, :]
    return c


def _placement(sign_k):
    pq = np.zeros((3 * LANES, 2 * LANES), np.float32)
    pk = np.zeros((3 * LANES, 2 * LANES), np.float32)
    oq = np.zeros((1, 2 * LANES), np.float32)
    ok = np.zeros((1, 2 * LANES), np.float32)
    for h in range(N_HEADS):
        p, j = divmod(h, 2)
        for i in range(3):
            pq[i * LANES + h, p * LANES + j * AUG_PER_HEAD + i] = 1.0
            oq[0, p * LANES + j * AUG_PER_HEAD + 3 + i] = 1.0
            pk[i * LANES + h, p * LANES + j * AUG_PER_HEAD + 3 + i] = sign_k
            ok[0, p * LANES + j * AUG_PER_HEAD + i] = 1.0
    return pq, pk, oq, ok


def _in_proj_kernel(h_ref, g_ref, w_ref, bf_ref, c0_ref, tril_ref, pq_ref, pk_ref, oq_ref, ok_ref,
                    qs_ref, kf_ref, vf_ref, kb_ref, vb_ref, lf_ref, aq_ref, ak_ref, zc_ref, carry_ref):
    @pl.when(pl.program_id(1) == 0)
    def _():
        carry_ref[...] = c0_ref[0]

    x = h_ref[...]
    hn = x * lax.rsqrt(jnp.mean(x * x, axis=-1, keepdims=True) + EPS) * g_ref[...]
    hb = hn.astype(BF16)
    qs_ref[...] = (_dot(hb, w_ref[:, 0:256]) * (HEAD_DIM ** -0.5)).astype(BF16)
    k = _dot(hb, w_ref[:, 256:512])
    kf_ref[...] = k
    kb_ref[...] = k.astype(BF16)
    v = _dot(hb, w_ref[:, 512:768])
    vf_ref[...] = v
    vb_ref[...] = v.astype(BF16)
    zc_ref[...] = _dot(hb, w_ref[:, 768:2048])
    zf = _dot(hb, w_ref[:, 2048:2176]) + bf_ref[...]
    lf = jnp.minimum(zf, 0.0) - jnp.log1p(jnp.exp(-jnp.abs(zf)))
    lane = lax.broadcasted_iota(jnp.int32, lf.shape, 1)
    lf = jnp.where(lane < N_HEADS, lf, 0.0)
    lf_ref[...] = lf
    cp = _split3(_running_sum(lf, tril_ref[...], carry_ref))
    aq_ref[...] = (_dot(cp, pq_ref[...]) + oq_ref[...]).astype(BF16)
    ak_ref[...] = (_dot(cp, pk_ref[...]) + ok_ref[...]).astype(BF16)


def _in_proj(h, g1, w, bfp, c0, nb, seq, tm):
    t = nb * seq
    ns = seq // tm
    tril = jnp.asarray(np.tril(np.ones((tm, tm), np.float32)), BF16)
    pq, pk, oq, ok = _placement(-1.0)
    row = lambda b, s: (b * ns + s, 0)
    const = lambda b, s: (0, 0)
    blk = lambda w_: pl.BlockSpec((tm, w_), row)
    full = lambda a: pl.BlockSpec(a.shape, const)
    consts = [jnp.asarray(pq, BF16), jnp.asarray(pk, BF16), jnp.asarray(oq), jnp.asarray(ok)]
    outs = [(256, BF16), (256, F32), (256, F32), (256, BF16), (256, BF16), (LANES, F32),
            (256, BF16), (256, BF16), (1280, F32)]
    return pl.pallas_call(
        _in_proj_kernel,
        grid=(nb, ns),
        in_specs=[blk(D_MODEL), _wspec(g1), _wspec(w), _wspec(bfp),
                  pl.BlockSpec((1, 1, LANES), lambda b, s: (b, 0, 0)), full(tril)] + [full(a) for a in consts],
        out_specs=[blk(w_) for w_, _ in outs],
        out_shape=[jax.ShapeDtypeStruct((t, w_), dt) for w_, dt in outs],
        scratch_shapes=[pltpu.VMEM((1, LANES), F32)],
        compiler_params=_params(("arbitrary", "arbitrary")),
        name="in_proj",
    )(h, g1.arr, w.arr, bfp.arr, c0, tril, *consts)


def _cache_sum_kernel(lf_ref, tril_ref, pk_ref, ok_ref, ak_ref, tot_ref, carry_ref):
    @pl.when(pl.program_id(1) == 0)
    def _():
        carry_ref[...] = jnp.zeros_like(carry_ref)

    cp = _split3(_running_sum(lf_ref[...], tril_ref[...], carry_ref))
    ak_ref[...] = (_dot(cp, pk_ref[...]) + ok_ref[...]).astype(BF16)
    tot_ref[0] = carry_ref[...]


def _cache_sum(lfp, nb, past, tm):
    ns = past // tm
    tril = jnp.asarray(np.tril(np.ones((tm, tm), np.float32)), BF16)
    _, pk, _, ok = _placement(-1.0)
    pk, ok = jnp.asarray(pk, BF16), jnp.asarray(ok)
    const = lambda b, s: (0, 0)
    return pl.pallas_call(
        _cache_sum_kernel,
        grid=(nb, ns),
        in_specs=[pl.BlockSpec((tm, LANES), lambda b, s: (b * ns + s, 0)), pl.BlockSpec(tril.shape, const),
                  pl.BlockSpec(pk.shape, const), pl.BlockSpec(ok.shape, const)],
        out_specs=[pl.BlockSpec((tm, 256), lambda b, s: (b * ns + s, 0)),
                   pl.BlockSpec((1, 1, LANES), lambda b, s: (b, 0, 0))],
        out_shape=[jax.ShapeDtypeStruct((nb * past, 256), BF16), jax.ShapeDtypeStruct((nb, 1, LANES), F32)],
        scratch_shapes=[pltpu.VMEM((1, LANES), F32)],
        compiler_params=_params(("arbitrary", "arbitrary")),
        name="cache_sum",
    )(lfp, tril, pk, ok)


NEG_BIG = -1e30


def _head_lhs(q, aq, j, lane):
    zero = jnp.zeros((), BF16)
    qm = jnp.where((lane >= HEAD_DIM * j) & (lane < HEAD_DIM * (j + 1)), q, zero)
    am = jnp.where((lane >= AUG_PER_HEAD * j) & (lane < AUG_PER_HEAD * (j + 1)), aq, zero)
    return jnp.concatenate([qm, am], axis=1)


def _scores(lhs, k, ak):
    rhs = jnp.concatenate([k, ak], axis=1)
    return lax.dot_general(lhs, rhs, (((1,), (1,)), ((), ())), preferred_element_type=F32)


def _attn_kernel(q_ref, aq_ref, k_ref, ak_ref, v_ref, o_ref, m_sc, l_sc, acc_sc, *, t):
    qi = pl.program_id(1)
    lane = lax.broadcasted_iota(jnp.int32, (1, LANES), 1)
    row = lax.broadcasted_iota(jnp.int32, (t, t), 0)
    col = lax.broadcasted_iota(jnp.int32, (t, t), 1)
    nct = t // LANES
    for p in range(2):
        cs = slice(p * LANES, (p + 1) * LANES)
        lhs = [_head_lhs(q_ref[:, cs], aq_ref[:, cs], j, lane) for j in range(2)]

        def scores(j, ki, masked):
            rows = pl.ds(pl.multiple_of(ki * t, t), t)
            s = _scores(lhs[j], k_ref[rows, cs], ak_ref[rows, cs])
            if masked:
                s = jnp.where(col <= row, s, -jnp.inf)
            return [s[:, c * LANES:(c + 1) * LANES] for c in range(nct)]

        def stats(ki, masked):
            for j in range(2):
                parts = scores(j, ki, masked)
                m_old = m_sc[j]
                m_new = functools.reduce(jnp.maximum, parts, m_old)
                tot = l_sc[j] * jnp.exp(m_old - m_new)
                for part in parts:
                    tot = tot + jnp.exp(part - m_new)
                l_sc[j] = tot
                m_sc[j] = m_new

        def accumulate(ki, masked):
            v = v_ref[pl.ds(pl.multiple_of(ki * t, t), t), cs]
            for j in range(2):
                parts = scores(j, ki, masked)
                m_rep = m_sc[j]
                r_rep = l_sc[j]
                prob = jnp.concatenate([(jnp.exp(part - m_rep) * r_rep).astype(BF16) for part in parts], axis=1)
                acc_sc[j] += _dot(prob, v)

        def sweep(fn):
            def body(ki, carry):
                fn(ki, False)
                return carry
            lax.fori_loop(0, qi, body, 0)
            fn(qi, True)

        m_sc[...] = jnp.full_like(m_sc, NEG_BIG)
        l_sc[...] = jnp.zeros_like(l_sc)
        sweep(stats)
        for j in range(2):
            m_lane = m_sc[j]
            m_row = jnp.max(m_lane, axis=1, keepdims=True)
            l_row = jnp.sum(l_sc[j] * jnp.exp(m_lane - m_row), axis=1, keepdims=True)
            m_sc[j] = jnp.broadcast_to(m_row, m_lane.shape)
            l_sc[j] = jnp.broadcast_to(1.0 / l_row, m_lane.shape)
        acc_sc[...] = jnp.zeros_like(acc_sc)
        sweep(accumulate)
        o_ref[:, cs] = jnp.where(lane < HEAD_DIM, acc_sc[0], acc_sc[1]).astype(BF16)


def _attention(qs, aq, kb, ak, vb, nb, seq, t):
    assert seq % t == 0 and t % LANES == 0
    nq = seq // t
    qblk = pl.BlockSpec((t, 256), lambda b, i: (b * nq + i, 0))
    kblk = pl.BlockSpec((seq, 256), lambda b, i: (b, 0))
    return pl.pallas_call(
        functools.partial(_attn_kernel, t=t),
        grid=(nb, nq),
        in_specs=[qblk, qblk, kblk, kblk, kblk],
        out_specs=qblk,
        out_shape=jax.ShapeDtypeStruct((nb * seq, 256), BF16),
        scratch_shapes=[pltpu.VMEM((2, t, LANES), F32)] * 3,
        compiler_params=_params(("arbitrary", "arbitrary")),
        name="attn",
    )(qs, aq, kb, ak, vb)


def _attn_rows_kernel(q_ref, aq_ref, k_ref, ak_ref, v_ref, o_ref, *, past):
    lq, lk = q_ref.shape[0], k_ref.shape[0]
    lane = lax.broadcasted_iota(jnp.int32, (1, LANES), 1)
    row = lax.broadcasted_iota(jnp.int32, (lq, lk), 0)
    col = lax.broadcasted_iota(jnp.int32, (lq, lk), 1)
    for p in range(2):
        cs = slice(p * LANES, (p + 1) * LANES)
        outs = []
        for j in range(2):
            s = _scores(_head_lhs(q_ref[:, cs], aq_ref[:, cs], j, lane), k_ref[:, cs], ak_ref[:, cs])
            s = jnp.where(col <= past + row, s, -jnp.inf)
            e = jnp.exp(s - jnp.max(s, axis=1, keepdims=True))
            prob = (e * (1.0 / jnp.sum(e, axis=1, keepdims=True))).astype(BF16)
            outs.append(_dot(prob, v_ref[:, cs]))
        o_ref[:, cs] = jnp.where(lane < HEAD_DIM, outs[0], outs[1]).astype(BF16)


def _attention_rows(qs, aq, kb, ak, vb, nb, lq, lk):
    qblk = pl.BlockSpec((lq, 256), lambda b: (b, 0))
    kblk = pl.BlockSpec((lk, 256), lambda b: (b, 0))
    return pl.pallas_call(
        functools.partial(_attn_rows_kernel, past=lk - lq),
        grid=(nb,),
        in_specs=[qblk, qblk, kblk, kblk, kblk],
        out_specs=qblk,
        out_shape=jax.ShapeDtypeStruct((nb * lq, 256), BF16),
        compiler_params=_params(("arbitrary",)),
        name="attn_rows",
    )(qs, aq, kb, ak, vb)


def _gelu(x):
    return 0.5 * x * (1.0 + jnp.tanh(0.7978845608028654 * (x + 0.044715 * (x * x * x))))


def _mixers_kernel(ca_ref, cg_ref, px_ref, su_ref, sv_ref, cprev_ref, pprev_ref, dw_ref, dwb_ref, lng_ref, lnb_ref,
                   pw_ref, plw_ref, psc_ref, sgw_ref, sgb_ref,
                   bcd_ref, cst_ref, pst_ref, sgv_ref, xc_ref, xp_ref, *, ts, tc, pos0):
    s = pl.program_id(1)

    @pl.when(s == 0)
    def _():
        xc_ref[0:32, :] = cprev_ref[0].astype(BF16).astype(F32)
        xp_ref[0:16, :] = pprev_ref[0]

    @pl.when(s > 0)
    def _():
        xc_ref[0:32, :] = xc_ref[ts:ts + 32, :]
        xp_ref[0:16, :] = xp_ref[ts:ts + 16, :]

    g = cg_ref[...]
    u = ca_ref[...] * (1.0 / (1.0 + jnp.exp(-g)))
    xc_ref[32:32 + ts, :] = u.astype(BF16).astype(F32)
    y = jnp.zeros((ts, GROUP_W), F32)
    for j in range(CONV_W):
        y = y + xc_ref[2 + j:2 + j + ts, :] * dw_ref[j:j + 1, :]
    y = y + dwb_ref[...]
    mu = jnp.mean(y, axis=-1, keepdims=True)
    var = jnp.mean(jnp.square(y - mu), axis=-1, keepdims=True)
    y = (y - mu) * lax.rsqrt(var + EPS) * lng_ref[...] + lnb_ref[...]
    y = y * (1.0 / (1.0 + jnp.exp(-y)))
    bcd_ref[:, 0:256] = _dot(y.astype(BF16), pw_ref[...]).astype(BF16)
    cst_ref[0] = u[ts - 32:ts, :]

    x = px_ref[...]
    xp_ref[16:16 + ts, :] = x
    lane = lax.broadcasted_iota(jnp.int32, (ts, GROUP_W), 1)
    pos = (pos0 + s * ts + lax.broadcasted_iota(jnp.int32, (ts, 1), 0)).astype(F32)
    acc = x
    mean = jnp.zeros((ts, GROUP_W), F32)
    wi = 0
    for i in range(1, POOL_WINDOWS[-1] + 1):
        if i > 1:
            acc = acc + xp_ref[16 - (i - 1):16 - (i - 1) + ts, :]
        if i == POOL_WINDOWS[wi]:
            cnt = jnp.minimum(pos + 1.0, float(i))
            mean = jnp.where((lane >= 64 * wi) & (lane < 64 * (wi + 1)), acc / cnt, mean)
            wi += 1
    dlt = (mean - x).astype(BF16)
    bcd_ref[:, 256:512] = (_dot(dlt, plw_ref[...]) * psc_ref[...]).astype(BF16)
    pst_ref[0] = xp_ref[ts:ts + 16, :]

    r = lax.broadcasted_iota(jnp.int32, (tc, tc), 0)
    c = lax.broadcasted_iota(jnp.int32, (tc, tc), 1)
    lane_c = lax.broadcasted_iota(jnp.int32, (tc, GROUP_W), 1)
    wts = [jnp.where(c <= r, sgw_ref[gi], 0.0).astype(BF16) for gi in range(SG_GROUPS)]
    for ci in range(ts // tc):
        uu = _gelu(su_ref[ci * tc:(ci + 1) * tc, :])
        vv = _gelu(sv_ref[ci * tc:(ci + 1) * tc, :])
        if sgv_ref is not None:
            sgv_ref[ci * tc:(ci + 1) * tc, :] = vv
        m = sgb_ref[...]
        for gi in range(SG_GROUPS):
            vg = jnp.where((lane_c >= 64 * gi) & (lane_c < 64 * (gi + 1)), vv, 0.0).astype(BF16)
            m = m + _dot(wts[gi], vg)
        bcd_ref[ci * tc:(ci + 1) * tc, 512:768] = (uu * m).astype(BF16)


def _mixers_kernel_nosgv(*refs, **kw):
    _mixers_kernel(*refs[:19], None, *refs[19:], **kw)


def _mixers(zc, cprev, pprev, lw, nb, seq, ts, tc, pos0, want_sgv):
    t = nb * seq
    ns = seq // ts
    zblk = lambda j: pl.BlockSpec((ts, GROUP_W), lambda b, s: (b * ns + s, j))
    perb = lambda r: pl.BlockSpec((1, r, GROUP_W), lambda b, s: (b, 0, 0))
    small = [lw['dw'], lw['dwb'], lw['lng'], lw['lnb'], lw['pw'], lw['plw'], lw['psc'], lw['sgw'], lw['sgb']]
    out_specs = [pl.BlockSpec((ts, 768), lambda b, s: (b * ns + s, 0)), perb(32), perb(16)]
    out_shape = [jax.ShapeDtypeStruct((t, 768), BF16), jax.ShapeDtypeStruct((nb, 32, GROUP_W), F32),
                 jax.ShapeDtypeStruct((nb, 16, GROUP_W), F32)]
    if want_sgv:
        out_specs.append(pl.BlockSpec((ts, GROUP_W), lambda b, s: (b * ns + s, 0)))
        out_shape.append(jax.ShapeDtypeStruct((t, GROUP_W), F32))
    body = _mixers_kernel if want_sgv else _mixers_kernel_nosgv
    return pl.pallas_call(
        functools.partial(body, ts=ts, tc=tc, pos0=pos0),
        grid=(nb, ns),
        in_specs=[zblk(j) for j in range(5)] + [perb(32), perb(16)] + [_wspec(a) for a in small],
        out_specs=out_specs,
        out_shape=out_shape,
        scratch_shapes=[pltpu.VMEM((32 + ts, GROUP_W), F32), pltpu.VMEM((16 + ts, GROUP_W), F32)],
        compiler_params=_params(("arbitrary", "arbitrary")),
        name="mixers",
    )(zc, zc, zc, zc, zc, cprev, pprev, *[a.arr for a in small])


R_GATE, R_EXPERT, R_RANK = 0, 2, 4


def _out_proj_kernel(a_ref, bcd_ref, h_ref, wo_ref, g_ref, *rest, moe):
    if moe:
        router_ref, stril_ref, h1_ref, hn_ref, route_ref, cnt_ref, carry_ref = rest
    else:
        h1_ref, hn_ref = rest
    mix = _dot(a_ref[...], wo_ref[0:256, :]) + _dot(bcd_ref[...], wo_ref[256:1024, :])
    h1 = h_ref[...] + mix
    hn = h1 * lax.rsqrt(jnp.mean(h1 * h1, axis=-1, keepdims=True) + EPS) * g_ref[...]
    h1_ref[...] = h1
    hn_ref[...] = hn.astype(hn_ref.dtype)
    if moe:
        @pl.when(pl.program_id(0) == 0)
        def _():
            carry_ref[...] = jnp.zeros_like(carry_ref)

        logits = _dot(hn.astype(BF16), router_ref[...])
        lane = lax.broadcasted_iota(jnp.int32, logits.shape, 1)
        logits = jnp.where(lane < N_EXPERTS, logits, -jnp.inf)
        v1 = jnp.max(logits, axis=1, keepdims=True)
        i1 = jnp.min(jnp.where(logits == v1, lane, LANES), axis=1, keepdims=True)
        rest_l = jnp.where(lane == i1, -jnp.inf, logits)
        v2 = jnp.max(rest_l, axis=1, keepdims=True)
        i2 = jnp.min(jnp.where(rest_l == v2, lane, LANES), axis=1, keepdims=True)
        e = jnp.exp(v2 - v1)
        g1 = 1.0 / (1.0 + e)
        oh1, oh2 = lane == i1, lane == i2
        oh = jnp.where(oh1 | oh2, 1.0, 0.0)
        before = _dot(stril_ref[...], oh.astype(BF16)) + carry_ref[...]
        r1 = jnp.sum(jnp.where(oh1, before, 0.0), axis=1, keepdims=True)
        r2 = jnp.sum(jnp.where(oh2, before, 0.0), axis=1, keepdims=True)
        carry_ref[...] += jnp.sum(oh, axis=0, keepdims=True)
        cnt_ref[...] = carry_ref[...]
        rec = jnp.zeros_like(logits)
        for ln, val in ((R_GATE, g1), (R_GATE + 1, e * g1), (R_EXPERT, i1.astype(F32)), (R_EXPERT + 1, i2.astype(F32)),
                        (R_RANK, r1), (R_RANK + 1, r2)):
            rec = jnp.where(lane == ln, val, rec)
        route_ref[...] = rec


def _out_proj(a, bcd, h, wo, g2, router, tm):
    t = h.shape[0]
    moe = router is not None
    row = lambda i: (i, 0)
    const = lambda i: (0, 0)
    ins = [a, bcd, h, wo.arr, g2.arr]
    in_specs = [pl.BlockSpec((tm, 256), row), pl.BlockSpec((tm, 768), row), pl.BlockSpec((tm, D_MODEL), row),
                _wspec(wo), _wspec(g2)]
    out_specs = [pl.BlockSpec((tm, D_MODEL), row), pl.BlockSpec((tm, D_MODEL), row)]
    out_shape = [jax.ShapeDtypeStruct((t, D_MODEL), F32), jax.ShapeDtypeStruct((t, D_MODEL), F32 if moe else BF16)]
    scratch = []
    if moe:
        stril = jnp.asarray(np.tril(np.ones((tm, tm), np.float32), -1), BF16)
        ins += [router.arr, stril]
        in_specs += [_wspec(router), pl.BlockSpec(stril.shape, const)]
        out_specs += [pl.BlockSpec((tm, LANES), row), pl.BlockSpec((1, LANES), const)]
        out_shape += [jax.ShapeDtypeStruct((t, LANES), F32), jax.ShapeDtypeStruct((1, LANES), F32)]
        scratch = [pltpu.VMEM((1, LANES), F32)]
    return pl.pallas_call(
        functools.partial(_out_proj_kernel, moe=moe),
        grid=(t // tm,),
        in_specs=in_specs, out_specs=out_specs, out_shape=out_shape, scratch_shapes=scratch,
        compiler_params=_params(("arbitrary",)),
        name="out_proj",
    )(*ins)


DMA_UNROLL = 8


def _row_copy(src_ref, src_row, dst_ref, dst_row, sem):
    return pltpu.make_async_copy(src_ref.at[pl.ds(src_row, 1)], dst_ref.at[pl.ds(dst_row, 1)], sem)


def _dispatch_kernel(pos_ref, x_ref, xs_ref, sem, *, tm):
    def issue(r, carry):
        for k in range(2):
            _row_copy(x_ref, r, xs_ref, pos_ref[0, 0, k * tm + r], sem).start(priority=k)
        return carry

    def drain(r, carry):
        for k in range(2):
            _row_copy(x_ref, r, xs_ref, pos_ref[0, 0, k * tm + r], sem).wait()
        return carry

    lax.fori_loop(0, tm, issue, 0, unroll=DMA_UNROLL)
    lax.fori_loop(0, tm, drain, 0, unroll=DMA_UNROLL)


def _dispatch(hn, pos_tiles, n_rows, tm):
    t = hn.shape[0]
    return pl.pallas_call(
        functools.partial(_dispatch_kernel, tm=tm),
        grid=(t // tm,),
        in_specs=[pl.BlockSpec((1, 1, 2 * tm), lambda i: (i, 0, 0), memory_space=pltpu.SMEM),
                  pl.BlockSpec((tm, D_MODEL), lambda i: (i, 0))],
        out_specs=pl.BlockSpec(memory_space=pl.ANY),
        out_shape=jax.ShapeDtypeStruct((n_rows, D_MODEL), F32),
        scratch_shapes=[pltpu.SemaphoreType.DMA],
        compiler_params=_params(("arbitrary",)),
        name="dispatch",
    )(pos_tiles, hn)


def _ffn_kernel(wt_ref, we_ref, lo_ref, hi_ref, first_ref, x_ref, wg_ref, wu_ref, wd_ref, y_ref):
    w = pl.program_id(0)
    lo, hi = lo_ref[w], hi_ref[w]

    @pl.when(hi > lo)
    def _():
        rowi = lax.broadcasted_iota(jnp.int32, (x_ref.shape[0], 1), 0)
        x = jnp.where((rowi >= lo) & (rowi < hi), x_ref[...], jnp.zeros((), x_ref.dtype)).astype(BF16)
        gate = _dot(x, wg_ref[0])
        up = _dot(x, wu_ref[0])
        act = (gate * (1.0 / (1.0 + jnp.exp(-gate))) * up).astype(BF16)
        y = _dot(act, wd_ref[0])

        @pl.when(first_ref[w] == 1)
        def _():
            y_ref[...] = y

        @pl.when(first_ref[w] == 0)
        def _():
            y_ref[...] += y


def _ffn(xs, wg, wu, wd, tabs, tm):
    d_ff = wg.arr.shape[-1]
    xmap = lambda w, wt, we, lo, hi, fi: (wt[w], 0)
    wmap = lambda w, wt, we, lo, hi, fi: (wg.idx, we[w], 0, 0)
    grid_spec = pltpu.PrefetchScalarGridSpec(
        num_scalar_prefetch=5,
        grid=(tabs[0].shape[0],),
        in_specs=[pl.BlockSpec((tm, D_MODEL), xmap),
                  pl.BlockSpec((None, 1, D_MODEL, d_ff), wmap), pl.BlockSpec((None, 1, D_MODEL, d_ff), wmap),
                  pl.BlockSpec((None, 1, d_ff, D_MODEL), wmap)],
        out_specs=pl.BlockSpec((tm, D_MODEL), xmap),
    )
    return pl.pallas_call(
        _ffn_kernel,
        grid_spec=grid_spec,
        out_shape=jax.ShapeDtypeStruct((xs.shape[0], D_MODEL), F32),
        compiler_params=_params(("arbitrary",)),
        name="ffn",
    )(*tabs, xs, wg.arr, wu.arr, wd.arr)


def _work_items(counts, n_rows, tm):
    i32 = jnp.int32
    ends = jnp.cumsum(counts)
    starts = ends - counts
    t_first = starts // tm
    n_e = jnp.where(counts > 0, (ends - 1) // tm - t_first + 1, 0)
    w_end = jnp.cumsum(n_e)
    w_start = w_end - n_e
    n_work = n_rows // tm + counts.shape[0] - 1
    w = jnp.arange(n_work, dtype=i32)
    valid = w < w_end[-1]
    e = jnp.minimum(jnp.sum((w[:, None] >= w_end[None, :]).astype(i32), axis=1), counts.shape[0] - 1)
    tile = jnp.where(valid, t_first[e] + (w - w_start[e]), n_rows // tm - 1).astype(i32)
    lo = jnp.where(valid, jnp.maximum(starts[e] - tile * tm, 0), 0).astype(i32)
    hi = jnp.where(valid, jnp.minimum(ends[e] - tile * tm, tm), 0).astype(i32)
    e = jnp.where(valid, e, e[jnp.maximum(w_end[-1] - 1, 0)]).astype(i32)
    first = jnp.concatenate([jnp.ones((1,), i32), (tile[1:] != tile[:-1]).astype(i32)])
    return (tile, e, lo, hi, first), starts


def _ple_kernel(*refs, moe, final, tm):
    refs = list(refs)
    if moe:
        pos_ref = refs.pop(0)
    h1_ref, y_ref = refs[0], refs[1]
    k = 2
    if moe:
        route_ref = refs[k]
        k += 1
    p_ref, gw_ref, pw_ref = refs[k:k + 3]
    k += 3
    if final:
        fg_ref = refs[k]
        k += 1
    h_ref = refs[k]
    if moe:
        ysc_ref, sem = refs[-2], refs[-1]

        def issue(r, carry):
            for s in range(2):
                _row_copy(y_ref, pos_ref[0, 0, s * tm + r], ysc_ref.at[s], r, sem).start(priority=s)
            return carry

        def drain(r, carry):
            for s in range(2):
                _row_copy(y_ref, pos_ref[0, 0, s * tm + r], ysc_ref.at[s], r, sem).wait()
            return carry

        lax.fori_loop(0, tm, issue, 0, unroll=DMA_UNROLL)
        lax.fori_loop(0, tm, drain, 0, unroll=DMA_UNROLL)
        rec = route_ref[...]
        lane = lax.broadcasted_iota(jnp.int32, rec.shape, 1)
        g1 = jnp.sum(jnp.where(lane == R_GATE, rec, 0.0), axis=1, keepdims=True)
        g2 = jnp.sum(jnp.where(lane == R_GATE + 1, rec, 0.0), axis=1, keepdims=True)
        h2 = h1_ref[...] + (g1 * ysc_ref[0] + g2 * ysc_ref[1])
    else:
        h2 = h1_ref[...] + y_ref[...]
    gate = _dot(h2.astype(BF16), gw_ref[...])
    pe = _dot(p_ref[...].astype(BF16), pw_ref[...])
    h3 = h2 + (1.0 / (1.0 + jnp.exp(-gate))) * pe
    h_ref[...] = h3
    if final:
        refs[k + 1][...] = h3 * lax.rsqrt(jnp.mean(h3 * h3, axis=-1, keepdims=True) + EPS) * fg_ref[...]


def _ple(h1, ys, route, pos_tiles, p, gw, pw, fg, tm):
    t = h1.shape[0]
    moe = route is not None
    final = fg is not None
    row = lambda i: (i, 0)
    const = lambda i: (0, 0)
    ins, in_specs, scratch = [], [], []
    if moe:
        ins += [pos_tiles, h1, ys, route]
        in_specs += [pl.BlockSpec((1, 1, 2 * tm), lambda i: (i, 0, 0), memory_space=pltpu.SMEM),
                     pl.BlockSpec((tm, D_MODEL), row), pl.BlockSpec(memory_space=pl.ANY), pl.BlockSpec((tm, LANES), row)]
        scratch = [pltpu.VMEM((2, tm, D_MODEL), F32), pltpu.SemaphoreType.DMA]
    else:
        ins += [h1, ys]
        in_specs += [pl.BlockSpec((tm, D_MODEL), row), pl.BlockSpec((tm, D_MODEL), row)]
    ins += [p.arr, gw.arr, pw.arr]
    in_specs += [pl.BlockSpec((None, tm, 256), lambda i: (p.idx, i, 0)), _wspec(gw), _wspec(pw)]
    out_specs = [pl.BlockSpec((tm, D_MODEL), row)]
    out_shape = [jax.ShapeDtypeStruct((t, D_MODEL), F32)]
    if final:
        ins.append(fg)
        in_specs.append(pl.BlockSpec(fg.shape, const))
        out_specs.append(pl.BlockSpec((tm, D_MODEL), row))
        out_shape.append(jax.ShapeDtypeStruct((t, D_MODEL), F32))
    return pl.pallas_call(
        functools.partial(_ple_kernel, moe=moe, final=final, tm=tm),
        grid=(t // tm,),
        in_specs=in_specs, out_specs=out_specs, out_shape=out_shape, scratch_shapes=scratch,
        compiler_params=_params(("arbitrary",)),
        name="ple",
    )(*ins)


def _tile(n, pref):
    return pref if n % pref == 0 else n


def _layer(h, p_l, lw, nb, seq, cache, cprev, pprev, final_g):
    t = nb * seq
    tm = _tile(seq, 512)
    if cache is None:
        c0 = jnp.zeros((nb, 1, LANES), F32)
        past = 0
    else:
        ck, cv, clf = cache
        past = ck.shape[1]
        akc, c0 = _cache_sum(clf, nb, past, _tile(past, 512))
    qs, kf, vf, kb, vb, lf, aq, ak, zc = _in_proj(h, lw['g1'], lw['w_in'], lw['bf'], c0, nb, seq, tm)
    if cache is None:
        a = _attention(qs, aq, kb, ak, vb, nb, seq, tm)
    else:
        lk = past + seq
        cat = lambda c_, n_: jnp.concatenate([c_.reshape(nb, past, 256), n_.reshape(nb, seq, 256)], axis=1
                                             ).reshape(nb * lk, 256)
        a = _attention_rows(qs, aq, cat(ck, kb), cat(akc, ak), cat(cv, vb), nb, seq, lk)
    tc = min(seq, 128)
    outs = _mixers(zc, cprev, pprev, lw, nb, seq, tm, tc, past, cache is not None)
    bcd, cst, pst = outs[:3]
    sgv = outs[3] if cache is not None else None
    tmo = _tile(t, 512)
    moe = 'router' in lw
    res = _out_proj(a, bcd, h, lw['w_o'], lw['g2'], lw.get('router'), tmo)
    h1, hn = res[0], res[1]
    tf = _tile(t, 256)
    nt = t // tf
    if moe:
        route, cnt = res[2], res[3]
        tabs, starts = _work_items(cnt[0, :N_EXPERTS].astype(jnp.int32), 2 * t, tf)
        experts = route[:, R_EXPERT:R_EXPERT + 2].astype(jnp.int32)
        pos = starts[experts] + route[:, R_RANK:R_RANK + 2].astype(jnp.int32)

        def pos_tiles(tm_):
            return pos.reshape(t // tm_, tm_, 2).transpose(0, 2, 1).reshape(t // tm_, 1, 2 * tm_)

        xs = _dispatch(hn, pos_tiles(tmo), 2 * t, tmo)
        ys = _ffn(xs, lw['wg'], lw['wu'], lw['wd'], tabs, tf)
        hres = _ple(h1, ys, route, pos_tiles(tf), p_l, lw['gw'], lw['plw_e'], final_g, tf)
    else:
        zeros = jnp.zeros((nt,), jnp.int32)
        tabs = (jnp.arange(nt, dtype=jnp.int32), zeros, zeros, zeros + tf, zeros + 1)
        ys = _ffn(hn, lw['wg'], lw['wu'], lw['wd'], tabs, tf)
        hres = _ple(h1, ys, None, None, p_l, lw['gw'], lw['plw_e'], final_g, tf)
    return hres, kf, vf, lf[:, :N_HEADS], cst[:, 2:], pst[:, 1:], sgv


def kernel(x_prompt, x_sample, p_prompt, p_sample, cache_k, cache_v, cache_logf, state_conv, state_pool, norm1_g, w_in, b_f, conv_dw_w, conv_dw_b, conv_ln_g, conv_ln_b, conv_pw_w, pool_w, pool_scale, sg_w, sg_b, w_o, norm2_g, ffn_w_gate, ffn_w_up, ffn_w_down, moe_router, moe_w_gate, moe_w_up, moe_w_down, ple_w, ple_gate_w, final_g):
    depth = w_in.shape[0]
    bp, sp, _ = x_prompt.shape
    bs, ss, _ = x_sample.shape
    past = cache_k.shape[2]
    hp = x_prompt.reshape(bp * sp, D_MODEL)
    hs = x_sample.reshape(bs * ss, D_MODEL)
    fg = final_g.reshape(1, D_MODEL)

    row = lambda x: x.reshape(depth, 1, -1)
    w_r = jnp.concatenate([w_in[..., 0:768], w_in[..., 772:2052],
                           jnp.pad(w_in[..., 768:772], ((0, 0), (0, 0), (0, LANES - N_HEADS)))], axis=-1).astype(BF16)
    plw = jnp.zeros((depth, GROUP_W, GROUP_W), F32)
    for gi in range(len(POOL_WINDOWS)):
        plw = plw.at[:, 64 * gi:64 * (gi + 1), 64 * gi:64 * (gi + 1)].set(pool_w[:, gi])
    stacked = {
        'g1': row(norm1_g), 'w_in': w_r, 'bf': row(jnp.pad(b_f, ((0, 0), (0, LANES - N_HEADS)))),
        'dw': jnp.pad(conv_dw_w, ((0, 0), (0, 1), (0, 0))), 'dwb': row(conv_dw_b),
        'lng': row(conv_ln_g), 'lnb': row(conv_ln_b), 'pw': conv_pw_w.astype(BF16), 'plw': plw.astype(BF16),
        'psc': row(pool_scale), 'w_o': w_o.astype(BF16), 'g2': row(norm2_g),
        'gw': ple_gate_w.astype(BF16), 'plw_e': ple_w.astype(BF16),
    }
    dense = {'wg': ffn_w_gate.astype(BF16)[:, None], 'wu': ffn_w_up.astype(BF16)[:, None],
             'wd': ffn_w_down.astype(BF16)[:, None]}
    sparse = {'wg': moe_w_gate.astype(BF16), 'wu': moe_w_up.astype(BF16), 'wd': moe_w_down.astype(BF16),
              'router': jnp.pad(moe_router, ((0, 0), (0, 0), (0, LANES - N_EXPERTS))).astype(BF16)}

    def sg_params(tc):
        return {'sgw': sg_w[:, :, :tc, :tc],
                'sgb': jnp.repeat(jnp.swapaxes(sg_b[:, :, :tc], 1, 2), GROUP_W // SG_GROUPS, axis=2)}

    sg_p, sg_s = sg_params(min(sp, 128)), sg_params(min(ss, 128))
    pp_all = p_prompt.reshape(depth, bp * sp, -1)
    ps_all = p_sample.reshape(depth, bs * ss, -1)
    ck_all = cache_k.reshape(depth, bs, past, 256).astype(BF16)
    cv_all = cache_v.reshape(depth, bs, past, 256).astype(BF16)
    clf_all = jnp.pad(cache_logf, ((0, 0), (0, 0), (0, 0), (0, LANES - N_HEADS))).reshape(depth, bs * past, LANES)
    sc_all = jnp.pad(state_conv, ((0, 0), (0, 0), (2, 0), (0, 0)))
    sp_all = jnp.pad(state_pool, ((0, 0), (0, 0), (1, 0), (0, 0)))
    cz, pz = jnp.zeros((bp, 32, GROUP_W), F32), jnp.zeros((bp, 16, GROUP_W), F32)

    outs_p, outs_s = [], []
    for l in range(depth):
        ffn_w = dense if l % 2 == 0 else sparse

        def layer_weights(sg):
            lw = {n: _LW(a, l) for n, a in {**stacked, **sg}.items()}
            lw.update({n: _LW(a, l // 2) for n, a in ffn_w.items()})
            return lw

        last = fg if l == depth - 1 else None
        res = _layer(hp, _LW(pp_all, l), layer_weights(sg_p), bp, sp, None, cz, pz, last)
        hp_res, outs = res[0], res[1:]
        outs_p.append(outs)
        res = _layer(hs, _LW(ps_all, l), layer_weights(sg_s), bs, ss, (ck_all[l], cv_all[l], clf_all[l]),
                     sc_all[l], sp_all[l], last)
        hs_res, outs = res[0], res[1:]
        outs_s.append(outs)
        if last is None:
            hp, hs = hp_res[0], hs_res[0]
        else:
            y_prompt, y_sample = hp_res[1], hs_res[1]

    def stack(outs, i, shape):
        return jnp.stack([o[i] for o in outs]).reshape((depth,) + shape)

    return (y_prompt.reshape(bp, sp, D_MODEL), y_sample.reshape(bs, ss, D_MODEL),
            stack(outs_p, 0, (bp, sp, N_HEADS, HEAD_DIM)), stack(outs_p, 1, (bp, sp, N_HEADS, HEAD_DIM)),
            stack(outs_p, 2, (bp, sp, N_HEADS)), stack(outs_p, 3, (bp, CONV_W - 1, GROUP_W)),
            stack(outs_p, 4, (bp, POOL_HIST, GROUP_W)),
            stack(outs_s, 0, (bs, ss, N_HEADS, HEAD_DIM)), stack(outs_s, 1, (bs, ss, N_HEADS, HEAD_DIM)),
            stack(outs_s, 2, (bs, ss, N_HEADS)), stack(outs_s, 3, (bs, CONV_W - 1, GROUP_W)),
            stack(outs_s, 4, (bs, POOL_HIST, GROUP_W)), stack(outs_s, 5, (bs, ss, GROUP_W)))
```

```python
import functools
from typing import NamedTuple

import numpy as np
import jax
import jax.numpy as jnp
from jax import lax
from jax.experimental import pallas as pl
from jax.experimental.pallas import tpu as pltpu

F32 = jnp.float32
BF16 = jnp.bfloat16
EPS = 1e-6

D_MODEL = 1024
GROUP_W = 256
N_HEADS = 4
HEAD_DIM = 64
CONV_W = 31
POOL_WINDOWS = (2, 4, 8, 16)
POOL_HIST = 15
SG_GROUPS = 4
N_EXPERTS = 8
LANES = 128
AUG_PER_HEAD = 6
VMEM_LIMIT = 56 * 1024 * 1024


def _params(sem, vmem=VMEM_LIMIT):
    return pltpu.CompilerParams(dimension_semantics=sem, vmem_limit_bytes=vmem)


def _dot(a, b):
    return jnp.dot(a, b, preferred_element_type=F32)


class _LW(NamedTuple):
    arr: jax.Array
    idx: int


def _wspec(w):
    nd = w.arr.ndim - 1
    return pl.BlockSpec((None,) + w.arr.shape[1:], lambda *_: (w.idx,) + (0,) * nd)


def _split3(x):
    hi = x.astype(BF16)
    r1 = x - hi.astype(F32)
    mid = r1.astype(BF16)
    lo = (r1 - mid.astype(F32)).astype(BF16)
    return jnp.concatenate([hi, mid, lo], axis=1)


def _running_sum(x, tril, carry_ref):
    cs = _dot(tril, _split3(x))
    c = carry_ref[...] + (cs[:, :LANES] + cs[:, LANES:2 * LANES] + cs[:, 2 * LANES:])
    n = x.shape[0]
    carry_ref[...] = c[n - 1:n, :]
    return c


def _placement(sign_k):
    pq = np.zeros((3 * LANES, 2 * LANES), np.float32)
    pk = np.zeros((3 * LANES, 2 * LANES), np.float32)
    oq = np.zeros((1, 2 * LANES), np.float32)
    ok = np.zeros((1, 2 * LANES), np.float32)
    for h in range(N_HEADS):
        p, j = divmod(h, 2)
        for i in range(3):
            pq[i * LANES + h, p * LANES + j * AUG_PER_HEAD + i] = 1.0
            oq[0, p * LANES + j * AUG_PER_HEAD + 3 + i] = 1.0
            pk[i * LANES + h, p * LANES + j * AUG_PER_HEAD + 3 + i] = sign_k
            ok[0, p * LANES + j * AUG_PER_HEAD + i] = 1.0
    return pq, pk, oq, ok


def _in_proj_kernel(h_ref, g_ref, w_ref, bf_ref, c0_ref, tril_ref, pq_ref, pk_ref, oq_ref, ok_ref,
                    qs_ref, kf_ref, vf_ref, kb_ref, vb_ref, lf_ref, aq_ref, ak_ref, zc_ref, carry_ref):
    @pl.when(pl.program_id(1) == 0)
    def _():
        carry_ref[...] = c0_ref[0]

    x = h_ref[...]
    hn = x * lax.rsqrt(jnp.mean(x * x, axis=-1, keepdims=True) + EPS) * g_ref[...]
    hb = hn.astype(BF16)
    qs_ref[...] = (_dot(hb, w_ref[:, 0:256]) * (HEAD_DIM ** -0.5)).astype(BF16)
    k = _dot(hb, w_ref[:, 256:512])
    kf_ref[...] = k
    kb_ref[...] = k.astype(BF16)
    v = _dot(hb, w_ref[:, 512:768])
    vf_ref[...] = v
    vb_ref[...] = v.astype(BF16)
    zc_ref[...] = _dot(hb, w_ref[:, 768:2048])
    zf = _dot(hb, w_ref[:, 2048:2176]) + bf_ref[...]
    lf = jnp.minimum(zf, 0.0) - jnp.log1p(jnp.exp(-jnp.abs(zf)))
    lane = lax.broadcasted_iota(jnp.int32, lf.shape, 1)
    lf = jnp.where(lane < N_HEADS, lf, 0.0)
    lf_ref[...] = lf
    cp = _split3(_running_sum(lf, tril_ref[...], carry_ref))
    aq_ref[...] = (_dot(cp, pq_ref[...]) + oq_ref[...]).astype(BF16)
    ak_ref[...] = (_dot(cp, pk_ref[...]) + ok_ref[...]).astype(BF16)


N_IN_PROJ_INPUTS, N_IN_PROJ_OUTPUTS, N_MIX_INPUTS = 10, 8, 11


def _in_mix_kernel(*refs, ts, tc, pos0, want_sgv):
    proj_in = refs[:N_IN_PROJ_INPUTS]
    mix_in = refs[N_IN_PROJ_INPUTS:N_IN_PROJ_INPUTS + N_MIX_INPUTS]
    rest = refs[N_IN_PROJ_INPUTS + N_MIX_INPUTS:]
    proj_out, rest = rest[:N_IN_PROJ_OUTPUTS], rest[N_IN_PROJ_OUTPUTS:]
    bcd_ref, cst_ref, pst_ref = rest[:3]
    sgv_ref = rest[3] if want_sgv else None
    carry_ref, zc_sc, xc_ref, xp_ref = rest[3 + want_sgv:]
    _in_proj_kernel(*proj_in, *proj_out, zc_sc, carry_ref)
    z = [zc_sc.at[:, j * GROUP_W:(j + 1) * GROUP_W] for j in range(5)]
    _mixers_kernel(*z, *mix_in, bcd_ref, cst_ref, pst_ref, sgv_ref, xc_ref, xp_ref, ts=ts, tc=tc, pos0=pos0)


def _in_mix(h, lw, c0, cprev, pprev, nb, seq, ts, tc, pos0, want_sgv):
    t = nb * seq
    ns = seq // ts
    tril = jnp.asarray(np.tril(np.ones((ts, ts), np.float32)), BF16)
    pq, pk, oq, ok = _placement(-1.0)
    row = lambda b, s: (b * ns + s, 0)
    const = lambda b, s: (0, 0)
    blk = lambda w_: pl.BlockSpec((ts, w_), row)
    full = lambda a: pl.BlockSpec(a.shape, const)
    perb = lambda r: pl.BlockSpec((1, r, GROUP_W), lambda b, s: (b, 0, 0))
    consts = [jnp.asarray(pq, BF16), jnp.asarray(pk, BF16), jnp.asarray(oq), jnp.asarray(ok)]
    small = [lw[n] for n in ('dw', 'dwb', 'lng', 'lnb', 'pw', 'plw', 'psc', 'sgw', 'sgb')]
    outs = [(256, BF16), (256, F32), (256, F32), (256, BF16), (256, BF16), (LANES, F32), (256, BF16), (256, BF16),
            (768, BF16)]
    out_specs = [blk(w_) for w_, _ in outs] + [perb(32), perb(16)]
    out_shape = [jax.ShapeDtypeStruct((t, w_), dt) for w_, dt in outs] + [
        jax.ShapeDtypeStruct((nb, 32, GROUP_W), F32), jax.ShapeDtypeStruct((nb, 16, GROUP_W), F32)]
    if want_sgv:
        out_specs.append(blk(GROUP_W))
        out_shape.append(jax.ShapeDtypeStruct((t, GROUP_W), F32))
    return pl.pallas_call(
        functools.partial(_in_mix_kernel, ts=ts, tc=tc, pos0=pos0, want_sgv=want_sgv),
        grid=(nb, ns),
        in_specs=[blk(D_MODEL), _wspec(lw['g1']), _wspec(lw['w_in']), _wspec(lw['bf']),
                  pl.BlockSpec((1, 1, LANES), lambda b, s: (b, 0, 0)), full(tril)] + [full(a) for a in consts]
                 + [perb(32), perb(16)] + [_wspec(a) for a in small],
        out_specs=out_specs,
        out_shape=out_shape,
        scratch_shapes=[pltpu.VMEM((1, LANES), F32), pltpu.VMEM((ts, 5 * GROUP_W), F32),
                        pltpu.VMEM((40 + ts, GROUP_W), F32), pltpu.VMEM((16 + ts, GROUP_W), F32)],
        compiler_params=_params(("arbitrary", "arbitrary")),
        name="in_mix",
    )(h, lw['g1'].arr, lw['w_in'].arr, lw['bf'].arr, c0, tril, *consts, cprev, pprev, *[a.arr for a in small])


def _cache_sum_kernel(lf_ref, tril_ref, pk_ref, ok_ref, ak_ref, tot_ref, carry_ref):
    @pl.when(pl.program_id(1) == 0)
    def _():
        carry_ref[...] = jnp.zeros_like(carry_ref)

    cp = _split3(_running_sum(lf_ref[...], tril_ref[...], carry_ref))
    ak_ref[...] = (_dot(cp, pk_ref[...]) + ok_ref[...]).astype(BF16)
    tot_ref[0] = carry_ref[...]


def _cache_sum(lfp, nb, past, tm):
    ns = past // tm
    tril = jnp.asarray(np.tril(np.ones((tm, tm), np.float32)), BF16)
    _, pk, _, ok = _placement(-1.0)
    pk, ok = jnp.asarray(pk, BF16), jnp.asarray(ok)
    const = lambda b, s: (0, 0)
    return pl.pallas_call(
        _cache_sum_kernel,
        grid=(nb, ns),
        in_specs=[pl.BlockSpec((tm, LANES), lambda b, s: (b * ns + s, 0)), pl.BlockSpec(tril.shape, const),
                  pl.BlockSpec(pk.shape, const), pl.BlockSpec(ok.shape, const)],
        out_specs=[pl.BlockSpec((tm, 256), lambda b, s: (b * ns + s, 0)),
                   pl.BlockSpec((1, 1, LANES), lambda b, s: (b, 0, 0))],
        out_shape=[jax.ShapeDtypeStruct((nb * past, 256), BF16), jax.ShapeDtypeStruct((nb, 1, LANES), F32)],
        scratch_shapes=[pltpu.VMEM((1, LANES), F32)],
        compiler_params=_params(("arbitrary", "arbitrary")),
        name="cache_sum",
    )(lfp, tril, pk, ok)


NEG_BIG = -1e30


def _head_lhs(q, aq, j, lane):
    zero = jnp.zeros((), BF16)
    qm = jnp.where((lane >= HEAD_DIM * j) & (lane < HEAD_DIM * (j + 1)), q, zero)
    am = jnp.where((lane >= AUG_PER_HEAD * j) & (lane < AUG_PER_HEAD * (j + 1)), aq, zero)
    return jnp.concatenate([qm, am], axis=1)


def _scores(lhs, k, ak):
    rhs = jnp.concatenate([k, ak], axis=1)
    return lax.dot_general(lhs, rhs, (((1,), (1,)), ((), ())), preferred_element_type=F32)


def _attn_kernel(q_ref, aq_ref, k_ref, ak_ref, v_ref, o_ref, m_sc, l_sc, acc_sc, e_sc, f_sc, *, t):
    qi = pl.program_id(1)
    lane = lax.broadcasted_iota(jnp.int32, (1, LANES), 1)
    row = lax.broadcasted_iota(jnp.int32, (t, t), 0)
    col = lax.broadcasted_iota(jnp.int32, (t, t), 1)
    nct = t // LANES
    for p in range(2):
        cs = slice(p * LANES, (p + 1) * LANES)
        lhs = [_head_lhs(q_ref[:, cs], aq_ref[:, cs], j, lane) for j in range(2)]

        def scores(j, ki, masked):
            rows = pl.ds(pl.multiple_of(ki * t, t), t)
            s = _scores(lhs[j], k_ref[rows, cs], ak_ref[rows, cs])
            if masked:
                s = jnp.where(col <= row, s, -jnp.inf)
            return [s[:, c * LANES:(c + 1) * LANES] for c in range(nct)]

        def stats(ki, masked):
            for j in range(2):
                parts = scores(j, ki, masked)
                m_old = m_sc[j]
                m_new = functools.reduce(jnp.maximum, parts, m_old)
                tot = l_sc[j] * jnp.exp(m_old - m_new)
                for c, part in enumerate(parts):
                    e = jnp.exp(part - m_new)
                    e_sc[j, ki, :, c * LANES:(c + 1) * LANES] = e
                    tot = tot + e
                l_sc[j] = tot
                m_sc[j] = m_new
                f_sc[j, ki] = m_new

        def body(ki, carry):
            stats(ki, False)
            return carry

        m_sc[...] = jnp.full_like(m_sc, NEG_BIG)
        l_sc[...] = jnp.zeros_like(l_sc)
        lax.fori_loop(0, qi, body, 0)
        stats(qi, True)
        for j in range(2):
            m_lane = m_sc[j]
            m_row = jnp.max(m_lane, axis=1, keepdims=True)
            l_row = jnp.sum(l_sc[j] * jnp.exp(m_lane - m_row), axis=1, keepdims=True)
            m_sc[j] = jnp.broadcast_to(m_row, m_lane.shape)
            l_sc[j] = jnp.broadcast_to(1.0 / l_row, m_lane.shape)

        def rescale(ki, carry):
            for j in range(2):
                f_sc[j, ki] = jnp.exp(f_sc[j, ki] - m_sc[j]) * l_sc[j]
            return carry

        lax.fori_loop(0, qi + 1, rescale, 0)
        acc_sc[...] = jnp.zeros_like(acc_sc)

        def accumulate(ki, carry):
            v = v_ref[pl.ds(pl.multiple_of(ki * t, t), t), cs]
            for j in range(2):
                f = f_sc[j, ki]
                prob = jnp.concatenate([(e_sc[j, ki, :, c * LANES:(c + 1) * LANES] * f).astype(BF16)
                                        for c in range(nct)], axis=1)
                acc_sc[j] += _dot(prob, v)
            return carry

        lax.fori_loop(0, qi + 1, accumulate, 0)
        o_ref[:, cs] = jnp.where(lane < HEAD_DIM, acc_sc[0], acc_sc[1]).astype(BF16)


def _attention(qs, aq, kb, ak, vb, nb, seq, t):
    assert seq % t == 0 and t % LANES == 0
    nq = seq // t
    qblk = pl.BlockSpec((t, 256), lambda b, i: (b * nq + i, 0))
    kblk = pl.BlockSpec((seq, 256), lambda b, i: (b, 0))
    return pl.pallas_call(
        functools.partial(_attn_kernel, t=t),
        grid=(nb, nq),
        in_specs=[qblk, qblk, kblk, kblk, kblk],
        out_specs=qblk,
        out_shape=jax.ShapeDtypeStruct((nb * seq, 256), BF16),
        scratch_shapes=[pltpu.VMEM((2, t, LANES), F32)] * 3 + [pltpu.VMEM((2, nq, t, t), F32),
                                                               pltpu.VMEM((2, nq, t, LANES), F32)],
        compiler_params=_params(("arbitrary", "arbitrary")),
        name="attn",
    )(qs, aq, kb, ak, vb)


def _attn_rows_kernel(q_ref, aq_ref, k_ref, ak_ref, v_ref, o_ref, *, past):
    lq, lk = q_ref.shape[0], k_ref.shape[0]
    lane = lax.broadcasted_iota(jnp.int32, (1, LANES), 1)
    row = lax.broadcasted_iota(jnp.int32, (lq, lk), 0)
    col = lax.broadcasted_iota(jnp.int32, (lq, lk), 1)
    for p in range(2):
        cs = slice(p * LANES, (p + 1) * LANES)
        outs = []
        for j in range(2):
            s = _scores(_head_lhs(q_ref[:, cs], aq_ref[:, cs], j, lane), k_ref[:, cs], ak_ref[:, cs])
            s = jnp.where(col <= past + row, s, -jnp.inf)
            e = jnp.exp(s - jnp.max(s, axis=1, keepdims=True))
            prob = (e * (1.0 / jnp.sum(e, axis=1, keepdims=True))).astype(BF16)
            outs.append(_dot(prob, v_ref[:, cs]))
        o_ref[:, cs] = jnp.where(lane < HEAD_DIM, outs[0], outs[1]).astype(BF16)


def _attention_rows(qs, aq, kb, ak, vb, nb, lq, lk):
    qblk = pl.BlockSpec((lq, 256), lambda b: (b, 0))
    kblk = pl.BlockSpec((lk, 256), lambda b: (b, 0))
    return pl.pallas_call(
        functools.partial(_attn_rows_kernel, past=lk - lq),
        grid=(nb,),
        in_specs=[qblk, qblk, kblk, kblk, kblk],
        out_specs=qblk,
        out_shape=jax.ShapeDtypeStruct((nb * lq, 256), BF16),
        compiler_params=_params(("arbitrary",)),
        name="attn_rows",
    )(qs, aq, kb, ak, vb)


ROW_CHUNK = 64


def _gelu(x):
    return 0.5 * x * (1.0 + jnp.tanh(0.7978845608028654 * (x + 0.044715 * (x * x * x))))


def _mixers_kernel(ca_ref, cg_ref, px_ref, su_ref, sv_ref, cprev_ref, pprev_ref, dw_ref, dwb_ref, lng_ref, lnb_ref,
                   pw_ref, plw_ref, psc_ref, sgw_ref, sgb_ref,
                   bcd_ref, cst_ref, pst_ref, sgv_ref, xc_ref, xp_ref, *, ts, tc, pos0):
    s = pl.program_id(1)

    @pl.when(s == 0)
    def _():
        xc_ref[0:32, :] = cprev_ref[0].astype(BF16).astype(F32)
        xp_ref[0:16, :] = pprev_ref[0]

    @pl.when(s > 0)
    def _():
        xc_ref[0:32, :] = xc_ref[ts:ts + 32, :]
        xp_ref[0:16, :] = xp_ref[ts:ts + 16, :]

    g = cg_ref[...]
    u = ca_ref[...] * (1.0 / (1.0 + jnp.exp(-g)))
    xc_ref[32:32 + ts, :] = u.astype(BF16).astype(F32)
    xc_ref[32 + ts:40 + ts, :] = jnp.zeros((8, GROUP_W), F32)
    cst_ref[0] = u[ts - 32:ts, :]
    xp_ref[16:16 + ts, :] = px_ref[...]
    pst_ref[0] = xp_ref[ts:ts + 16, :]

    rc = min(ts, ROW_CHUNK)
    lane = lax.broadcasted_iota(jnp.int32, (rc, GROUP_W), 1)
    conv_act, pool_dlt = [], []
    for r0 in range(0, ts, rc):
        y = jnp.zeros((rc, GROUP_W), F32)
        for r in range(8):
            part = jnp.zeros((rc + 8, GROUP_W), F32)
            for a in range(r if r >= 2 else r + 8, CONV_W + 2, 8):
                part = part + xc_ref[r0 + a - r:r0 + a - r + rc + 8, :] * dw_ref[a - 2:a - 1, :]
            y = y + part[r:r + rc, :]
        y = y + dwb_ref[...]
        mu = jnp.mean(y, axis=-1, keepdims=True)
        var = jnp.mean(jnp.square(y - mu), axis=-1, keepdims=True)
        y = (y - mu) * lax.rsqrt(var + EPS) * lng_ref[...] + lnb_ref[...]
        conv_act.append((y * (1.0 / (1.0 + jnp.exp(-y)))).astype(BF16))

        x = xp_ref[16 + r0:16 + r0 + rc, :]
        pos = (pos0 + s * ts + r0 + lax.broadcasted_iota(jnp.int32, (rc, 1), 0)).astype(F32)
        acc = x
        mean = jnp.zeros((rc, GROUP_W), F32)
        wi = 0
        for i in range(1, POOL_WINDOWS[-1] + 1):
            if i > 1:
                acc = acc + xp_ref[16 - (i - 1) + r0:16 - (i - 1) + r0 + rc, :]
            if i == POOL_WINDOWS[wi]:
                cnt = jnp.minimum(pos + 1.0, float(i))
                mean = jnp.where((lane >= 64 * wi) & (lane < 64 * (wi + 1)), acc / cnt, mean)
                wi += 1
        pool_dlt.append((mean - x).astype(BF16))
    bcd_ref[:, 0:256] = _dot(jnp.concatenate(conv_act, axis=0), pw_ref[...]).astype(BF16)
    bcd_ref[:, 256:512] = (_dot(jnp.concatenate(pool_dlt, axis=0), plw_ref[...]) * psc_ref[...]).astype(BF16)

    r = lax.broadcasted_iota(jnp.int32, (tc, tc), 0)
    c = lax.broadcasted_iota(jnp.int32, (tc, tc), 1)
    lane_c = lax.broadcasted_iota(jnp.int32, (tc, GROUP_W), 1)
    wts = [jnp.where(c <= r, sgw_ref[gi], 0.0).astype(BF16) for gi in range(SG_GROUPS)]
    for ci in range(ts // tc):
        uu = _gelu(su_ref[ci * tc:(ci + 1) * tc, :])
        vv = _gelu(sv_ref[ci * tc:(ci + 1) * tc, :])
        if sgv_ref is not None:
            sgv_ref[ci * tc:(ci + 1) * tc, :] = vv
        m = sgb_ref[...]
        for gi in range(SG_GROUPS):
            vg = jnp.where((lane_c >= 64 * gi) & (lane_c < 64 * (gi + 1)), vv, 0.0).astype(BF16)
            m = m + _dot(wts[gi], vg)
        bcd_ref[ci * tc:(ci + 1) * tc, 512:768] = (uu * m).astype(BF16)


R_GATE, R_EXPERT, R_RANK = 0, 2, 4


def _out_proj_kernel(a_ref, bcd_ref, h_ref, wo_ref, g_ref, *rest, moe):
    if moe:
        router_ref, stril_ref, h1_ref, hn_ref, route_ref, cnt_ref, carry_ref = rest
    else:
        h1_ref, hn_ref = rest
    mix = _dot(a_ref[...], wo_ref[0:256, :]) + _dot(bcd_ref[...], wo_ref[256:1024, :])
    h1 = h_ref[...] + mix
    hn = h1 * lax.rsqrt(jnp.mean(h1 * h1, axis=-1, keepdims=True) + EPS) * g_ref[...]
    h1_ref[...] = h1
    hn_ref[...] = hn.astype(hn_ref.dtype)
    if moe:
        @pl.when(pl.program_id(0) == 0)
        def _():
            carry_ref[...] = jnp.zeros_like(carry_ref)

        logits = _dot(hn.astype(BF16), router_ref[...])
        lane = lax.broadcasted_iota(jnp.int32, logits.shape, 1)
        logits = jnp.where(lane < N_EXPERTS, logits, -jnp.inf)
        v1 = jnp.max(logits, axis=1, keepdims=True)
        i1 = jnp.min(jnp.where(logits == v1, lane, LANES), axis=1, keepdims=True)
        rest_l = jnp.where(lane == i1, -jnp.inf, logits)
        v2 = jnp.max(rest_l, axis=1, keepdims=True)
        i2 = jnp.min(jnp.where(rest_l == v2, lane, LANES), axis=1, keepdims=True)
        e = jnp.exp(v2 - v1)
        g1 = 1.0 / (1.0 + e)
        oh1, oh2 = lane == i1, lane == i2
        oh = jnp.where(oh1 | oh2, 1.0, 0.0)
        before = _dot(stril_ref[...], oh.astype(BF16)) + carry_ref[...]
        r1 = jnp.sum(jnp.where(oh1, before, 0.0), axis=1, keepdims=True)
        r2 = jnp.sum(jnp.where(oh2, before, 0.0), axis=1, keepdims=True)
        carry_ref[...] += jnp.sum(oh, axis=0, keepdims=True)
        cnt_ref[...] = carry_ref[...]
        rec = jnp.zeros_like(logits)
        for ln, val in ((R_GATE, g1), (R_GATE + 1, e * g1), (R_EXPERT, i1.astype(F32)), (R_EXPERT + 1, i2.astype(F32)),
                        (R_RANK, r1), (R_RANK + 1, r2)):
            rec = jnp.where(lane == ln, val, rec)
        route_ref[...] = rec


def _out_proj(a, bcd, h, wo, g2, router, tm):
    t = h.shape[0]
    moe = router is not None
    row = lambda i: (i, 0)
    const = lambda i: (0, 0)
    ins = [a, bcd, h, wo.arr, g2.arr]
    in_specs = [pl.BlockSpec((tm, 256), row), pl.BlockSpec((tm, 768), row), pl.BlockSpec((tm, D_MODEL), row),
                _wspec(wo), _wspec(g2)]
    out_specs = [pl.BlockSpec((tm, D_MODEL), row), pl.BlockSpec((tm, D_MODEL), row)]
    out_shape = [jax.ShapeDtypeStruct((t, D_MODEL), F32), jax.ShapeDtypeStruct((t, D_MODEL), F32 if moe else BF16)]
    scratch = []
    if moe:
        stril = jnp.asarray(np.tril(np.ones((tm, tm), np.float32), -1), BF16)
        ins += [router.arr, stril]
        in_specs += [_wspec(router), pl.BlockSpec(stril.shape, const)]
        out_specs += [pl.BlockSpec((tm, LANES), row), pl.BlockSpec((1, LANES), const)]
        out_shape += [jax.ShapeDtypeStruct((t, LANES), F32), jax.ShapeDtypeStruct((1, LANES), F32)]
        scratch = [pltpu.VMEM((1, LANES), F32)]
    return pl.pallas_call(
        functools.partial(_out_proj_kernel, moe=moe),
        grid=(t // tm,),
        in_specs=in_specs, out_specs=out_specs, out_shape=out_shape, scratch_shapes=scratch,
        compiler_params=_params(("arbitrary",)),
        name="out_proj",
    )(*ins)


DMA_UNROLL = 8


def _row_copy(src_ref, src_row, dst_ref, dst_row, sem):
    return pltpu.make_async_copy(src_ref.at[pl.ds(src_row, 1)], dst_ref.at[pl.ds(dst_row, 1)], sem)


def _dispatch_kernel(pos_ref, x_ref, xs_ref, sem, *, tm):
    def issue(r, carry):
        for k in range(2):
            _row_copy(x_ref, r, xs_ref, pos_ref[0, 0, k * tm + r], sem).start(priority=k)
        return carry

    def drain(r, carry):
        for k in range(2):
            _row_copy(x_ref, r, xs_ref, pos_ref[0, 0, k * tm + r], sem).wait()
        return carry

    lax.fori_loop(0, tm, issue, 0, unroll=DMA_UNROLL)
    lax.fori_loop(0, tm, drain, 0, unroll=DMA_UNROLL)


def _dispatch(hn, pos_tiles, n_rows, tm):
    t = hn.shape[0]
    return pl.pallas_call(
        functools.partial(_dispatch_kernel, tm=tm),
        grid=(t // tm,),
        in_specs=[pl.BlockSpec((1, 1, 2 * tm), lambda i: (i, 0, 0), memory_space=pltpu.SMEM),
                  pl.BlockSpec((tm, D_MODEL), lambda i: (i, 0))],
        out_specs=pl.BlockSpec(memory_space=pl.ANY),
        out_shape=jax.ShapeDtypeStruct((n_rows, D_MODEL), F32),
        scratch_shapes=[pltpu.SemaphoreType.DMA],
        compiler_params=_params(("arbitrary",)),
        name="dispatch",
    )(pos_tiles, hn)


def _ffn_kernel(wt_ref, we_ref, lo_ref, hi_ref, first_ref, x_ref, wg_ref, wu_ref, wd_ref, y_ref):
    w = pl.program_id(0)
    lo, hi = lo_ref[w], hi_ref[w]

    @pl.when(hi > lo)
    def _():
        rowi = lax.broadcasted_iota(jnp.int32, (x_ref.shape[0], 1), 0)
        x = jnp.where((rowi >= lo) & (rowi < hi), x_ref[...], jnp.zeros((), x_ref.dtype)).astype(BF16)
        gate = _dot(x, wg_ref[0])
        up = _dot(x, wu_ref[0])
        act = (gate * (1.0 / (1.0 + jnp.exp(-gate))) * up).astype(BF16)
        y = _dot(act, wd_ref[0])

        @pl.when(first_ref[w] == 1)
        def _():
            y_ref[...] = y

        @pl.when(first_ref[w] == 0)
        def _():
            y_ref[...] += y


def _ffn(xs, wg, wu, wd, tabs, tm):
    d_ff = wg.arr.shape[-1]
    xmap = lambda w, wt, we, lo, hi, fi: (wt[w], 0)
    wmap = lambda w, wt, we, lo, hi, fi: (wg.idx, we[w], 0, 0)
    grid_spec = pltpu.PrefetchScalarGridSpec(
        num_scalar_prefetch=5,
        grid=(tabs[0].shape[0],),
        in_specs=[pl.BlockSpec((tm, D_MODEL), xmap),
                  pl.BlockSpec((None, 1, D_MODEL, d_ff), wmap), pl.BlockSpec((None, 1, D_MODEL, d_ff), wmap),
                  pl.BlockSpec((None, 1, d_ff, D_MODEL), wmap)],
        out_specs=pl.BlockSpec((tm, D_MODEL), xmap),
    )
    return pl.pallas_call(
        _ffn_kernel,
        grid_spec=grid_spec,
        out_shape=jax.ShapeDtypeStruct((xs.shape[0], D_MODEL), F32),
        compiler_params=_params(("arbitrary",)),
        name="ffn",
    )(*tabs, xs, wg.arr, wu.arr, wd.arr)


def _work_items(counts, n_rows, tm):
    i32 = jnp.int32
    ends = jnp.cumsum(counts)
    starts = ends - counts
    t_first = starts // tm
    n_e = jnp.where(counts > 0, (ends - 1) // tm - t_first + 1, 0)
    w_end = jnp.cumsum(n_e)
    w_start = w_end - n_e
    n_work = n_rows // tm + counts.shape[0] - 1
    w = jnp.arange(n_work, dtype=i32)
    valid = w < w_end[-1]
    e = jnp.minimum(jnp.sum((w[:, None] >= w_end[None, :]).astype(i32), axis=1), counts.shape[0] - 1)
    tile = jnp.where(valid, t_first[e] + (w - w_start[e]), n_rows // tm - 1).astype(i32)
    lo = jnp.where(valid, jnp.maximum(starts[e] - tile * tm, 0), 0).astype(i32)
    hi = jnp.where(valid, jnp.minimum(ends[e] - tile * tm, tm), 0).astype(i32)
    e = jnp.where(valid, e, e[jnp.maximum(w_end[-1] - 1, 0)]).astype(i32)
    first = jnp.concatenate([jnp.ones((1,), i32), (tile[1:] != tile[:-1]).astype(i32)])
    return (tile, e, lo, hi, first), starts


def _ple_kernel(*refs, moe, final, tm):
    refs = list(refs)
    if moe:
        pos_ref = refs.pop(0)
    h1_ref, y_ref = refs[0], refs[1]
    k = 2
    if moe:
        route_ref = refs[k]
        k += 1
    p_ref, gw_ref, pw_ref = refs[k:k + 3]
    k += 3
    if final:
        fg_ref = refs[k]
        k += 1
    h_ref = refs[k]
    if moe:
        ysc_ref, sem = refs[-2], refs[-1]

        def issue(r, carry):
            for s in range(2):
                _row_copy(y_ref, pos_ref[0, 0, s * tm + r], ysc_ref.at[s], r, sem).start(priority=s)
            return carry

        def drain(r, carry):
            for s in range(2):
                _row_copy(y_ref, pos_ref[0, 0, s * tm + r], ysc_ref.at[s], r, sem).wait()
            return carry

        lax.fori_loop(0, tm, issue, 0, unroll=DMA_UNROLL)
        lax.fori_loop(0, tm, drain, 0, unroll=DMA_UNROLL)
        rec = route_ref[...]
        lane = lax.broadcasted_iota(jnp.int32, rec.shape, 1)
        g1 = jnp.sum(jnp.where(lane == R_GATE, rec, 0.0), axis=1, keepdims=True)
        g2 = jnp.sum(jnp.where(lane == R_GATE + 1, rec, 0.0), axis=1, keepdims=True)
        h2 = h1_ref[...] + (g1 * ysc_ref[0] + g2 * ysc_ref[1])
    else:
        h2 = h1_ref[...] + y_ref[...]
    gate = _dot(h2.astype(BF16), gw_ref[...])
    pe = _dot(p_ref[...].astype(BF16), pw_ref[...])
    h3 = h2 + (1.0 / (1.0 + jnp.exp(-gate))) * pe
    h_ref[...] = h3
    if final:
        refs[k + 1][...] = h3 * lax.rsqrt(jnp.mean(h3 * h3, axis=-1, keepdims=True) + EPS) * fg_ref[...]


def _ple(h1, ys, route, pos_tiles, p, gw, pw, fg, tm):
    t = h1.shape[0]
    moe = route is not None
    final = fg is not None
    row = lambda i: (i, 0)
    const = lambda i: (0, 0)
    ins, in_specs, scratch = [], [], []
    if moe:
        ins += [pos_tiles, h1, ys, route]
        in_specs += [pl.BlockSpec((1, 1, 2 * tm), lambda i: (i, 0, 0), memory_space=pltpu.SMEM),
                     pl.BlockSpec((tm, D_MODEL), row), pl.BlockSpec(memory_space=pl.ANY), pl.BlockSpec((tm, LANES), row)]
        scratch = [pltpu.VMEM((2, tm, D_MODEL), F32), pltpu.SemaphoreType.DMA]
    else:
        ins += [h1, ys]
        in_specs += [pl.BlockSpec((tm, D_MODEL), row), pl.BlockSpec((tm, D_MODEL), row)]
    ins += [p.arr, gw.arr, pw.arr]
    in_specs += [pl.BlockSpec((None, tm, 256), lambda i: (p.idx, i, 0)), _wspec(gw), _wspec(pw)]
    out_specs = [pl.BlockSpec((tm, D_MODEL), row)]
    out_shape = [jax.ShapeDtypeStruct((t, D_MODEL), F32)]
    if final:
        ins.append(fg)
        in_specs.append(pl.BlockSpec(fg.shape, const))
        out_specs.append(pl.BlockSpec((tm, D_MODEL), row))
        out_shape.append(jax.ShapeDtypeStruct((t, D_MODEL), F32))
    return pl.pallas_call(
        functools.partial(_ple_kernel, moe=moe, final=final, tm=tm),
        grid=(t // tm,),
        in_specs=in_specs, out_specs=out_specs, out_shape=out_shape, scratch_shapes=scratch,
        compiler_params=_params(("arbitrary",)),
        name="ple",
    )(*ins)


def _tile(n, pref):
    return pref if n % pref == 0 else n


def _layer(h, p_l, lw, nb, seq, cache, cprev, pprev, final_g):
    t = nb * seq
    tm = _tile(seq, 512)
    if cache is None:
        c0 = jnp.zeros((nb, 1, LANES), F32)
        past = 0
    else:
        ck, cv, clf = cache
        past = ck.shape[1]
        akc, c0 = _cache_sum(clf, nb, past, _tile(past, 512))
    outs = _in_mix(h, lw, c0, cprev, pprev, nb, seq, tm, min(seq, 128), past, cache is not None)
    qs, kf, vf, kb, vb, lf, aq, ak, bcd, cst, pst = outs[:11]
    sgv = outs[11] if cache is not None else None
    if cache is None:
        a = _attention(qs, aq, kb, ak, vb, nb, seq, tm)
    else:
        lk = past + seq
        cat = lambda c_, n_: jnp.concatenate([c_.reshape(nb, past, 256), n_.reshape(nb, seq, 256)], axis=1
                                             ).reshape(nb * lk, 256)
        a = _attention_rows(qs, aq, cat(ck, kb), cat(akc, ak), cat(cv, vb), nb, seq, lk)
    tmo = _tile(t, 512)
    moe = 'router' in lw
    res = _out_proj(a, bcd, h, lw['w_o'], lw['g2'], lw.get('router'), tmo)
    h1, hn = res[0], res[1]
    tf = _tile(t, 256)
    nt = t // tf
    if moe:
        route, cnt = res[2], res[3]
        tabs, starts = _work_items(cnt[0, :N_EXPERTS].astype(jnp.int32), 2 * t, tf)
        experts = route[:, R_EXPERT:R_EXPERT + 2].astype(jnp.int32)
        pos = starts[experts] + route[:, R_RANK:R_RANK + 2].astype(jnp.int32)

        def pos_tiles(tm_):
            return pos.reshape(t // tm_, tm_, 2).transpose(0, 2, 1).reshape(t // tm_, 1, 2 * tm_)

        xs = _dispatch(hn, pos_tiles(tmo), 2 * t, tmo)
        ys = _ffn(xs, lw['wg'], lw['wu'], lw['wd'], tabs, tf)
        hres = _ple(h1, ys, route, pos_tiles(tf), p_l, lw['gw'], lw['plw_e'], final_g, tf)
    else:
        zeros = jnp.zeros((nt,), jnp.int32)
        tabs = (jnp.arange(nt, dtype=jnp.int32), zeros, zeros, zeros + tf, zeros + 1)
        ys = _ffn(hn, lw['wg'], lw['wu'], lw['wd'], tabs, tf)
        hres = _ple(h1, ys, None, None, p_l, lw['gw'], lw['plw_e'], final_g, tf)
    return hres, kf, vf, lf[:, :N_HEADS], cst[:, 2:], pst[:, 1:], sgv


def kernel(x_prompt, x_sample, p_prompt, p_sample, cache_k, cache_v, cache_logf, state_conv, state_pool, norm1_g, w_in, b_f, conv_dw_w, conv_dw_b, conv_ln_g, conv_ln_b, conv_pw_w, pool_w, pool_scale, sg_w, sg_b, w_o, norm2_g, ffn_w_gate, ffn_w_up, ffn_w_down, moe_router, moe_w_gate, moe_w_up, moe_w_down, ple_w, ple_gate_w, final_g):
    depth = w_in.shape[0]
    bp, sp, _ = x_prompt.shape
    bs, ss, _ = x_sample.shape
    past = cache_k.shape[2]
    hp = x_prompt.reshape(bp * sp, D_MODEL)
    hs = x_sample.reshape(bs * ss, D_MODEL)
    fg = final_g.reshape(1, D_MODEL)

    row = lambda x: x.reshape(depth, 1, -1)
    w_r = jnp.concatenate([w_in[..., 0:768], w_in[..., 772:2052],
                           jnp.pad(w_in[..., 768:772], ((0, 0), (0, 0), (0, LANES - N_HEADS)))], axis=-1).astype(BF16)
    plw = jnp.zeros((depth, GROUP_W, GROUP_W), F32)
    for gi in range(len(POOL_WINDOWS)):
        plw = plw.at[:, 64 * gi:64 * (gi + 1), 64 * gi:64 * (gi + 1)].set(pool_w[:, gi])
    stacked = {
        'g1': row(norm1_g), 'w_in': w_r, 'bf': row(jnp.pad(b_f, ((0, 0), (0, LANES - N_HEADS)))),
        'dw': jnp.pad(conv_dw_w, ((0, 0), (0, 1), (0, 0))), 'dwb': row(conv_dw_b),
        'lng': row(conv_ln_g), 'lnb': row(conv_ln_b), 'pw': conv_pw_w.astype(BF16), 'plw': plw.astype(BF16),
        'psc': row(pool_scale), 'w_o': w_o.astype(BF16), 'g2': row(norm2_g),
        'gw': ple_gate_w.astype(BF16), 'plw_e': ple_w.astype(BF16),
    }
    dense = {'wg': ffn_w_gate.astype(BF16)[:, None], 'wu': ffn_w_up.astype(BF16)[:, None],
             'wd': ffn_w_down.astype(BF16)[:, None]}
    sparse = {'wg': moe_w_gate.astype(BF16), 'wu': moe_w_up.astype(BF16), 'wd': moe_w_down.astype(BF16),
              'router': jnp.pad(moe_router, ((0, 0), (0, 0), (0, LANES - N_EXPERTS))).astype(BF16)}

    def sg_params(tc):
        return {'sgw': sg_w[:, :, :tc, :tc],
                'sgb': jnp.repeat(jnp.swapaxes(sg_b[:, :, :tc], 1, 2), GROUP_W // SG_GROUPS, axis=2)}

    sg_p, sg_s = sg_params(min(sp, 128)), sg_params(min(ss, 128))
    pp_all = p_prompt.reshape(depth, bp * sp, -1)
    ps_all = p_sample.reshape(depth, bs * ss, -1)
    ck_all = cache_k.reshape(depth, bs, past, 256).astype(BF16)
    cv_all = cache_v.reshape(depth, bs, past, 256).astype(BF16)
    clf_all = jnp.pad(cache_logf, ((0, 0), (0, 0), (0, 0), (0, LANES - N_HEADS))).reshape(depth, bs * past, LANES)
    sc_all = jnp.pad(state_conv, ((0, 0), (0, 0), (2, 0), (0, 0)))
    sp_all = jnp.pad(state_pool, ((0, 0), (0, 0), (1, 0), (0, 0)))
    cz, pz = jnp.zeros((bp, 32, GROUP_W), F32), jnp.zeros((bp, 16, GROUP_W), F32)

    outs_p, outs_s = [], []
    for l in range(depth):
        ffn_w = dense if l % 2 == 0 else sparse

        def layer_weights(sg):
            lw = {n: _LW(a, l) for n, a in {**stacked, **sg}.items()}
            lw.update({n: _LW(a, l // 2) for n, a in ffn_w.items()})
            return lw

        last = fg if l == depth - 1 else None
        res = _layer(hp, _LW(pp_all, l), layer_weights(sg_p), bp, sp, None, cz, pz, last)
        hp_res, outs = res[0], res[1:]
        outs_p.append(outs)
        res = _layer(hs, _LW(ps_all, l), layer_weights(sg_s), bs, ss, (ck_all[l], cv_all[l], clf_all[l]),
                     sc_all[l], sp_all[l], last)
        hs_res, outs = res[0], res[1:]
        outs_s.append(outs)
        if last is None:
            hp, hs = hp_res[0], hs_res[0]
        else:
            y_prompt, y_sample = hp_res[1], hs_res[1]

    def stack(outs, i, shape):
        return jnp.stack([o[i] for o in outs]).reshape((depth,) + shape)

    return (y_prompt.reshape(bp, sp, D_MODEL), y_sample.reshape(bs, ss, D_MODEL),
            stack(outs_p, 0, (bp, sp, N_HEADS, HEAD_DIM)), stack(outs_p, 1, (bp, sp, N_HEADS, HEAD_DIM)),
            stack(outs_p, 2, (bp, sp, N_HEADS)), stack(outs_p, 3, (bp, CONV_W - 1, GROUP_W)),
            stack(outs_p, 4, (bp, POOL_HIST, GROUP_W)),
            stack(outs_s, 0, (bs, ss, N_HEADS, HEAD_DIM)), stack(outs_s, 1, (bs, ss, N_HEADS, HEAD_DIM)),
            stack(outs_s, 2, (bs, ss, N_HEADS)), stack(outs_s, 3, (bs, CONV_W - 1, GROUP_W)),
            stack(outs_s, 4, (bs, POOL_HIST, GROUP_W)), stack(outs_s, 5, (bs, ss, GROUP_W)))
```

```python
import functools
from typing import NamedTuple

import numpy as np
import jax
import jax.numpy as jnp
from jax import lax
from jax.experimental import pallas as pl
from jax.experimental.pallas import tpu as pltpu

F32 = jnp.float32
BF16 = jnp.bfloat16
EPS = 1e-6

D_MODEL = 1024
GROUP_W = 256
N_HEADS = 4
HEAD_DIM = 64
CONV_W = 31
POOL_WINDOWS = (2, 4, 8, 16)
POOL_HIST = 15
SG_GROUPS = 4
N_EXPERTS = 8
LANES = 128
AUG_PER_HEAD = 6
VMEM_LIMIT = 56 * 1024 * 1024


def _params(sem, vmem=VMEM_LIMIT):
    return pltpu.CompilerParams(dimension_semantics=sem, vmem_limit_bytes=vmem)


def _dot(a, b):
    return jnp.dot(a, b, preferred_element_type=F32)


class _LW(NamedTuple):
    arr: jax.Array
    idx: int


def _wspec(w):
    nd = w.arr.ndim - 1
    return pl.BlockSpec((None,) + w.arr.shape[1:], lambda *_: (w.idx,) + (0,) * nd)


def _split3(x):
    hi = x.astype(BF16)
    r1 = x - hi.astype(F32)
    mid = r1.astype(BF16)
    lo = (r1 - mid.astype(F32)).astype(BF16)
    return jnp.concatenate([hi, mid, lo], axis=1)


def _running_sum(x, tril, carry_ref):
    cs = _dot(tril, _split3(x))
    c = carry_ref[...] + (cs[:, :LANES] + cs[:, LANES:2 * LANES] + cs[:, 2 * LANES:])
    n = x.shape[0]
    carry_ref[...] = c[n - 1:n, :]
    return c


def _placement(sign_k):
    pq = np.zeros((3 * LANES, 2 * LANES), np.float32)
    pk = np.zeros((3 * LANES, 2 * LANES), np.float32)
    oq = np.zeros((1, 2 * LANES), np.float32)
    ok = np.zeros((1, 2 * LANES), np.float32)
    for h in range(N_HEADS):
        p, j = divmod(h, 2)
        for i in range(3):
            pq[i * LANES + h, p * LANES + j * AUG_PER_HEAD + i] = 1.0
            oq[0, p * LANES + j * AUG_PER_HEAD + 3 + i] = 1.0
            pk[i * LANES + h, p * LANES + j * AUG_PER_HEAD + 3 + i] = sign_k
            ok[0, p * LANES + j * AUG_PER_HEAD + i] = 1.0
    return pq, pk, oq, ok


def _in_proj_kernel(h_ref, g_ref, w_ref, bf_ref, c0_ref, tril_ref, pq_ref, pk_ref, oq_ref, ok_ref,
                    qs_ref, kf_ref, vf_ref, kb_ref, vb_ref, lf_ref, aq_ref, ak_ref, zc_ref, carry_ref,
                    kv_transposed=False):
    @pl.when(pl.program_id(1) == 0)
    def _():
        carry_ref[...] = c0_ref[0]

    x = h_ref[...]
    hn = x * lax.rsqrt(jnp.mean(x * x, axis=-1, keepdims=True) + EPS) * g_ref[...]
    hb = hn.astype(BF16)
    qs_ref[...] = (_dot(hb, w_ref[:, 0:256]) * (HEAD_DIM ** -0.5)).astype(BF16)
    k = _dot(hb, w_ref[:, 256:512])
    kf_ref[...] = k.T if kv_transposed else k
    kb_ref[...] = k.astype(BF16)
    v = _dot(hb, w_ref[:, 512:768])
    vf_ref[...] = v.T if kv_transposed else v
    vb_ref[...] = v.astype(BF16)
    zc_ref[...] = _dot(hb, w_ref[:, 768:2048])
    zf = _dot(hb, w_ref[:, 2048:2176]) + bf_ref[...]
    lf = jnp.minimum(zf, 0.0) - jnp.log1p(jnp.exp(-jnp.abs(zf)))
    lane = lax.broadcasted_iota(jnp.int32, lf.shape, 1)
    lf = jnp.where(lane < N_HEADS, lf, 0.0)
    lf_ref[...] = lf
    cp = _split3(_running_sum(lf, tril_ref[...], carry_ref))
    aq_ref[...] = (_dot(cp, pq_ref[...]) + oq_ref[...]).astype(BF16)
    ak_ref[...] = (_dot(cp, pk_ref[...]) + ok_ref[...]).astype(BF16)


N_IN_PROJ_INPUTS, N_IN_PROJ_OUTPUTS, N_MIX_INPUTS = 10, 8, 11


def _in_mix_kernel(*refs, ts, tc, pos0, want_sgv, kv_transposed):
    proj_in = refs[:N_IN_PROJ_INPUTS]
    mix_in = refs[N_IN_PROJ_INPUTS:N_IN_PROJ_INPUTS + N_MIX_INPUTS]
    rest = refs[N_IN_PROJ_INPUTS + N_MIX_INPUTS:]
    proj_out, rest = rest[:N_IN_PROJ_OUTPUTS], rest[N_IN_PROJ_OUTPUTS:]
    bcd_ref, cst_ref, pst_ref = rest[:3]
    sgv_ref = rest[3] if want_sgv else None
    carry_ref, zc_sc, xc_ref, xp_ref = rest[3 + want_sgv:]
    _in_proj_kernel(*proj_in, *proj_out, zc_sc, carry_ref, kv_transposed=kv_transposed)
    z = [zc_sc.at[:, j * GROUP_W:(j + 1) * GROUP_W] for j in range(5)]
    _mixers_kernel(*z, *mix_in, bcd_ref, cst_ref, pst_ref, sgv_ref, xc_ref, xp_ref, ts=ts, tc=tc, pos0=pos0)


def _in_mix(h, lw, c0, cprev, pprev, nb, seq, ts, tc, pos0, want_sgv, kv_transposed):
    t = nb * seq
    ns = seq // ts
    tril = jnp.asarray(np.tril(np.ones((ts, ts), np.float32)), BF16)
    pq, pk, oq, ok = _placement(-1.0)
    row = lambda b, s: (b * ns + s, 0)
    const = lambda b, s: (0, 0)
    blk = lambda w_: pl.BlockSpec((ts, w_), row)
    full = lambda a: pl.BlockSpec(a.shape, const)
    perb = lambda r: pl.BlockSpec((1, r, GROUP_W), lambda b, s: (b, 0, 0))
    consts = [jnp.asarray(pq, BF16), jnp.asarray(pk, BF16), jnp.asarray(oq), jnp.asarray(ok)]
    small = [lw[n] for n in ('dw', 'dwb', 'lng', 'lnb', 'pw', 'plw', 'psc', 'sgw', 'sgb')]
    outs = [(256, BF16), (256, F32), (256, F32), (256, BF16), (256, BF16), (LANES, F32), (256, BF16), (256, BF16),
            (768, BF16)]
    out_specs = [blk(w_) for w_, _ in outs] + [perb(32), perb(16)]
    out_shape = [jax.ShapeDtypeStruct((t, w_), dt) for w_, dt in outs] + [
        jax.ShapeDtypeStruct((nb, 32, GROUP_W), F32), jax.ShapeDtypeStruct((nb, 16, GROUP_W), F32)]
    if kv_transposed:
        for i in (1, 2):
            out_specs[i] = pl.BlockSpec((GROUP_W, ts), lambda b, s: (b, s))
            out_shape[i] = jax.ShapeDtypeStruct((nb * GROUP_W, seq), F32)
    if want_sgv:
        out_specs.append(blk(GROUP_W))
        out_shape.append(jax.ShapeDtypeStruct((t, GROUP_W), F32))
    return pl.pallas_call(
        functools.partial(_in_mix_kernel, ts=ts, tc=tc, pos0=pos0, want_sgv=want_sgv, kv_transposed=kv_transposed),
        grid=(nb, ns),
        in_specs=[blk(D_MODEL), _wspec(lw['g1']), _wspec(lw['w_in']), _wspec(lw['bf']),
                  pl.BlockSpec((1, 1, LANES), lambda b, s: (b, 0, 0)), full(tril)] + [full(a) for a in consts]
                 + [perb(32), perb(16)] + [_wspec(a) for a in small],
        out_specs=out_specs,
        out_shape=out_shape,
        scratch_shapes=[pltpu.VMEM((1, LANES), F32), pltpu.VMEM((ts, 5 * GROUP_W), F32),
                        pltpu.VMEM((40 + ts, GROUP_W), F32), pltpu.VMEM((16 + ts, GROUP_W), F32)],
        compiler_params=_params(("arbitrary", "arbitrary")),
        name="in_mix",
    )(h, lw['g1'].arr, lw['w_in'].arr, lw['bf'].arr, c0, tril, *consts, cprev, pprev, *[a.arr for a in small])


def _cache_sum_kernel(lf_ref, tril_ref, pk_ref, ok_ref, ak_ref, tot_ref, carry_ref):
    @pl.when(pl.program_id(1) == 0)
    def _():
        carry_ref[...] = jnp.zeros_like(carry_ref)

    cp = _split3(_running_sum(lf_ref[...], tril_ref[...], carry_ref))
    ak_ref[...] = (_dot(cp, pk_ref[...]) + ok_ref[...]).astype(BF16)
    tot_ref[0] = carry_ref[...]


def _cache_sum(lfp, nb, past, tm):
    ns = past // tm
    tril = jnp.asarray(np.tril(np.ones((tm, tm), np.float32)), BF16)
    _, pk, _, ok = _placement(-1.0)
    pk, ok = jnp.asarray(pk, BF16), jnp.asarray(ok)
    const = lambda b, s: (0, 0)
    return pl.pallas_call(
        _cache_sum_kernel,
        grid=(nb, ns),
        in_specs=[pl.BlockSpec((tm, LANES), lambda b, s: (b * ns + s, 0)), pl.BlockSpec(tril.shape, const),
                  pl.BlockSpec(pk.shape, const), pl.BlockSpec(ok.shape, const)],
        out_specs=[pl.BlockSpec((tm, 256), lambda b, s: (b * ns + s, 0)),
                   pl.BlockSpec((1, 1, LANES), lambda b, s: (b, 0, 0))],
        out_shape=[jax.ShapeDtypeStruct((nb * past, 256), BF16), jax.ShapeDtypeStruct((nb, 1, LANES), F32)],
        scratch_shapes=[pltpu.VMEM((1, LANES), F32)],
        compiler_params=_params(("arbitrary", "arbitrary")),
        name="cache_sum",
    )(lfp, tril, pk, ok)


NEG_BIG = -1e30


def _head_lhs(q, aq, j, lane):
    zero = jnp.zeros((), BF16)
    qm = jnp.where((lane >= HEAD_DIM * j) & (lane < HEAD_DIM * (j + 1)), q, zero)
    am = jnp.where((lane >= AUG_PER_HEAD * j) & (lane < AUG_PER_HEAD * (j + 1)), aq, zero)
    return jnp.concatenate([qm, am], axis=1)


def _scores(lhs, k, ak):
    rhs = jnp.concatenate([k, ak], axis=1)
    return lax.dot_general(lhs, rhs, (((1,), (1,)), ((), ())), preferred_element_type=F32)


def _attn_kernel(q_ref, aq_ref, k_ref, ak_ref, v_ref, o_ref, m_sc, l_sc, acc_sc, e_sc, f_sc, *, t):
    qi = pl.program_id(1)
    lane = lax.broadcasted_iota(jnp.int32, (1, LANES), 1)
    row = lax.broadcasted_iota(jnp.int32, (t, t), 0)
    col = lax.broadcasted_iota(jnp.int32, (t, t), 1)
    nct = t // LANES
    for p in range(2):
        cs = slice(p * LANES, (p + 1) * LANES)
        lhs = [_head_lhs(q_ref[:, cs], aq_ref[:, cs], j, lane) for j in range(2)]

        def scores(j, ki, masked):
            rows = pl.ds(pl.multiple_of(ki * t, t), t)
            s = _scores(lhs[j], k_ref[rows, cs], ak_ref[rows, cs])
            if masked:
                s = jnp.where(col <= row, s, -jnp.inf)
            return [s[:, c * LANES:(c + 1) * LANES] for c in range(nct)]

        def stats(ki, masked):
            for j in range(2):
                parts = scores(j, ki, masked)
                m_old = m_sc[j]
                m_new = functools.reduce(jnp.maximum, parts, m_old)
                tot = l_sc[j] * jnp.exp(m_old - m_new)
                for c, part in enumerate(parts):
                    e = jnp.exp(part - m_new)
                    e_sc[j, ki, :, c * LANES:(c + 1) * LANES] = e
                    tot = tot + e
                l_sc[j] = tot
                m_sc[j] = m_new
                f_sc[j, ki] = m_new

        def body(ki, carry):
            stats(ki, False)
            return carry

        m_sc[...] = jnp.full_like(m_sc, NEG_BIG)
        l_sc[...] = jnp.zeros_like(l_sc)
        lax.fori_loop(0, qi, body, 0)
        stats(qi, True)
        for j in range(2):
            m_lane = m_sc[j]
            m_row = jnp.max(m_lane, axis=1, keepdims=True)
            l_row = jnp.sum(l_sc[j] * jnp.exp(m_lane - m_row), axis=1, keepdims=True)
            m_sc[j] = jnp.broadcast_to(m_row, m_lane.shape)
            l_sc[j] = jnp.broadcast_to(1.0 / l_row, m_lane.shape)

        def rescale(ki, carry):
            for j in range(2):
                f_sc[j, ki] = jnp.exp(f_sc[j, ki] - m_sc[j]) * l_sc[j]
            return carry

        lax.fori_loop(0, qi + 1, rescale, 0)
        acc_sc[...] = jnp.zeros_like(acc_sc)

        def accumulate(ki, carry):
            v = v_ref[pl.ds(pl.multiple_of(ki * t, t), t), cs]
            for j in range(2):
                f = f_sc[j, ki]
                prob = jnp.concatenate([(e_sc[j, ki, :, c * LANES:(c + 1) * LANES] * f).astype(BF16)
                                        for c in range(nct)], axis=1)
                acc_sc[j] += _dot(prob, v)
            return carry

        lax.fori_loop(0, qi + 1, accumulate, 0)
        o_ref[:, cs] = jnp.where(lane < HEAD_DIM, acc_sc[0], acc_sc[1]).astype(BF16)


def _attention(qs, aq, kb, ak, vb, nb, seq, t):
    assert seq % t == 0 and t % LANES == 0
    nq = seq // t
    qblk = pl.BlockSpec((t, 256), lambda b, i: (b * nq + i, 0))
    kblk = pl.BlockSpec((seq, 256), lambda b, i: (b, 0))
    return pl.pallas_call(
        functools.partial(_attn_kernel, t=t),
        grid=(nb, nq),
        in_specs=[qblk, qblk, kblk, kblk, kblk],
        out_specs=qblk,
        out_shape=jax.ShapeDtypeStruct((nb * seq, 256), BF16),
        scratch_shapes=[pltpu.VMEM((2, t, LANES), F32)] * 3 + [pltpu.VMEM((2, nq, t, t), F32),
                                                               pltpu.VMEM((2, nq, t, LANES), F32)],
        compiler_params=_params(("arbitrary", "arbitrary")),
        name="attn",
    )(qs, aq, kb, ak, vb)


def _attn_rows_kernel(q_ref, aq_ref, k_ref, ak_ref, v_ref, o_ref, *, past):
    lq, lk = q_ref.shape[0], k_ref.shape[0]
    lane = lax.broadcasted_iota(jnp.int32, (1, LANES), 1)
    row = lax.broadcasted_iota(jnp.int32, (lq, lk), 0)
    col = lax.broadcasted_iota(jnp.int32, (lq, lk), 1)
    for p in range(2):
        cs = slice(p * LANES, (p + 1) * LANES)
        outs = []
        for j in range(2):
            s = _scores(_head_lhs(q_ref[:, cs], aq_ref[:, cs], j, lane), k_ref[:, cs], ak_ref[:, cs])
            s = jnp.where(col <= past + row, s, -jnp.inf)
            e = jnp.exp(s - jnp.max(s, axis=1, keepdims=True))
            prob = (e * (1.0 / jnp.sum(e, axis=1, keepdims=True))).astype(BF16)
            outs.append(_dot(prob, v_ref[:, cs]))
        o_ref[:, cs] = jnp.where(lane < HEAD_DIM, outs[0], outs[1]).astype(BF16)


def _attention_rows(qs, aq, kb, ak, vb, nb, lq, lk):
    qblk = pl.BlockSpec((lq, 256), lambda b: (b, 0))
    kblk = pl.BlockSpec((lk, 256), lambda b: (b, 0))
    return pl.pallas_call(
        functools.partial(_attn_rows_kernel, past=lk - lq),
        grid=(nb,),
        in_specs=[qblk, qblk, kblk, kblk, kblk],
        out_specs=qblk,
        out_shape=jax.ShapeDtypeStruct((nb * lq, 256), BF16),
        compiler_params=_params(("arbitrary",)),
        name="attn_rows",
    )(qs, aq, kb, ak, vb)


ROW_CHUNK = 64


def _gelu(x):
    return 0.5 * x * (1.0 + jnp.tanh(0.7978845608028654 * (x + 0.044715 * (x * x * x))))


def _mixers_kernel(ca_ref, cg_ref, px_ref, su_ref, sv_ref, cprev_ref, pprev_ref, dw_ref, dwb_ref, lng_ref, lnb_ref,
                   pw_ref, plw_ref, psc_ref, sgw_ref, sgb_ref,
                   bcd_ref, cst_ref, pst_ref, sgv_ref, xc_ref, xp_ref, *, ts, tc, pos0):
    s = pl.program_id(1)

    @pl.when(s == 0)
    def _():
        xc_ref[0:32, :] = cprev_ref[0].astype(BF16).astype(F32)
        xp_ref[0:16, :] = pprev_ref[0]

    @pl.when(s > 0)
    def _():
        xc_ref[0:32, :] = xc_ref[ts:ts + 32, :]
        xp_ref[0:16, :] = xp_ref[ts:ts + 16, :]

    g = cg_ref[...]
    u = ca_ref[...] * (1.0 / (1.0 + jnp.exp(-g)))
    xc_ref[32:32 + ts, :] = u.astype(BF16).astype(F32)
    xc_ref[32 + ts:40 + ts, :] = jnp.zeros((8, GROUP_W), F32)
    cst_ref[0] = u[ts - 32:ts, :]
    xp_ref[16:16 + ts, :] = px_ref[...]
    pst_ref[0] = xp_ref[ts:ts + 16, :]

    rc = min(ts, ROW_CHUNK)
    lane = lax.broadcasted_iota(jnp.int32, (rc, GROUP_W), 1)
    conv_act, pool_dlt = [], []
    for r0 in range(0, ts, rc):
        y = jnp.zeros((rc, GROUP_W), F32)
        for r in range(8):
            part = jnp.zeros((rc + 8, GROUP_W), F32)
            for a in range(r if r >= 2 else r + 8, CONV_W + 2, 8):
                part = part + xc_ref[r0 + a - r:r0 + a - r + rc + 8, :] * dw_ref[a - 2:a - 1, :]
            y = y + part[r:r + rc, :]
        y = y + dwb_ref[...]
        mu = jnp.mean(y, axis=-1, keepdims=True)
        var = jnp.mean(jnp.square(y - mu), axis=-1, keepdims=True)
        y = (y - mu) * lax.rsqrt(var + EPS) * lng_ref[...] + lnb_ref[...]
        conv_act.append((y * (1.0 / (1.0 + jnp.exp(-y)))).astype(BF16))

        x = xp_ref[16 + r0:16 + r0 + rc, :]
        pos = (pos0 + s * ts + r0 + lax.broadcasted_iota(jnp.int32, (rc, 1), 0)).astype(F32)
        acc = x
        mean = jnp.zeros((rc, GROUP_W), F32)
        wi = 0
        for i in range(1, POOL_WINDOWS[-1] + 1):
            if i > 1:
                acc = acc + xp_ref[16 - (i - 1) + r0:16 - (i - 1) + r0 + rc, :]
            if i == POOL_WINDOWS[wi]:
                cnt = jnp.minimum(pos + 1.0, float(i))
                mean = jnp.where((lane >= 64 * wi) & (lane < 64 * (wi + 1)), acc / cnt, mean)
                wi += 1
        pool_dlt.append((mean - x).astype(BF16))
    bcd_ref[:, 0:256] = _dot(jnp.concatenate(conv_act, axis=0), pw_ref[...]).astype(BF16)
    bcd_ref[:, 256:512] = (_dot(jnp.concatenate(pool_dlt, axis=0), plw_ref[...]) * psc_ref[...]).astype(BF16)

    r = lax.broadcasted_iota(jnp.int32, (tc, tc), 0)
    c = lax.broadcasted_iota(jnp.int32, (tc, tc), 1)
    lane_c = lax.broadcasted_iota(jnp.int32, (tc, GROUP_W), 1)
    wts = [jnp.where(c <= r, sgw_ref[gi], 0.0).astype(BF16) for gi in range(SG_GROUPS)]
    for ci in range(ts // tc):
        uu = _gelu(su_ref[ci * tc:(ci + 1) * tc, :])
        vv = _gelu(sv_ref[ci * tc:(ci + 1) * tc, :])
        if sgv_ref is not None:
            sgv_ref[ci * tc:(ci + 1) * tc, :] = vv
        m = sgb_ref[...]
        for gi in range(SG_GROUPS):
            vg = jnp.where((lane_c >= 64 * gi) & (lane_c < 64 * (gi + 1)), vv, 0.0).astype(BF16)
            m = m + _dot(wts[gi], vg)
        bcd_ref[ci * tc:(ci + 1) * tc, 512:768] = (uu * m).astype(BF16)


R_GATE, R_EXPERT, R_RANK = 0, 2, 4


def _out_proj_kernel(a_ref, bcd_ref, h_ref, wo_ref, g_ref, *rest, moe):
    if moe:
        router_ref, stril_ref, h1_ref, hn_ref, route_ref, cnt_ref, carry_ref = rest
    else:
        h1_ref, hn_ref = rest
    mix = _dot(a_ref[...], wo_ref[0:256, :]) + _dot(bcd_ref[...], wo_ref[256:1024, :])
    h1 = h_ref[...] + mix
    hn = h1 * lax.rsqrt(jnp.mean(h1 * h1, axis=-1, keepdims=True) + EPS) * g_ref[...]
    h1_ref[...] = h1
    hn_ref[...] = hn.astype(hn_ref.dtype)
    if moe:
        @pl.when(pl.program_id(0) == 0)
        def _():
            carry_ref[...] = jnp.zeros_like(carry_ref)

        logits = _dot(hn.astype(BF16), router_ref[...])
        lane = lax.broadcasted_iota(jnp.int32, logits.shape, 1)
        logits = jnp.where(lane < N_EXPERTS, logits, -jnp.inf)
        v1 = jnp.max(logits, axis=1, keepdims=True)
        i1 = jnp.min(jnp.where(logits == v1, lane, LANES), axis=1, keepdims=True)
        rest_l = jnp.where(lane == i1, -jnp.inf, logits)
        v2 = jnp.max(rest_l, axis=1, keepdims=True)
        i2 = jnp.min(jnp.where(rest_l == v2, lane, LANES), axis=1, keepdims=True)
        e = jnp.exp(v2 - v1)
        g1 = 1.0 / (1.0 + e)
        oh1, oh2 = lane == i1, lane == i2
        oh = jnp.where(oh1 | oh2, 1.0, 0.0)
        before = _dot(stril_ref[...], oh.astype(BF16)) + carry_ref[...]
        r1 = jnp.sum(jnp.where(oh1, before, 0.0), axis=1, keepdims=True)
        r2 = jnp.sum(jnp.where(oh2, before, 0.0), axis=1, keepdims=True)
        carry_ref[...] += jnp.sum(oh, axis=0, keepdims=True)
        cnt_ref[...] = carry_ref[...]
        rec = jnp.zeros_like(logits)
        for ln, val in ((R_GATE, g1), (R_GATE + 1, e * g1), (R_EXPERT, i1.astype(F32)), (R_EXPERT + 1, i2.astype(F32)),
                        (R_RANK, r1), (R_RANK + 1, r2)):
            rec = jnp.where(lane == ln, val, rec)
        route_ref[...] = rec


def _out_proj(a, bcd, h, wo, g2, router, tm):
    t = h.shape[0]
    moe = router is not None
    row = lambda i: (i, 0)
    const = lambda i: (0, 0)
    ins = [a, bcd, h, wo.arr, g2.arr]
    in_specs = [pl.BlockSpec((tm, 256), row), pl.BlockSpec((tm, 768), row), pl.BlockSpec((tm, D_MODEL), row),
                _wspec(wo), _wspec(g2)]
    out_specs = [pl.BlockSpec((tm, D_MODEL), row), pl.BlockSpec((tm, D_MODEL), row)]
    out_shape = [jax.ShapeDtypeStruct((t, D_MODEL), F32), jax.ShapeDtypeStruct((t, D_MODEL), F32 if moe else BF16)]
    scratch = []
    if moe:
        stril = jnp.asarray(np.tril(np.ones((tm, tm), np.float32), -1), BF16)
        ins += [router.arr, stril]
        in_specs += [_wspec(router), pl.BlockSpec(stril.shape, const)]
        out_specs += [pl.BlockSpec((tm, LANES), row), pl.BlockSpec((1, LANES), const)]
        out_shape += [jax.ShapeDtypeStruct((t, LANES), F32), jax.ShapeDtypeStruct((1, LANES), F32)]
        scratch = [pltpu.VMEM((1, LANES), F32)]
    return pl.pallas_call(
        functools.partial(_out_proj_kernel, moe=moe),
        grid=(t // tm,),
        in_specs=in_specs, out_specs=out_specs, out_shape=out_shape, scratch_shapes=scratch,
        compiler_params=_params(("arbitrary",)),
        name="out_proj",
    )(*ins)


DMA_UNROLL = 8


def _row_copy(src_ref, src_row, dst_ref, dst_row, sem):
    return pltpu.make_async_copy(src_ref.at[pl.ds(src_row, 1)], dst_ref.at[pl.ds(dst_row, 1)], sem)


def _dispatch_kernel(pos_ref, x_ref, xs_ref, sem, *, tm):
    def issue(r, carry):
        for k in range(2):
            _row_copy(x_ref, r, xs_ref, pos_ref[0, 0, k * tm + r], sem).start(priority=k)
        return carry

    def drain(r, carry):
        for k in range(2):
            _row_copy(x_ref, r, xs_ref, pos_ref[0, 0, k * tm + r], sem).wait()
        return carry

    lax.fori_loop(0, tm, issue, 0, unroll=DMA_UNROLL)
    lax.fori_loop(0, tm, drain, 0, unroll=DMA_UNROLL)


def _dispatch(hn, pos_tiles, n_rows, tm):
    t = hn.shape[0]
    return pl.pallas_call(
        functools.partial(_dispatch_kernel, tm=tm),
        grid=(t // tm,),
        in_specs=[pl.BlockSpec((1, 1, 2 * tm), lambda i: (i, 0, 0), memory_space=pltpu.SMEM),
                  pl.BlockSpec((tm, D_MODEL), lambda i: (i, 0))],
        out_specs=pl.BlockSpec(memory_space=pl.ANY),
        out_shape=jax.ShapeDtypeStruct((n_rows, D_MODEL), F32),
        scratch_shapes=[pltpu.SemaphoreType.DMA],
        compiler_params=_params(("arbitrary",)),
        name="dispatch",
    )(pos_tiles, hn)


def _ffn_kernel(wt_ref, we_ref, lo_ref, hi_ref, first_ref, x_ref, wg_ref, wu_ref, wd_ref, y_ref):
    w = pl.program_id(0)
    lo, hi = lo_ref[w], hi_ref[w]

    @pl.when(hi > lo)
    def _():
        rowi = lax.broadcasted_iota(jnp.int32, (x_ref.shape[0], 1), 0)
        x = jnp.where((rowi >= lo) & (rowi < hi), x_ref[...], jnp.zeros((), x_ref.dtype)).astype(BF16)
        gate = _dot(x, wg_ref[0])
        up = _dot(x, wu_ref[0])
        act = (gate * (1.0 / (1.0 + jnp.exp(-gate))) * up).astype(BF16)
        y = _dot(act, wd_ref[0])

        @pl.when(first_ref[w] == 1)
        def _():
            y_ref[...] = y

        @pl.when(first_ref[w] == 0)
        def _():
            y_ref[...] += y


def _ffn(xs, wg, wu, wd, tabs, tm):
    d_ff = wg.arr.shape[-1]
    xmap = lambda w, wt, we, lo, hi, fi: (wt[w], 0)
    wmap = lambda w, wt, we, lo, hi, fi: (wg.idx, we[w], 0, 0)
    grid_spec = pltpu.PrefetchScalarGridSpec(
        num_scalar_prefetch=5,
        grid=(tabs[0].shape[0],),
        in_specs=[pl.BlockSpec((tm, D_MODEL), xmap),
                  pl.BlockSpec((None, 1, D_MODEL, d_ff), wmap), pl.BlockSpec((None, 1, D_MODEL, d_ff), wmap),
                  pl.BlockSpec((None, 1, d_ff, D_MODEL), wmap)],
        out_specs=pl.BlockSpec((tm, D_MODEL), xmap),
    )
    return pl.pallas_call(
        _ffn_kernel,
        grid_spec=grid_spec,
        out_shape=jax.ShapeDtypeStruct((xs.shape[0], D_MODEL), F32),
        compiler_params=_params(("arbitrary",)),
        name="ffn",
    )(*tabs, xs, wg.arr, wu.arr, wd.arr)


def _work_items(counts, n_rows, tm):
    i32 = jnp.int32
    ends = jnp.cumsum(counts)
    starts = ends - counts
    t_first = starts // tm
    n_e = jnp.where(counts > 0, (ends - 1) // tm - t_first + 1, 0)
    w_end = jnp.cumsum(n_e)
    w_start = w_end - n_e
    n_work = n_rows // tm + counts.shape[0] - 1
    w = jnp.arange(n_work, dtype=i32)
    valid = w < w_end[-1]
    e = jnp.minimum(jnp.sum((w[:, None] >= w_end[None, :]).astype(i32), axis=1), counts.shape[0] - 1)
    tile = jnp.where(valid, t_first[e] + (w - w_start[e]), n_rows // tm - 1).astype(i32)
    lo = jnp.where(valid, jnp.maximum(starts[e] - tile * tm, 0), 0).astype(i32)
    hi = jnp.where(valid, jnp.minimum(ends[e] - tile * tm, tm), 0).astype(i32)
    e = jnp.where(valid, e, e[jnp.maximum(w_end[-1] - 1, 0)]).astype(i32)
    first = jnp.concatenate([jnp.ones((1,), i32), (tile[1:] != tile[:-1]).astype(i32)])
    return (tile, e, lo, hi, first), starts


def _ple_kernel(*refs, moe, final, tm):
    refs = list(refs)
    if moe:
        pos_ref = refs.pop(0)
    h1_ref, y_ref = refs[0], refs[1]
    k = 2
    if moe:
        route_ref = refs[k]
        k += 1
    p_ref, gw_ref, pw_ref = refs[k:k + 3]
    k += 3
    if final:
        fg_ref = refs[k]
        k += 1
    h_ref = refs[k]
    if moe:
        ysc_ref, sem = refs[-2], refs[-1]

        def issue(r, carry):
            for s in range(2):
                _row_copy(y_ref, pos_ref[0, 0, s * tm + r], ysc_ref.at[s], r, sem).start(priority=s)
            return carry

        def drain(r, carry):
            for s in range(2):
                _row_copy(y_ref, pos_ref[0, 0, s * tm + r], ysc_ref.at[s], r, sem).wait()
            return carry

        lax.fori_loop(0, tm, issue, 0, unroll=DMA_UNROLL)
        lax.fori_loop(0, tm, drain, 0, unroll=DMA_UNROLL)
        rec = route_ref[...]
        lane = lax.broadcasted_iota(jnp.int32, rec.shape, 1)
        g1 = jnp.sum(jnp.where(lane == R_GATE, rec, 0.0), axis=1, keepdims=True)
        g2 = jnp.sum(jnp.where(lane == R_GATE + 1, rec, 0.0), axis=1, keepdims=True)
        h2 = h1_ref[...] + (g1 * ysc_ref[0] + g2 * ysc_ref[1])
    else:
        h2 = h1_ref[...] + y_ref[...]
    gate = _dot(h2.astype(BF16), gw_ref[...])
    pe = _dot(p_ref[...].astype(BF16), pw_ref[...])
    h3 = h2 + (1.0 / (1.0 + jnp.exp(-gate))) * pe
    h_ref[...] = h3
    if final:
        refs[k + 1][...] = h3 * lax.rsqrt(jnp.mean(h3 * h3, axis=-1, keepdims=True) + EPS) * fg_ref[...]


def _ple(h1, ys, route, pos_tiles, p, gw, pw, fg, tm):
    t = h1.shape[0]
    moe = route is not None
    final = fg is not None
    row = lambda i: (i, 0)
    const = lambda i: (0, 0)
    ins, in_specs, scratch = [], [], []
    if moe:
        ins += [pos_tiles, h1, ys, route]
        in_specs += [pl.BlockSpec((1, 1, 2 * tm), lambda i: (i, 0, 0), memory_space=pltpu.SMEM),
                     pl.BlockSpec((tm, D_MODEL), row), pl.BlockSpec(memory_space=pl.ANY), pl.BlockSpec((tm, LANES), row)]
        scratch = [pltpu.VMEM((2, tm, D_MODEL), F32), pltpu.SemaphoreType.DMA]
    else:
        ins += [h1, ys]
        in_specs += [pl.BlockSpec((tm, D_MODEL), row), pl.BlockSpec((tm, D_MODEL), row)]
    ins += [p.arr, gw.arr, pw.arr]
    in_specs += [pl.BlockSpec((None, tm, 256), lambda i: (p.idx, i, 0)), _wspec(gw), _wspec(pw)]
    out_specs = [pl.BlockSpec((tm, D_MODEL), row)]
    out_shape = [jax.ShapeDtypeStruct((t, D_MODEL), F32)]
    if final:
        ins.append(fg)
        in_specs.append(pl.BlockSpec(fg.shape, const))
        out_specs.append(pl.BlockSpec((tm, D_MODEL), row))
        out_shape.append(jax.ShapeDtypeStruct((t, D_MODEL), F32))
    return pl.pallas_call(
        functools.partial(_ple_kernel, moe=moe, final=final, tm=tm),
        grid=(t // tm,),
        in_specs=in_specs, out_specs=out_specs, out_shape=out_shape, scratch_shapes=scratch,
        compiler_params=_params(("arbitrary",)),
        name="ple",
    )(*ins)


def _tile(n, pref):
    return pref if n % pref == 0 else n


def _layer(h, p_l, lw, nb, seq, cache, cprev, pprev, final_g):
    t = nb * seq
    tm = _tile(seq, 512)
    if cache is None:
        c0 = jnp.zeros((nb, 1, LANES), F32)
        past = 0
    else:
        ck, cv, clf = cache
        past = ck.shape[1]
        akc, c0 = _cache_sum(clf, nb, past, _tile(past, 512))
    outs = _in_mix(h, lw, c0, cprev, pprev, nb, seq, tm, min(seq, 128), past, cache is not None,
                   kv_transposed=cache is None and seq % LANES == 0)
    qs, kf, vf, kb, vb, lf, aq, ak, bcd, cst, pst = outs[:11]
    sgv = outs[11] if cache is not None else None
    if cache is None:
        a = _attention(qs, aq, kb, ak, vb, nb, seq, tm)
    else:
        lk = past + seq
        cat = lambda c_, n_: jnp.concatenate([c_.reshape(nb, past, 256), n_.reshape(nb, seq, 256)], axis=1
                                             ).reshape(nb * lk, 256)
        a = _attention_rows(qs, aq, cat(ck, kb), cat(akc, ak), cat(cv, vb), nb, seq, lk)
    tmo = _tile(t, 512)
    moe = 'router' in lw
    res = _out_proj(a, bcd, h, lw['w_o'], lw['g2'], lw.get('router'), tmo)
    h1, hn = res[0], res[1]
    tf = _tile(t, 256)
    nt = t // tf
    if moe:
        route, cnt = res[2], res[3]
        tabs, starts = _work_items(cnt[0, :N_EXPERTS].astype(jnp.int32), 2 * t, tf)
        experts = route[:, R_EXPERT:R_EXPERT + 2].astype(jnp.int32)
        pos = starts[experts] + route[:, R_RANK:R_RANK + 2].astype(jnp.int32)

        def pos_tiles(tm_):
            return pos.reshape(t // tm_, tm_, 2).transpose(0, 2, 1).reshape(t // tm_, 1, 2 * tm_)

        xs = _dispatch(hn, pos_tiles(tmo), 2 * t, tmo)
        ys = _ffn(xs, lw['wg'], lw['wu'], lw['wd'], tabs, tf)
        hres = _ple(h1, ys, route, pos_tiles(tf), p_l, lw['gw'], lw['plw_e'], final_g, tf)
    else:
        zeros = jnp.zeros((nt,), jnp.int32)
        tabs = (jnp.arange(nt, dtype=jnp.int32), zeros, zeros, zeros + tf, zeros + 1)
        ys = _ffn(hn, lw['wg'], lw['wu'], lw['wd'], tabs, tf)
        hres = _ple(h1, ys, None, None, p_l, lw['gw'], lw['plw_e'], final_g, tf)
    return hres, kf, vf, lf[:, :N_HEADS], cst[:, 2:], pst[:, 1:], sgv


def kernel(x_prompt, x_sample, p_prompt, p_sample, cache_k, cache_v, cache_logf, state_conv, state_pool, norm1_g, w_in, b_f, conv_dw_w, conv_dw_b, conv_ln_g, conv_ln_b, conv_pw_w, pool_w, pool_scale, sg_w, sg_b, w_o, norm2_g, ffn_w_gate, ffn_w_up, ffn_w_down, moe_router, moe_w_gate, moe_w_up, moe_w_down, ple_w, ple_gate_w, final_g):
    depth = w_in.shape[0]
    bp, sp, _ = x_prompt.shape
    bs, ss, _ = x_sample.shape
    past = cache_k.shape[2]
    hp = x_prompt.reshape(bp * sp, D_MODEL)
    hs = x_sample.reshape(bs * ss, D_MODEL)
    fg = final_g.reshape(1, D_MODEL)

    row = lambda x: x.reshape(depth, 1, -1)
    w_r = jnp.concatenate([w_in[..., 0:768], w_in[..., 772:2052],
                           jnp.pad(w_in[..., 768:772], ((0, 0), (0, 0), (0, LANES - N_HEADS)))], axis=-1).astype(BF16)
    plw = jnp.zeros((depth, GROUP_W, GROUP_W), F32)
    for gi in range(len(POOL_WINDOWS)):
        plw = plw.at[:, 64 * gi:64 * (gi + 1), 64 * gi:64 * (gi + 1)].set(pool_w[:, gi])
    stacked = {
        'g1': row(norm1_g), 'w_in': w_r, 'bf': row(jnp.pad(b_f, ((0, 0), (0, LANES - N_HEADS)))),
        'dw': jnp.pad(conv_dw_w, ((0, 0), (0, 1), (0, 0))), 'dwb': row(conv_dw_b),
        'lng': row(conv_ln_g), 'lnb': row(conv_ln_b), 'pw': conv_pw_w.astype(BF16), 'plw': plw.astype(BF16),
        'psc': row(pool_scale), 'w_o': w_o.astype(BF16), 'g2': row(norm2_g),
        'gw': ple_gate_w.astype(BF16), 'plw_e': ple_w.astype(BF16),
    }
    dense = {'wg': ffn_w_gate.astype(BF16)[:, None], 'wu': ffn_w_up.astype(BF16)[:, None],
             'wd': ffn_w_down.astype(BF16)[:, None]}
    sparse = {'wg': moe_w_gate.astype(BF16), 'wu': moe_w_up.astype(BF16), 'wd': moe_w_down.astype(BF16),
              'router': jnp.pad(moe_router, ((0, 0), (0, 0), (0, LANES - N_EXPERTS))).astype(BF16)}

    def sg_params(tc):
        return {'sgw': sg_w[:, :, :tc, :tc],
                'sgb': jnp.repeat(jnp.swapaxes(sg_b[:, :, :tc], 1, 2), GROUP_W // SG_GROUPS, axis=2)}

    sg_p, sg_s = sg_params(min(sp, 128)), sg_params(min(ss, 128))
    pp_all = p_prompt.reshape(depth, bp * sp, -1)
    ps_all = p_sample.reshape(depth, bs * ss, -1)
    ck_all = cache_k.reshape(depth, bs, past, 256).astype(BF16)
    cv_all = cache_v.reshape(depth, bs, past, 256).astype(BF16)
    clf_all = jnp.pad(cache_logf, ((0, 0), (0, 0), (0, 0), (0, LANES - N_HEADS))).reshape(depth, bs * past, LANES)
    sc_all = jnp.pad(state_conv, ((0, 0), (0, 0), (2, 0), (0, 0)))
    sp_all = jnp.pad(state_pool, ((0, 0), (0, 0), (1, 0), (0, 0)))
    cz, pz = jnp.zeros((bp, 32, GROUP_W), F32), jnp.zeros((bp, 16, GROUP_W), F32)

    outs_p, outs_s = [], []
    for l in range(depth):
        ffn_w = dense if l % 2 == 0 else sparse

        def layer_weights(sg):
            lw = {n: _LW(a, l) for n, a in {**stacked, **sg}.items()}
            lw.update({n: _LW(a, l // 2) for n, a in ffn_w.items()})
            return lw

        last = fg if l == depth - 1 else None
        res = _layer(hp, _LW(pp_all, l), layer_weights(sg_p), bp, sp, None, cz, pz, last)
        hp_res, outs = res[0], res[1:]
        outs_p.append(outs)
        res = _layer(hs, _LW(ps_all, l), layer_weights(sg_s), bs, ss, (ck_all[l], cv_all[l], clf_all[l]),
                     sc_all[l], sp_all[l], last)
        hs_res, outs = res[0], res[1:]
        outs_s.append(outs)
        if last is None:
            hp, hs = hp_res[0], hs_res[0]
        else:
            y_prompt, y_sample = hp_res[1], hs_res[1]

    def stack(outs, i, shape):
        return jnp.stack([o[i] for o in outs]).reshape((depth,) + shape)

    def stack_kv(i):
        if outs_p[0][i].shape == (bp * sp, GROUP_W):
            return stack(outs_p, i, (bp, sp, N_HEADS, HEAD_DIM))
        kv = stack(outs_p, i, (bp, N_HEADS, HEAD_DIM, sp))
        return jnp.transpose(kv, (0, 1, 4, 2, 3))

    return (y_prompt.reshape(bp, sp, D_MODEL), y_sample.reshape(bs, ss, D_MODEL),
            stack_kv(0), stack_kv(1),
            stack(outs_p, 2, (bp, sp, N_HEADS)), stack(outs_p, 3, (bp, CONV_W - 1, GROUP_W)),
            stack(outs_p, 4, (bp, POOL_HIST, GROUP_W)),
            stack(outs_s, 0, (bs, ss, N_HEADS, HEAD_DIM)), stack(outs_s, 1, (bs, ss, N_HEADS, HEAD_DIM)),
            stack(outs_s, 2, (bs, ss, N_HEADS)), stack(outs_s, 3, (bs, CONV_W - 1, GROUP_W)),
            stack(outs_s, 4, (bs, POOL_HIST, GROUP_W)), stack(outs_s, 5, (bs, ss, GROUP_W)))
```

```python
import functools
from typing import NamedTuple

import numpy as np
import jax
import jax.numpy as jnp
from jax import lax
from jax.experimental import pallas as pl
from jax.experimental.pallas import tpu as pltpu

F32 = jnp.float32
BF16 = jnp.bfloat16
EPS = 1e-6

D_MODEL = 1024
GROUP_W = 256
N_HEADS = 4
HEAD_DIM = 64
CONV_W = 31
POOL_WINDOWS = (2, 4, 8, 16)
POOL_HIST = 15
SG_GROUPS = 4
N_EXPERTS = 8
LANES = 128
AUG_PER_HEAD = 6
VMEM_LIMIT = 56 * 1024 * 1024


def _params(sem, vmem=VMEM_LIMIT):
    return pltpu.CompilerParams(dimension_semantics=sem, vmem_limit_bytes=vmem)


def _dot(a, b):
    return jnp.dot(a, b, preferred_element_type=F32)


class _LW(NamedTuple):
    arr: jax.Array
    idx: int


def _wspec(w):
    nd = w.arr.ndim - 1
    return pl.BlockSpec((None,) + w.arr.shape[1:], lambda *_: (w.idx,) + (0,) * nd)


def _split3(x):
    hi = x.astype(BF16)
    r1 = x - hi.astype(F32)
    mid = r1.astype(BF16)
    lo = (r1 - mid.astype(F32)).astype(BF16)
    return jnp.concatenate([hi, mid, lo], axis=1)


def _running_sum(x, tril, carry_ref):
    cs = _dot(tril, _split3(x))
    c = carry_ref[...] + (cs[:, :LANES] + cs[:, LANES:2 * LANES] + cs[:, 2 * LANES:])
    n = x.shape[0]
    carry_ref[...] = c[n - 1:n, :]
    return c


def _placement(sign_k):
    pq = np.zeros((3 * LANES, 2 * LANES), np.float32)
    pk = np.zeros((3 * LANES, 2 * LANES), np.float32)
    oq = np.zeros((1, 2 * LANES), np.float32)
    ok = np.zeros((1, 2 * LANES), np.float32)
    for h in range(N_HEADS):
        p, j = divmod(h, 2)
        for i in range(3):
            pq[i * LANES + h, p * LANES + j * AUG_PER_HEAD + i] = 1.0
            oq[0, p * LANES + j * AUG_PER_HEAD + 3 + i] = 1.0
            pk[i * LANES + h, p * LANES + j * AUG_PER_HEAD + 3 + i] = sign_k
            ok[0, p * LANES + j * AUG_PER_HEAD + i] = 1.0
    return pq, pk, oq, ok


def _in_proj_kernel(h_ref, g_ref, w_ref, bf_ref, c0_ref, tril_ref, pq_ref, pk_ref, oq_ref, ok_ref,
                    qs_ref, kf_ref, vf_ref, kb_ref, vb_ref, lf_ref, aq_ref, ak_ref, zc_ref, carry_ref,
                    kv_transposed=False):
    @pl.when(pl.program_id(1) == 0)
    def _():
        carry_ref[...] = c0_ref[0]

    x = h_ref[...]
    hn = x * lax.rsqrt(jnp.mean(x * x, axis=-1, keepdims=True) + EPS) * g_ref[...]
    hb = hn.astype(BF16)
    qs_ref[...] = (_dot(hb, w_ref[:, 0:256]) * (HEAD_DIM ** -0.5)).astype(BF16)
    k = _dot(hb, w_ref[:, 256:512])
    kf_ref[...] = k.T if kv_transposed else k
    kb_ref[...] = k.astype(BF16)
    v = _dot(hb, w_ref[:, 512:768])
    vf_ref[...] = v.T if kv_transposed else v
    vb_ref[...] = v.astype(BF16)
    zc_ref[...] = _dot(hb, w_ref[:, 768:2048])
    zf = _dot(hb, w_ref[:, 2048:2176]) + bf_ref[...]
    lf = jnp.minimum(zf, 0.0) - jnp.log1p(jnp.exp(-jnp.abs(zf)))
    lane = lax.broadcasted_iota(jnp.int32, lf.shape, 1)
    lf = jnp.where(lane < N_HEADS, lf, 0.0)
    lf_ref[...] = lf
    cp = _split3(_running_sum(lf, tril_ref[...], carry_ref))
    aq_ref[...] = (_dot(cp, pq_ref[...]) + oq_ref[...]).astype(BF16)
    ak_ref[...] = (_dot(cp, pk_ref[...]) + ok_ref[...]).astype(BF16)


N_IN_PROJ_INPUTS, N_IN_PROJ_OUTPUTS, N_MIX_INPUTS = 10, 8, 11


def _in_mix_kernel(*refs, ts, tc, pos0, want_sgv, kv_transposed):
    proj_in = refs[:N_IN_PROJ_INPUTS]
    mix_in = refs[N_IN_PROJ_INPUTS:N_IN_PROJ_INPUTS + N_MIX_INPUTS]
    rest = refs[N_IN_PROJ_INPUTS + N_MIX_INPUTS:]
    proj_out, rest = rest[:N_IN_PROJ_OUTPUTS], rest[N_IN_PROJ_OUTPUTS:]
    bcd_ref, cst_ref, pst_ref = rest[:3]
    sgv_ref = rest[3] if want_sgv else None
    carry_ref, zc_sc, xc_ref, xp_ref = rest[3 + want_sgv:]
    _in_proj_kernel(*proj_in, *proj_out, zc_sc, carry_ref, kv_transposed=kv_transposed)
    z = [zc_sc.at[:, j * GROUP_W:(j + 1) * GROUP_W] for j in range(5)]
    _mixers_kernel(*z, *mix_in, bcd_ref, cst_ref, pst_ref, sgv_ref, xc_ref, xp_ref, ts=ts, tc=tc, pos0=pos0)


def _in_mix(h, lw, c0, cprev, pprev, nb, seq, ts, tc, pos0, want_sgv, kv_transposed):
    t = nb * seq
    ns = seq // ts
    tril = jnp.asarray(np.tril(np.ones((ts, ts), np.float32)), BF16)
    pq, pk, oq, ok = _placement(-1.0)
    row = lambda b, s: (b * ns + s, 0)
    const = lambda b, s: (0, 0)
    blk = lambda w_: pl.BlockSpec((ts, w_), row)
    full = lambda a: pl.BlockSpec(a.shape, const)
    perb = lambda r: pl.BlockSpec((1, r, GROUP_W), lambda b, s: (b, 0, 0))
    consts = [jnp.asarray(pq, BF16), jnp.asarray(pk, BF16), jnp.asarray(oq), jnp.asarray(ok)]
    small = [lw[n] for n in ('dw', 'dwb', 'lng', 'lnb', 'pw', 'plw', 'psc', 'sgw', 'sgb')]
    outs = [(256, BF16), (256, F32), (256, F32), (256, BF16), (256, BF16), (LANES, F32), (256, BF16), (256, BF16),
            (768, BF16)]
    out_specs = [blk(w_) for w_, _ in outs] + [perb(32), perb(16)]
    out_shape = [jax.ShapeDtypeStruct((t, w_), dt) for w_, dt in outs] + [
        jax.ShapeDtypeStruct((nb, 32, GROUP_W), F32), jax.ShapeDtypeStruct((nb, 16, GROUP_W), F32)]
    if kv_transposed:
        for i in (1, 2):
            out_specs[i] = pl.BlockSpec((GROUP_W, ts), lambda b, s: (b, s))
            out_shape[i] = jax.ShapeDtypeStruct((nb * GROUP_W, seq), F32)
    if want_sgv:
        out_specs.append(blk(GROUP_W))
        out_shape.append(jax.ShapeDtypeStruct((t, GROUP_W), F32))
    return pl.pallas_call(
        functools.partial(_in_mix_kernel, ts=ts, tc=tc, pos0=pos0, want_sgv=want_sgv, kv_transposed=kv_transposed),
        grid=(nb, ns),
        in_specs=[blk(D_MODEL), _wspec(lw['g1']), _wspec(lw['w_in']), _wspec(lw['bf']),
                  pl.BlockSpec((1, 1, LANES), lambda b, s: (b, 0, 0)), full(tril)] + [full(a) for a in consts]
                 + [perb(32), perb(16)] + [_wspec(a) for a in small],
        out_specs=out_specs,
        out_shape=out_shape,
        scratch_shapes=[pltpu.VMEM((1, LANES), F32), pltpu.VMEM((ts, 5 * GROUP_W), F32),
                        pltpu.VMEM((40 + ts, GROUP_W), F32), pltpu.VMEM((16 + ts, GROUP_W), F32)],
        compiler_params=_params(("arbitrary", "arbitrary")),
        name="in_mix",
    )(h, lw['g1'].arr, lw['w_in'].arr, lw['bf'].arr, c0, tril, *consts, cprev, pprev, *[a.arr for a in small])


def _cache_sum_kernel(lf_ref, tril_ref, pk_ref, ok_ref, ak_ref, tot_ref, carry_ref):
    @pl.when(pl.program_id(1) == 0)
    def _():
        carry_ref[...] = jnp.zeros_like(carry_ref)

    cp = _split3(_running_sum(lf_ref[...], tril_ref[...], carry_ref))
    ak_ref[...] = (_dot(cp, pk_ref[...]) + ok_ref[...]).astype(BF16)
    tot_ref[0] = carry_ref[...]


def _cache_sum(lfp, nb, past, tm):
    ns = past // tm
    tril = jnp.asarray(np.tril(np.ones((tm, tm), np.float32)), BF16)
    _, pk, _, ok = _placement(-1.0)
    pk, ok = jnp.asarray(pk, BF16), jnp.asarray(ok)
    const = lambda b, s: (0, 0)
    return pl.pallas_call(
        _cache_sum_kernel,
        grid=(nb, ns),
        in_specs=[pl.BlockSpec((tm, LANES), lambda b, s: (b * ns + s, 0)), pl.BlockSpec(tril.shape, const),
                  pl.BlockSpec(pk.shape, const), pl.BlockSpec(ok.shape, const)],
        out_specs=[pl.BlockSpec((tm, 256), lambda b, s: (b * ns + s, 0)),
                   pl.BlockSpec((1, 1, LANES), lambda b, s: (b, 0, 0))],
        out_shape=[jax.ShapeDtypeStruct((nb * past, 256), BF16), jax.ShapeDtypeStruct((nb, 1, LANES), F32)],
        scratch_shapes=[pltpu.VMEM((1, LANES), F32)],
        compiler_params=_params(("arbitrary", "arbitrary")),
        name="cache_sum",
    )(lfp, tril, pk, ok)


NEG_BIG = -1e30


def _head_lhs(q, aq, j, lane):
    zero = jnp.zeros((), BF16)
    qm = jnp.where((lane >= HEAD_DIM * j) & (lane < HEAD_DIM * (j + 1)), q, zero)
    am = jnp.where((lane >= AUG_PER_HEAD * j) & (lane < AUG_PER_HEAD * (j + 1)), aq, zero)
    return jnp.concatenate([qm, am], axis=1)


def _scores(lhs, k, ak):
    rhs = jnp.concatenate([k, ak], axis=1)
    return lax.dot_general(lhs, rhs, (((1,), (1,)), ((), ())), preferred_element_type=F32)


def _attn_kernel(q_ref, aq_ref, k_ref, ak_ref, v_ref, o_ref, m_sc, l_sc, acc_sc, e_sc, f_sc, *, t):
    qi = pl.program_id(1)
    lane = lax.broadcasted_iota(jnp.int32, (1, LANES), 1)
    row = lax.broadcasted_iota(jnp.int32, (t, t), 0)
    col = lax.broadcasted_iota(jnp.int32, (t, t), 1)
    nct = t // LANES
    for p in range(2):
        cs = slice(p * LANES, (p + 1) * LANES)
        lhs = [_head_lhs(q_ref[:, cs], aq_ref[:, cs], j, lane) for j in range(2)]

        def scores(j, ki, masked):
            rows = pl.ds(pl.multiple_of(ki * t, t), t)
            s = _scores(lhs[j], k_ref[rows, cs], ak_ref[rows, cs])
            if masked:
                s = jnp.where(col <= row, s, -jnp.inf)
            return [s[:, c * LANES:(c + 1) * LANES] for c in range(nct)]

        def stats(ki, masked):
            for j in range(2):
                parts = scores(j, ki, masked)
                m_old = m_sc[j]
                m_new = functools.reduce(jnp.maximum, parts, m_old)
                tot = l_sc[j] * jnp.exp(m_old - m_new)
                for c, part in enumerate(parts):
                    e = jnp.exp(part - m_new)
                    e_sc[j, ki, :, c * LANES:(c + 1) * LANES] = e
                    tot = tot + e
                l_sc[j] = tot
                m_sc[j] = m_new
                f_sc[j, ki] = m_new

        def body(ki, carry):
            stats(ki, False)
            return carry

        m_sc[...] = jnp.full_like(m_sc, NEG_BIG)
        l_sc[...] = jnp.zeros_like(l_sc)
        lax.fori_loop(0, qi, body, 0)
        stats(qi, True)
        for j in range(2):
            m_lane = m_sc[j]
            m_row = jnp.max(m_lane, axis=1, keepdims=True)
            l_row = jnp.sum(l_sc[j] * jnp.exp(m_lane - m_row), axis=1, keepdims=True)
            m_sc[j] = jnp.broadcast_to(m_row, m_lane.shape)
            l_sc[j] = jnp.broadcast_to(1.0 / l_row, m_lane.shape)

        def rescale(ki, carry):
            for j in range(2):
                f_sc[j, ki] = jnp.exp(f_sc[j, ki] - m_sc[j]) * l_sc[j]
            return carry

        lax.fori_loop(0, qi + 1, rescale, 0)
        acc_sc[...] = jnp.zeros_like(acc_sc)

        def accumulate(ki, carry):
            v = v_ref[pl.ds(pl.multiple_of(ki * t, t), t), cs]
            for j in range(2):
                f = f_sc[j, ki]
                prob = jnp.concatenate([(e_sc[j, ki, :, c * LANES:(c + 1) * LANES] * f).astype(BF16)
                                        for c in range(nct)], axis=1)
                acc_sc[j] += _dot(prob, v)
            return carry

        lax.fori_loop(0, qi + 1, accumulate, 0)
        o_ref[:, cs] = jnp.where(lane < HEAD_DIM, acc_sc[0], acc_sc[1]).astype(BF16)


def _attention(qs, aq, kb, ak, vb, nb, seq, t):
    assert seq % t == 0 and t % LANES == 0
    nq = seq // t
    qblk = pl.BlockSpec((t, 256), lambda b, i: (b * nq + i, 0))
    kblk = pl.BlockSpec((seq, 256), lambda b, i: (b, 0))
    return pl.pallas_call(
        functools.partial(_attn_kernel, t=t),
        grid=(nb, nq),
        in_specs=[qblk, qblk, kblk, kblk, kblk],
        out_specs=qblk,
        out_shape=jax.ShapeDtypeStruct((nb * seq, 256), BF16),
        scratch_shapes=[pltpu.VMEM((2, t, LANES), F32)] * 3 + [pltpu.VMEM((2, nq, t, t), F32),
                                                               pltpu.VMEM((2, nq, t, LANES), F32)],
        compiler_params=_params(("arbitrary", "arbitrary")),
        name="attn",
    )(qs, aq, kb, ak, vb)


def _attn_rows_kernel(q_ref, aq_ref, k_ref, ak_ref, v_ref, o_ref, *, past):
    lq, lk = q_ref.shape[0], k_ref.shape[0]
    lane = lax.broadcasted_iota(jnp.int32, (1, LANES), 1)
    row = lax.broadcasted_iota(jnp.int32, (lq, lk), 0)
    col = lax.broadcasted_iota(jnp.int32, (lq, lk), 1)
    for p in range(2):
        cs = slice(p * LANES, (p + 1) * LANES)
        outs = []
        for j in range(2):
            s = _scores(_head_lhs(q_ref[:, cs], aq_ref[:, cs], j, lane), k_ref[:, cs], ak_ref[:, cs])
            s = jnp.where(col <= past + row, s, -jnp.inf)
            e = jnp.exp(s - jnp.max(s, axis=1, keepdims=True))
            prob = (e * (1.0 / jnp.sum(e, axis=1, keepdims=True))).astype(BF16)
            outs.append(_dot(prob, v_ref[:, cs]))
        o_ref[:, cs] = jnp.where(lane < HEAD_DIM, outs[0], outs[1]).astype(BF16)


def _attention_rows(qs, aq, kb, ak, vb, nb, lq, lk):
    qblk = pl.BlockSpec((lq, 256), lambda b: (b, 0))
    kblk = pl.BlockSpec((lk, 256), lambda b: (b, 0))
    return pl.pallas_call(
        functools.partial(_attn_rows_kernel, past=lk - lq),
        grid=(nb,),
        in_specs=[qblk, qblk, kblk, kblk, kblk],
        out_specs=qblk,
        out_shape=jax.ShapeDtypeStruct((nb * lq, 256), BF16),
        compiler_params=_params(("arbitrary",)),
        name="attn_rows",
    )(qs, aq, kb, ak, vb)


ROW_CHUNK = 64


def _gelu(x):
    return 0.5 * x * (1.0 + jnp.tanh(0.7978845608028654 * (x + 0.044715 * (x * x * x))))


def _mixers_kernel(ca_ref, cg_ref, px_ref, su_ref, sv_ref, cprev_ref, pprev_ref, dw_ref, dwb_ref, lng_ref, lnb_ref,
                   pw_ref, plw_ref, psc_ref, sgw_ref, sgb_ref,
                   bcd_ref, cst_ref, pst_ref, sgv_ref, xc_ref, xp_ref, *, ts, tc, pos0):
    s = pl.program_id(1)

    @pl.when(s == 0)
    def _():
        xc_ref[0:32, :] = cprev_ref[0].astype(BF16).astype(F32)
        xp_ref[0:16, :] = pprev_ref[0]

    @pl.when(s > 0)
    def _():
        xc_ref[0:32, :] = xc_ref[ts:ts + 32, :]
        xp_ref[0:16, :] = xp_ref[ts:ts + 16, :]

    g = cg_ref[...]
    u = ca_ref[...] * (1.0 / (1.0 + jnp.exp(-g)))
    xc_ref[32:32 + ts, :] = u.astype(BF16).astype(F32)
    xc_ref[32 + ts:40 + ts, :] = jnp.zeros((8, GROUP_W), F32)
    cst_ref[0] = u[ts - 32:ts, :]
    xp_ref[16:16 + ts, :] = px_ref[...]
    pst_ref[0] = xp_ref[ts:ts + 16, :]

    rc = min(ts, ROW_CHUNK)
    lane = lax.broadcasted_iota(jnp.int32, (rc, GROUP_W), 1)
    conv_act, pool_dlt = [], []
    for r0 in range(0, ts, rc):
        y = jnp.zeros((rc, GROUP_W), F32)
        for r in range(8):
            part = jnp.zeros((rc + 8, GROUP_W), F32)
            for a in range(r if r >= 2 else r + 8, CONV_W + 2, 8):
                part = part + xc_ref[r0 + a - r:r0 + a - r + rc + 8, :] * dw_ref[a - 2:a - 1, :]
            y = y + part[r:r + rc, :]
        y = y + dwb_ref[...]
        mu = jnp.mean(y, axis=-1, keepdims=True)
        var = jnp.mean(jnp.square(y - mu), axis=-1, keepdims=True)
        y = (y - mu) * lax.rsqrt(var + EPS) * lng_ref[...] + lnb_ref[...]
        conv_act.append((y * (1.0 / (1.0 + jnp.exp(-y)))).astype(BF16))

        x = xp_ref[16 + r0:16 + r0 + rc, :]
        pos = (pos0 + s * ts + r0 + lax.broadcasted_iota(jnp.int32, (rc, 1), 0)).astype(F32)
        acc = x
        mean = jnp.zeros((rc, GROUP_W), F32)
        wi = 0
        for i in range(1, POOL_WINDOWS[-1] + 1):
            if i > 1:
                acc = acc + xp_ref[16 - (i - 1) + r0:16 - (i - 1) + r0 + rc, :]
            if i == POOL_WINDOWS[wi]:
                cnt = jnp.minimum(pos + 1.0, float(i))
                mean = jnp.where((lane >= 64 * wi) & (lane < 64 * (wi + 1)), acc / cnt, mean)
                wi += 1
        pool_dlt.append((mean - x).astype(BF16))
    bcd_ref[:, 0:256] = _dot(jnp.concatenate(conv_act, axis=0), pw_ref[...]).astype(BF16)
    bcd_ref[:, 256:512] = (_dot(jnp.concatenate(pool_dlt, axis=0), plw_ref[...]) * psc_ref[...]).astype(BF16)

    r = lax.broadcasted_iota(jnp.int32, (tc, tc), 0)
    c = lax.broadcasted_iota(jnp.int32, (tc, tc), 1)
    lane_c = lax.broadcasted_iota(jnp.int32, (tc, GROUP_W), 1)
    wts = [jnp.where(c <= r, sgw_ref[gi], 0.0).astype(BF16) for gi in range(SG_GROUPS)]
    for ci in range(ts // tc):
        uu = _gelu(su_ref[ci * tc:(ci + 1) * tc, :])
        vv = _gelu(sv_ref[ci * tc:(ci + 1) * tc, :])
        if sgv_ref is not None:
            sgv_ref[ci * tc:(ci + 1) * tc, :] = vv
        m = sgb_ref[...]
        for gi in range(SG_GROUPS):
            vg = jnp.where((lane_c >= 64 * gi) & (lane_c < 64 * (gi + 1)), vv, 0.0).astype(BF16)
            m = m + _dot(wts[gi], vg)
        bcd_ref[ci * tc:(ci + 1) * tc, 512:768] = (uu * m).astype(BF16)


R_GATE, R_EXPERT, R_RANK = 0, 2, 4


def _out_proj_kernel(a_ref, bcd_ref, h_ref, wo_ref, g_ref, *rest, moe):
    if moe:
        router_ref, stril_ref, h1_ref, hn_ref, route_ref, cnt_ref, carry_ref = rest
    else:
        h1_ref, hn_ref = rest
    mix = _dot(a_ref[...], wo_ref[0:256, :]) + _dot(bcd_ref[...], wo_ref[256:1024, :])
    h1 = h_ref[...] + mix
    hn = h1 * lax.rsqrt(jnp.mean(h1 * h1, axis=-1, keepdims=True) + EPS) * g_ref[...]
    h1_ref[...] = h1
    hn_ref[...] = hn.astype(hn_ref.dtype)
    if moe:
        @pl.when(pl.program_id(0) == 0)
        def _():
            carry_ref[...] = jnp.zeros_like(carry_ref)

        logits = _dot(hn.astype(BF16), router_ref[...])
        lane = lax.broadcasted_iota(jnp.int32, logits.shape, 1)
        logits = jnp.where(lane < N_EXPERTS, logits, -jnp.inf)
        v1 = jnp.max(logits, axis=1, keepdims=True)
        i1 = jnp.min(jnp.where(logits == v1, lane, LANES), axis=1, keepdims=True)
        rest_l = jnp.where(lane == i1, -jnp.inf, logits)
        v2 = jnp.max(rest_l, axis=1, keepdims=True)
        i2 = jnp.min(jnp.where(rest_l == v2, lane, LANES), axis=1, keepdims=True)
        e = jnp.exp(v2 - v1)
        g1 = 1.0 / (1.0 + e)
        oh1, oh2 = lane == i1, lane == i2
        oh = jnp.where(oh1 | oh2, 1.0, 0.0)
        before = _dot(stril_ref[...], oh.astype(BF16)) + carry_ref[...]
        r1 = jnp.sum(jnp.where(oh1, before, 0.0), axis=1, keepdims=True)
        r2 = jnp.sum(jnp.where(oh2, before, 0.0), axis=1, keepdims=True)
        carry_ref[...] += jnp.sum(oh, axis=0, keepdims=True)
        cnt_ref[...] = carry_ref[...]
        rec = jnp.zeros_like(logits)
        for ln, val in ((R_GATE, g1), (R_GATE + 1, e * g1), (R_EXPERT, i1.astype(F32)), (R_EXPERT + 1, i2.astype(F32)),
                        (R_RANK, r1), (R_RANK + 1, r2)):
            rec = jnp.where(lane == ln, val, rec)
        route_ref[...] = rec


def _out_proj(a, bcd, h, wo, g2, router, tm):
    t = h.shape[0]
    moe = router is not None
    row = lambda i: (i, 0)
    const = lambda i: (0, 0)
    ins = [a, bcd, h, wo.arr, g2.arr]
    in_specs = [pl.BlockSpec((tm, 256), row), pl.BlockSpec((tm, 768), row), pl.BlockSpec((tm, D_MODEL), row),
                _wspec(wo), _wspec(g2)]
    out_specs = [pl.BlockSpec((tm, D_MODEL), row), pl.BlockSpec((tm, D_MODEL), row)]
    out_shape = [jax.ShapeDtypeStruct((t, D_MODEL), F32), jax.ShapeDtypeStruct((t, D_MODEL), F32 if moe else BF16)]
    scratch = []
    if moe:
        stril = jnp.asarray(np.tril(np.ones((tm, tm), np.float32), -1), BF16)
        ins += [router.arr, stril]
        in_specs += [_wspec(router), pl.BlockSpec(stril.shape, const)]
        out_specs += [pl.BlockSpec((tm, LANES), row), pl.BlockSpec((1, LANES), const)]
        out_shape += [jax.ShapeDtypeStruct((t, LANES), F32), jax.ShapeDtypeStruct((1, LANES), F32)]
        scratch = [pltpu.VMEM((1, LANES), F32)]
    return pl.pallas_call(
        functools.partial(_out_proj_kernel, moe=moe),
        grid=(t // tm,),
        in_specs=in_specs, out_specs=out_specs, out_shape=out_shape, scratch_shapes=scratch,
        compiler_params=_params(("arbitrary",)),
        name="out_proj",
    )(*ins)


DMA_UNROLL = 8


def _row_copy(src_ref, src_row, dst_ref, dst_row, sem):
    return pltpu.make_async_copy(src_ref.at[pl.ds(src_row, 1)], dst_ref.at[pl.ds(dst_row, 1)], sem)


def _dispatch_kernel(pos_ref, x_ref, *rest, tm):
    xs_ref, sem = rest[-2:]

    def issue(r, carry):
        for k in range(2):
            _row_copy(x_ref, r, xs_ref, pos_ref[0, 0, k * tm + r], sem).start(priority=k)
        return carry

    def drain(r, carry):
        for k in range(2):
            _row_copy(x_ref, r, xs_ref, pos_ref[0, 0, k * tm + r], sem).wait()
        return carry

    lax.fori_loop(0, tm, issue, 0, unroll=DMA_UNROLL)
    lax.fori_loop(0, tm, drain, 0, unroll=DMA_UNROLL)


def _dispatch(hn, pos_tiles, n_rows, tm, xs_prev=None):
    t = hn.shape[0]
    ins = [pos_tiles, hn]
    in_specs = [pl.BlockSpec((1, 1, 2 * tm), lambda i: (i, 0, 0), memory_space=pltpu.SMEM),
                pl.BlockSpec((tm, D_MODEL), lambda i: (i, 0))]
    aliases = {}
    if xs_prev is not None:
        ins.append(xs_prev)
        in_specs.append(pl.BlockSpec(memory_space=pl.ANY))
        aliases = {2: 0}
    return pl.pallas_call(
        functools.partial(_dispatch_kernel, tm=tm),
        grid=(t // tm,),
        in_specs=in_specs,
        out_specs=pl.BlockSpec(memory_space=pl.ANY),
        out_shape=jax.ShapeDtypeStruct((n_rows, D_MODEL), F32),
        scratch_shapes=[pltpu.SemaphoreType.DMA],
        input_output_aliases=aliases,
        compiler_params=_params(("arbitrary",)),
        name="dispatch",
    )(*ins)


def _ffn_kernel(wt_ref, we_ref, lo_ref, hi_ref, first_ref, x_ref, wg_ref, wu_ref, wd_ref, y_ref):
    w = pl.program_id(0)
    lo, hi = lo_ref[w], hi_ref[w]

    @pl.when(hi > lo)
    def _():
        rowi = lax.broadcasted_iota(jnp.int32, (x_ref.shape[0], 1), 0)
        x = jnp.where((rowi >= lo) & (rowi < hi), x_ref[...], jnp.zeros((), x_ref.dtype)).astype(BF16)
        gate = _dot(x, wg_ref[0])
        up = _dot(x, wu_ref[0])
        act = (gate * (1.0 / (1.0 + jnp.exp(-gate))) * up).astype(BF16)
        y = _dot(act, wd_ref[0])

        @pl.when(first_ref[w] == 1)
        def _():
            y_ref[...] = y

        @pl.when(first_ref[w] == 0)
        def _():
            y_ref[...] += y


def _ffn(xs, wg, wu, wd, tabs, tm):
    d_ff = wg.arr.shape[-1]
    xmap = lambda w, wt, we, lo, hi, fi: (wt[w], 0)
    wmap = lambda w, wt, we, lo, hi, fi: (wg.idx, we[w], 0, 0)
    grid_spec = pltpu.PrefetchScalarGridSpec(
        num_scalar_prefetch=5,
        grid=(tabs[0].shape[0],),
        in_specs=[pl.BlockSpec((tm, D_MODEL), xmap),
                  pl.BlockSpec((None, 1, D_MODEL, d_ff), wmap), pl.BlockSpec((None, 1, D_MODEL, d_ff), wmap),
                  pl.BlockSpec((None, 1, d_ff, D_MODEL), wmap)],
        out_specs=pl.BlockSpec((tm, D_MODEL), xmap),
    )
    return pl.pallas_call(
        _ffn_kernel,
        grid_spec=grid_spec,
        out_shape=jax.ShapeDtypeStruct((xs.shape[0], D_MODEL), F32),
        compiler_params=_params(("arbitrary",)),
        name="ffn",
    )(*tabs, xs, wg.arr, wu.arr, wd.arr)


def _work_items(counts, n_rows, tm):
    i32 = jnp.int32
    ends = jnp.cumsum(counts)
    starts = ends - counts
    t_first = starts // tm
    n_e = jnp.where(counts > 0, (ends - 1) // tm - t_first + 1, 0)
    w_end = jnp.cumsum(n_e)
    w_start = w_end - n_e
    n_work = n_rows // tm + counts.shape[0] - 1
    w = jnp.arange(n_work, dtype=i32)
    valid = w < w_end[-1]
    e = jnp.minimum(jnp.sum((w[:, None] >= w_end[None, :]).astype(i32), axis=1), counts.shape[0] - 1)
    tile = jnp.where(valid, t_first[e] + (w - w_start[e]), n_rows // tm - 1).astype(i32)
    lo = jnp.where(valid, jnp.maximum(starts[e] - tile * tm, 0), 0).astype(i32)
    hi = jnp.where(valid, jnp.minimum(ends[e] - tile * tm, tm), 0).astype(i32)
    e = jnp.where(valid, e, e[jnp.maximum(w_end[-1] - 1, 0)]).astype(i32)
    first = jnp.concatenate([jnp.ones((1,), i32), (tile[1:] != tile[:-1]).astype(i32)])
    return (tile, e, lo, hi, first), starts


def _ple_kernel(*refs, moe, final, tm):
    refs = list(refs)
    if moe:
        pos_ref, pos_next_ref = refs.pop(0), refs.pop(0)
    h1_ref, y_ref = refs[0], refs[1]
    k = 2
    if moe:
        route_ref = refs[k]
        k += 1
    p_ref, gw_ref, pw_ref = refs[k:k + 3]
    k += 3
    if final:
        fg_ref = refs[k]
        k += 1
    h_ref = refs[k]
    if moe:
        ysc_ref, sems = refs[-2], refs[-1]
        i = pl.program_id(0)
        slot = i % 2

        def gather(pos, slot_, act):
            def body(r, carry):
                for s in range(2):
                    cp = _row_copy(y_ref, pos[0, 0, s * tm + r], ysc_ref.at[slot_, s], r, sems.at[slot_])
                    act(cp, s)
                return carry
            lax.fori_loop(0, tm, body, 0, unroll=DMA_UNROLL)

        start = lambda cp, s: cp.start(priority=s)
        wait = lambda cp, s: cp.wait()

        @pl.when(i == 0)
        def _():
            gather(pos_ref, 0, start)

        @pl.when(i + 1 < pl.num_programs(0))
        def _():
            gather(pos_next_ref, 1 - slot, start)

        gather(pos_ref, slot, wait)
        rec = route_ref[...]
        lane = lax.broadcasted_iota(jnp.int32, rec.shape, 1)
        g1 = jnp.sum(jnp.where(lane == R_GATE, rec, 0.0), axis=1, keepdims=True)
        g2 = jnp.sum(jnp.where(lane == R_GATE + 1, rec, 0.0), axis=1, keepdims=True)
        h2 = h1_ref[...] + (g1 * ysc_ref[slot, 0] + g2 * ysc_ref[slot, 1])
    else:
        h2 = h1_ref[...] + y_ref[...]
    gate = _dot(h2.astype(BF16), gw_ref[...])
    pe = _dot(p_ref[...].astype(BF16), pw_ref[...])
    h3 = h2 + (1.0 / (1.0 + jnp.exp(-gate))) * pe
    h_ref[...] = h3
    if final:
        refs[k + 1][...] = h3 * lax.rsqrt(jnp.mean(h3 * h3, axis=-1, keepdims=True) + EPS) * fg_ref[...]


def _ple(h1, ys, route, pos_tiles, p, gw, pw, fg, tm):
    t = h1.shape[0]
    moe = route is not None
    final = fg is not None
    row = lambda i: (i, 0)
    const = lambda i: (0, 0)
    ins, in_specs, scratch = [], [], []
    if moe:
        last_tile = t // tm - 1
        ins += [pos_tiles, pos_tiles, h1, ys, route]
        in_specs += [pl.BlockSpec((1, 1, 2 * tm), lambda i: (i, 0, 0), memory_space=pltpu.SMEM),
                     pl.BlockSpec((1, 1, 2 * tm), lambda i: (jnp.minimum(i + 1, last_tile), 0, 0),
                                  memory_space=pltpu.SMEM),
                     pl.BlockSpec((tm, D_MODEL), row), pl.BlockSpec(memory_space=pl.ANY), pl.BlockSpec((tm, LANES), row)]
        scratch = [pltpu.VMEM((2, 2, tm, D_MODEL), F32), pltpu.SemaphoreType.DMA((2,))]
    else:
        ins += [h1, ys]
        in_specs += [pl.BlockSpec((tm, D_MODEL), row), pl.BlockSpec((tm, D_MODEL), row)]
    ins += [p.arr, gw.arr, pw.arr]
    in_specs += [pl.BlockSpec((None, tm, 256), lambda i: (p.idx, i, 0)), _wspec(gw), _wspec(pw)]
    out_specs = [pl.BlockSpec((tm, D_MODEL), row)]
    out_shape = [jax.ShapeDtypeStruct((t, D_MODEL), F32)]
    if final:
        ins.append(fg)
        in_specs.append(pl.BlockSpec(fg.shape, const))
        out_specs.append(pl.BlockSpec((tm, D_MODEL), row))
        out_shape.append(jax.ShapeDtypeStruct((t, D_MODEL), F32))
    return pl.pallas_call(
        functools.partial(_ple_kernel, moe=moe, final=final, tm=tm),
        grid=(t // tm,),
        in_specs=in_specs, out_specs=out_specs, out_shape=out_shape, scratch_shapes=scratch,
        compiler_params=_params(("arbitrary",)),
        name="ple",
    )(*ins)


def _tile(n, pref):
    return pref if n % pref == 0 else n


def _front(h, lw, nb, seq, cache, cprev, pprev):
    t = nb * seq
    tm = _tile(seq, 512)
    if cache is None:
        c0 = jnp.zeros((nb, 1, LANES), F32)
        past = 0
    else:
        ck, cv, clf = cache
        past = ck.shape[1]
        akc, c0 = _cache_sum(clf, nb, past, _tile(past, 512))
    outs = _in_mix(h, lw, c0, cprev, pprev, nb, seq, tm, min(seq, 128), past, cache is not None,
                   kv_transposed=cache is None and seq % LANES == 0)
    qs, kf, vf, kb, vb, lf, aq, ak, bcd, cst, pst = outs[:11]
    sgv = outs[11] if cache is not None else None
    if cache is None:
        a = _attention(qs, aq, kb, ak, vb, nb, seq, tm)
    else:
        lk = past + seq
        cat = lambda c_, n_: jnp.concatenate([c_.reshape(nb, past, 256), n_.reshape(nb, seq, 256)], axis=1
                                             ).reshape(nb * lk, 256)
        a = _attention_rows(qs, aq, cat(ck, kb), cat(akc, ak), cat(cv, vb), nb, seq, lk)
    res = _out_proj(a, bcd, h, lw['w_o'], lw['g2'], lw.get('router'), _tile(t, 512))
    return res, (kf, vf, lf[:, :N_HEADS], cst[:, 2:], pst[:, 1:], sgv)


def _pos_tiles(pos, tm):
    t = pos.shape[0]
    return pos.reshape(t // tm, tm, 2).transpose(0, 2, 1).reshape(t // tm, 1, 2 * tm)


def _back_dense(res, p_l, lw, final_g):
    h1, hn = res
    t = h1.shape[0]
    tf = _tile(t, 256)
    nt = t // tf
    zeros = jnp.zeros((nt,), jnp.int32)
    tabs = (jnp.arange(nt, dtype=jnp.int32), zeros, zeros, zeros + tf, zeros + 1)
    ys = _ffn(hn, lw['wg'], lw['wu'], lw['wd'], tabs, tf)
    return _ple(h1, ys, None, None, p_l, lw['gw'], lw['plw_e'], final_g, tf)


def _back_moe(groups, lw, final_g):
    i32 = jnp.int32
    counts = [res[3][0, :N_EXPERTS].astype(i32) for res, _ in groups]
    n_rows = 2 * sum(res[0].shape[0] for res, _ in groups)
    tf = max(c for c in (256, 128, 64, 32, 16) if n_rows % c == 0)
    tabs, starts = _work_items(sum(counts), n_rows, tf)
    xs, pos_all, offset = None, [], jnp.zeros((N_EXPERTS,), i32)
    for (res, _), cnt in zip(groups, counts):
        h1, hn, route = res[:3]
        experts = route[:, R_EXPERT:R_EXPERT + 2].astype(i32)
        pos = (starts + offset)[experts] + route[:, R_RANK:R_RANK + 2].astype(i32)
        offset = offset + cnt
        tmo = _tile(h1.shape[0], 512)
        xs = _dispatch(hn, _pos_tiles(pos, tmo), n_rows, tmo, xs)
        pos_all.append(pos)
    ys = _ffn(xs, lw['wg'], lw['wu'], lw['wd'], tabs, tf)
    outs = []
    for (res, p_l), pos in zip(groups, pos_all):
        tp = _tile(res[0].shape[0], 256)
        outs.append(_ple(res[0], ys, res[2], _pos_tiles(pos, tp), p_l, lw['gw'], lw['plw_e'], final_g, tp))
    return outs


def kernel(x_prompt, x_sample, p_prompt, p_sample, cache_k, cache_v, cache_logf, state_conv, state_pool, norm1_g, w_in, b_f, conv_dw_w, conv_dw_b, conv_ln_g, conv_ln_b, conv_pw_w, pool_w, pool_scale, sg_w, sg_b, w_o, norm2_g, ffn_w_gate, ffn_w_up, ffn_w_down, moe_router, moe_w_gate, moe_w_up, moe_w_down, ple_w, ple_gate_w, final_g):
    depth = w_in.shape[0]
    bp, sp, _ = x_prompt.shape
    bs, ss, _ = x_sample.shape
    past = cache_k.shape[2]
    hp = x_prompt.reshape(bp * sp, D_MODEL)
    hs = x_sample.reshape(bs * ss, D_MODEL)
    fg = final_g.reshape(1, D_MODEL)

    row = lambda x: x.reshape(depth, 1, -1)
    w_r = jnp.concatenate([w_in[..., 0:768], w_in[..., 772:2052],
                           jnp.pad(w_in[..., 768:772], ((0, 0), (0, 0), (0, LANES - N_HEADS)))], axis=-1).astype(BF16)
    plw = jnp.zeros((depth, GROUP_W, GROUP_W), F32)
    for gi in range(len(POOL_WINDOWS)):
        plw = plw.at[:, 64 * gi:64 * (gi + 1), 64 * gi:64 * (gi + 1)].set(pool_w[:, gi])
    stacked = {
        'g1': row(norm1_g), 'w_in': w_r, 'bf': row(jnp.pad(b_f, ((0, 0), (0, LANES - N_HEADS)))),
        'dw': jnp.pad(conv_dw_w, ((0, 0), (0, 1), (0, 0))), 'dwb': row(conv_dw_b),
        'lng': row(conv_ln_g), 'lnb': row(conv_ln_b), 'pw': conv_pw_w.astype(BF16), 'plw': plw.astype(BF16),
        'psc': row(pool_scale), 'w_o': w_o.astype(BF16), 'g2': row(norm2_g),
        'gw': ple_gate_w.astype(BF16), 'plw_e': ple_w.astype(BF16),
    }
    dense = {'wg': ffn_w_gate.astype(BF16)[:, None], 'wu': ffn_w_up.astype(BF16)[:, None],
             'wd': ffn_w_down.astype(BF16)[:, None]}
    sparse = {'wg': moe_w_gate.astype(BF16), 'wu': moe_w_up.astype(BF16), 'wd': moe_w_down.astype(BF16),
              'router': jnp.pad(moe_router, ((0, 0), (0, 0), (0, LANES - N_EXPERTS))).astype(BF16)}

    def sg_params(tc):
        return {'sgw': sg_w[:, :, :tc, :tc],
                'sgb': jnp.repeat(jnp.swapaxes(sg_b[:, :, :tc], 1, 2), GROUP_W // SG_GROUPS, axis=2)}

    sg_p, sg_s = sg_params(min(sp, 128)), sg_params(min(ss, 128))
    pp_all = p_prompt.reshape(depth, bp * sp, -1)
    ps_all = p_sample.reshape(depth, bs * ss, -1)
    ck_all = cache_k.reshape(depth, bs, past, 256).astype(BF16)
    cv_all = cache_v.reshape(depth, bs, past, 256).astype(BF16)
    clf_all = jnp.pad(cache_logf, ((0, 0), (0, 0), (0, 0), (0, LANES - N_HEADS))).reshape(depth, bs * past, LANES)
    sc_all = jnp.pad(state_conv, ((0, 0), (0, 0), (2, 0), (0, 0)))
    sp_all = jnp.pad(state_pool, ((0, 0), (0, 0), (1, 0), (0, 0)))
    cz, pz = jnp.zeros((bp, 32, GROUP_W), F32), jnp.zeros((bp, 16, GROUP_W), F32)

    outs_p, outs_s = [], []
    for l in range(depth):
        ffn_w = dense if l % 2 == 0 else sparse

        def layer_weights(sg):
            lw = {n: _LW(a, l) for n, a in {**stacked, **sg}.items()}
            lw.update({n: _LW(a, l // 2) for n, a in ffn_w.items()})
            return lw

        last = fg if l == depth - 1 else None
        lwp, lws = layer_weights(sg_p), layer_weights(sg_s)
        res_p, outs = _front(hp, lwp, bp, sp, None, cz, pz)
        outs_p.append(outs)
        res_s, outs = _front(hs, lws, bs, ss, (ck_all[l], cv_all[l], clf_all[l]), sc_all[l], sp_all[l])
        outs_s.append(outs)
        if l % 2 == 0:
            hp_res = _back_dense(res_p, _LW(pp_all, l), lwp, last)
            hs_res = _back_dense(res_s, _LW(ps_all, l), lws, last)
        else:
            hp_res, hs_res = _back_moe([(res_p, _LW(pp_all, l)), (res_s, _LW(ps_all, l))], lwp, last)
        if last is None:
            hp, hs = hp_res[0], hs_res[0]
        else:
            y_prompt, y_sample = hp_res[1], hs_res[1]

    def stack(outs, i, shape):
        return jnp.stack([o[i] for o in outs]).reshape((depth,) + shape)

    def stack_kv(i):
        if outs_p[0][i].shape == (bp * sp, GROUP_W):
            return stack(outs_p, i, (bp, sp, N_HEADS, HEAD_DIM))
        kv = stack(outs_p, i, (bp, N_HEADS, HEAD_DIM, sp))
        return jnp.transpose(kv, (0, 1, 4, 2, 3))

    return (y_prompt.reshape(bp, sp, D_MODEL), y_sample.reshape(bs, ss, D_MODEL),
            stack_kv(0), stack_kv(1),
            stack(outs_p, 2, (bp, sp, N_HEADS)), stack(outs_p, 3, (bp, CONV_W - 1, GROUP_W)),
            stack(outs_p, 4, (bp, POOL_HIST, GROUP_W)),
            stack(outs_s, 0, (bs, ss, N_HEADS, HEAD_DIM)), stack(outs_s, 1, (bs, ss, N_HEADS, HEAD_DIM)),
            stack(outs_s, 2, (bs, ss, N_HEADS)), stack(outs_s, 3, (bs, CONV_W - 1, GROUP_W)),
            stack(outs_s, 4, (bs, POOL_HIST, GROUP_W)), stack(outs_s, 5, (bs, ss, GROUP_W)))
```

```python
import functools
from typing import NamedTuple

import numpy as np
import jax
import jax.numpy as jnp
from jax import lax
from jax.experimental import pallas as pl
from jax.experimental.pallas import tpu as pltpu

F32 = jnp.float32
BF16 = jnp.bfloat16
EPS = 1e-6

D_MODEL = 1024
GROUP_W = 256
N_HEADS = 4
HEAD_DIM = 64
CONV_W = 31
POOL_WINDOWS = (2, 4, 8, 16)
POOL_HIST = 15
SG_GROUPS = 4
N_EXPERTS = 8
LANES = 128
AUG_PER_HEAD = 6
VMEM_LIMIT = 56 * 1024 * 1024


def _params(sem, vmem=VMEM_LIMIT):
    return pltpu.CompilerParams(dimension_semantics=sem, vmem_limit_bytes=vmem)


def _dot(a, b):
    return jnp.dot(a, b, preferred_element_type=F32)


class _LW(NamedTuple):
    arr: jax.Array
    idx: int


def _wspec(w):
    nd = w.arr.ndim - 1
    return pl.BlockSpec((None,) + w.arr.shape[1:], lambda *_: (w.idx,) + (0,) * nd)


def _split3(x):
    hi = x.astype(BF16)
    r1 = x - hi.astype(F32)
    mid = r1.astype(BF16)
    lo = (r1 - mid.astype(F32)).astype(BF16)
    return jnp.concatenate([hi, mid, lo], axis=1)


def _running_sum(x, tril, carry_ref):
    cs = _dot(tril, _split3(x))
    c = carry_ref[...] + (cs[:, :LANES] + cs[:, LANES:2 * LANES] + cs[:, 2 * LANES:])
    n = x.shape[0]
    carry_ref[...] = c[n - 1:n, :]
    return c


def _placement(sign_k):
    pq = np.zeros((3 * LANES, 2 * LANES), np.float32)
    pk = np.zeros((3 * LANES, 2 * LANES), np.float32)
    oq = np.zeros((1, 2 * LANES), np.float32)
    ok = np.zeros((1, 2 * LANES), np.float32)
    for h in range(N_HEADS):
        p, j = divmod(h, 2)
        for i in range(3):
            pq[i * LANES + h, p * LANES + j * AUG_PER_HEAD + i] = 1.0
            oq[0, p * LANES + j * AUG_PER_HEAD + 3 + i] = 1.0
            pk[i * LANES + h, p * LANES + j * AUG_PER_HEAD + 3 + i] = sign_k
            ok[0, p * LANES + j * AUG_PER_HEAD + i] = 1.0
    return pq, pk, oq, ok


def _in_proj_kernel(h_ref, g_ref, w_ref, bf_ref, c0_ref, tril_ref, pq_ref, pk_ref, oq_ref, ok_ref,
                    qs_ref, kf_ref, vf_ref, kb_ref, vb_ref, lf_ref, aq_ref, ak_ref, zc_ref, carry_ref,
                    kv_transposed=False):
    @pl.when(pl.program_id(1) == 0)
    def _():
        carry_ref[...] = c0_ref[0]

    x = h_ref[...]
    hn = x * lax.rsqrt(jnp.mean(x * x, axis=-1, keepdims=True) + EPS) * g_ref[...]
    hb = hn.astype(BF16)
    qs_ref[...] = (_dot(hb, w_ref[:, 0:256]) * (HEAD_DIM ** -0.5)).astype(BF16)
    k = _dot(hb, w_ref[:, 256:512])
    kf_ref[...] = k.T if kv_transposed else k
    kb_ref[...] = k.astype(BF16)
    v = _dot(hb, w_ref[:, 512:768])
    vf_ref[...] = v.T if kv_transposed else v
    vb_ref[...] = v.astype(BF16)
    zc_ref[...] = _dot(hb, w_ref[:, 768:2048])
    zf = _dot(hb, w_ref[:, 2048:2176]) + bf_ref[...]
    lf = jnp.minimum(zf, 0.0) - jnp.log1p(jnp.exp(-jnp.abs(zf)))
    lane = lax.broadcasted_iota(jnp.int32, lf.shape, 1)
    lf = jnp.where(lane < N_HEADS, lf, 0.0)
    lf_ref[...] = lf
    cp = _split3(_running_sum(lf, tril_ref[...], carry_ref))
    aq_ref[...] = (_dot(cp, pq_ref[...]) + oq_ref[...]).astype(BF16)
    ak_ref[...] = (_dot(cp, pk_ref[...]) + ok_ref[...]).astype(BF16)


N_IN_PROJ_INPUTS, N_IN_PROJ_OUTPUTS, N_MIX_INPUTS = 10, 8, 11


def _in_mix_kernel(*refs, ts, tc, pos0, want_sgv, kv_transposed):
    proj_in = refs[:N_IN_PROJ_INPUTS]
    mix_in = refs[N_IN_PROJ_INPUTS:N_IN_PROJ_INPUTS + N_MIX_INPUTS]
    rest = refs[N_IN_PROJ_INPUTS + N_MIX_INPUTS:]
    proj_out, rest = rest[:N_IN_PROJ_OUTPUTS], rest[N_IN_PROJ_OUTPUTS:]
    bcd_ref, cst_ref, pst_ref = rest[:3]
    sgv_ref = rest[3] if want_sgv else None
    carry_ref, zc_sc, xc_ref, xp_ref = rest[3 + want_sgv:]
    _in_proj_kernel(*proj_in, *proj_out, zc_sc, carry_ref, kv_transposed=kv_transposed)
    z = [zc_sc.at[:, j * GROUP_W:(j + 1) * GROUP_W] for j in range(5)]
    _mixers_kernel(*z, *mix_in, bcd_ref, cst_ref, pst_ref, sgv_ref, xc_ref, xp_ref, ts=ts, tc=tc, pos0=pos0)


def _in_mix(h, lw, c0, cprev, pprev, nb, seq, ts, tc, pos0, want_sgv, kv_transposed):
    t = nb * seq
    ns = seq // ts
    tril = jnp.asarray(np.tril(np.ones((ts, ts), np.float32)), BF16)
    pq, pk, oq, ok = _placement(-1.0)
    row = lambda b, s: (b * ns + s, 0)
    const = lambda b, s: (0, 0)
    blk = lambda w_: pl.BlockSpec((ts, w_), row)
    full = lambda a: pl.BlockSpec(a.shape, const)
    perb = lambda r: pl.BlockSpec((1, r, GROUP_W), lambda b, s: (b, 0, 0))
    consts = [jnp.asarray(pq, BF16), jnp.asarray(pk, BF16), jnp.asarray(oq), jnp.asarray(ok)]
    small = [lw[n] for n in ('dw', 'dwb', 'lng', 'lnb', 'pw', 'plw', 'psc', 'sgw', 'sgb')]
    outs = [(256, BF16), (256, F32), (256, F32), (256, BF16), (256, BF16), (LANES, F32), (256, BF16), (256, BF16),
            (768, BF16)]
    out_specs = [blk(w_) for w_, _ in outs] + [perb(32), perb(16)]
    out_shape = [jax.ShapeDtypeStruct((t, w_), dt) for w_, dt in outs] + [
        jax.ShapeDtypeStruct((nb, 32, GROUP_W), F32), jax.ShapeDtypeStruct((nb, 16, GROUP_W), F32)]
    if kv_transposed:
        for i in (1, 2):
            out_specs[i] = pl.BlockSpec((GROUP_W, ts), lambda b, s: (b, s))
            out_shape[i] = jax.ShapeDtypeStruct((nb * GROUP_W, seq), F32)
    if want_sgv:
        out_specs.append(blk(GROUP_W))
        out_shape.append(jax.ShapeDtypeStruct((t, GROUP_W), F32))
    return pl.pallas_call(
        functools.partial(_in_mix_kernel, ts=ts, tc=tc, pos0=pos0, want_sgv=want_sgv, kv_transposed=kv_transposed),
        grid=(nb, ns),
        in_specs=[blk(D_MODEL), _wspec(lw['g1']), _wspec(lw['w_in']), _wspec(lw['bf']),
                  pl.BlockSpec((1, 1, LANES), lambda b, s: (b, 0, 0)), full(tril)] + [full(a) for a in consts]
                 + [perb(32), perb(16)] + [_wspec(a) for a in small],
        out_specs=out_specs,
        out_shape=out_shape,
        scratch_shapes=[pltpu.VMEM((1, LANES), F32), pltpu.VMEM((ts, 5 * GROUP_W), F32),
                        pltpu.VMEM((40 + ts, GROUP_W), F32), pltpu.VMEM((16 + ts, GROUP_W), F32)],
        compiler_params=_params(("arbitrary", "arbitrary")),
        name="in_mix",
    )(h, lw['g1'].arr, lw['w_in'].arr, lw['bf'].arr, c0, tril, *consts, cprev, pprev, *[a.arr for a in small])


def _cache_sum_kernel(lf_ref, tril_ref, pk_ref, ok_ref, ak_ref, tot_ref, carry_ref):
    @pl.when(pl.program_id(1) == 0)
    def _():
        carry_ref[...] = jnp.zeros_like(carry_ref)

    cp = _split3(_running_sum(lf_ref[...], tril_ref[...], carry_ref))
    ak_ref[...] = (_dot(cp, pk_ref[...]) + ok_ref[...]).astype(BF16)
    tot_ref[0] = carry_ref[...]


def _cache_sum(lfp, nb, past, tm):
    ns = past // tm
    tril = jnp.asarray(np.tril(np.ones((tm, tm), np.float32)), BF16)
    _, pk, _, ok = _placement(-1.0)
    pk, ok = jnp.asarray(pk, BF16), jnp.asarray(ok)
    const = lambda b, s: (0, 0)
    return pl.pallas_call(
        _cache_sum_kernel,
        grid=(nb, ns),
        in_specs=[pl.BlockSpec((tm, LANES), lambda b, s: (b * ns + s, 0)), pl.BlockSpec(tril.shape, const),
                  pl.BlockSpec(pk.shape, const), pl.BlockSpec(ok.shape, const)],
        out_specs=[pl.BlockSpec((tm, 256), lambda b, s: (b * ns + s, 0)),
                   pl.BlockSpec((1, 1, LANES), lambda b, s: (b, 0, 0))],
        out_shape=[jax.ShapeDtypeStruct((nb * past, 256), BF16), jax.ShapeDtypeStruct((nb, 1, LANES), F32)],
        scratch_shapes=[pltpu.VMEM((1, LANES), F32)],
        compiler_params=_params(("arbitrary", "arbitrary")),
        name="cache_sum",
    )(lfp, tril, pk, ok)


NEG_BIG = -1e30


def _head_lhs(q, aq, j, lane):
    zero = jnp.zeros((), BF16)
    qm = jnp.where((lane >= HEAD_DIM * j) & (lane < HEAD_DIM * (j + 1)), q, zero)
    am = jnp.where((lane >= AUG_PER_HEAD * j) & (lane < AUG_PER_HEAD * (j + 1)), aq, zero)
    return jnp.concatenate([qm, am], axis=1)


def _scores(lhs, k, ak):
    rhs = jnp.concatenate([k, ak], axis=1)
    return lax.dot_general(lhs, rhs, (((1,), (1,)), ((), ())), preferred_element_type=F32)


def _attn_kernel(q_ref, aq_ref, k_ref, ak_ref, v_ref, o_ref, m_sc, l_sc, acc_sc, e_sc, f_sc, *, t):
    qi = pl.program_id(1)
    lane = lax.broadcasted_iota(jnp.int32, (1, LANES), 1)
    row = lax.broadcasted_iota(jnp.int32, (t, t), 0)
    col = lax.broadcasted_iota(jnp.int32, (t, t), 1)
    nct = t // LANES
    for p in range(2):
        cs = slice(p * LANES, (p + 1) * LANES)
        lhs = [_head_lhs(q_ref[:, cs], aq_ref[:, cs], j, lane) for j in range(2)]

        def scores(j, ki, masked):
            rows = pl.ds(pl.multiple_of(ki * t, t), t)
            s = _scores(lhs[j], k_ref[rows, cs], ak_ref[rows, cs])
            if masked:
                s = jnp.where(col <= row, s, -jnp.inf)
            return [s[:, c * LANES:(c + 1) * LANES] for c in range(nct)]

        def stats(ki, masked):
            for j in range(2):
                parts = scores(j, ki, masked)
                m_old = m_sc[j]
                m_new = functools.reduce(jnp.maximum, parts, m_old)
                tot = l_sc[j] * jnp.exp(m_old - m_new)
                for c, part in enumerate(parts):
                    e = jnp.exp(part - m_new)
                    e_sc[j, ki, :, c * LANES:(c + 1) * LANES] = e
                    tot = tot + e
                l_sc[j] = tot
                m_sc[j] = m_new
                f_sc[j, ki] = m_new

        def body(ki, carry):
            stats(ki, False)
            return carry

        m_sc[...] = jnp.full_like(m_sc, NEG_BIG)
        l_sc[...] = jnp.zeros_like(l_sc)
        lax.fori_loop(0, qi, body, 0)
        stats(qi, True)
        for j in range(2):
            m_lane = m_sc[j]
            m_row = jnp.max(m_lane, axis=1, keepdims=True)
            l_row = jnp.sum(l_sc[j] * jnp.exp(m_lane - m_row), axis=1, keepdims=True)
            m_sc[j] = jnp.broadcast_to(m_row, m_lane.shape)
            l_sc[j] = jnp.broadcast_to(1.0 / l_row, m_lane.shape)

        def rescale(ki, carry):
            for j in range(2):
                f_sc[j, ki] = jnp.exp(f_sc[j, ki] - m_sc[j]) * l_sc[j]
            return carry

        lax.fori_loop(0, qi + 1, rescale, 0)
        acc_sc[...] = jnp.zeros_like(acc_sc)

        def accumulate(ki, carry):
            v = v_ref[pl.ds(pl.multiple_of(ki * t, t), t), cs]
            for j in range(2):
                f = f_sc[j, ki]
                prob = jnp.concatenate([(e_sc[j, ki, :, c * LANES:(c + 1) * LANES] * f).astype(BF16)
                                        for c in range(nct)], axis=1)
                acc_sc[j] += _dot(prob, v)
            return carry

        lax.fori_loop(0, qi + 1, accumulate, 0)
        o_ref[:, cs] = jnp.where(lane < HEAD_DIM, acc_sc[0], acc_sc[1]).astype(BF16)


N_ATTN_INPUTS, N_ATTN_SCRATCH = 5, 5


def _attn_out_kernel(*refs, t, moe):
    attn_in, rest = refs[:N_ATTN_INPUTS], refs[N_ATTN_INPUTS:]
    n_proj = 4 + 2 * moe + 2 + 2 * moe
    proj, rest = rest[:n_proj], rest[n_proj:]
    attn_sc, a_sc, carry = rest[:N_ATTN_SCRATCH], rest[N_ATTN_SCRATCH], rest[N_ATTN_SCRATCH + 1:]
    _attn_kernel(*attn_in, a_sc, *attn_sc, t=t)
    first = (pl.program_id(0) == 0) & (pl.program_id(1) == 0)
    _out_proj_kernel(a_sc, *proj, *carry, moe=moe, first_step=first)


def _attn_out(qs, aq, kb, ak, vb, bcd, h, wo, g2, router, nb, seq, t):
    assert seq % t == 0 and t % LANES == 0
    nq = seq // t
    tt = nb * seq
    moe = router is not None
    row = lambda b, i: (b * nq + i, 0)
    const = lambda b, i: (0, 0)
    qblk = pl.BlockSpec((t, 256), row)
    kblk = pl.BlockSpec((seq, 256), lambda b, i: (b, 0))
    ins = [qs, aq, kb, ak, vb, bcd, h, wo.arr, g2.arr]
    in_specs = [qblk, qblk, kblk, kblk, kblk, pl.BlockSpec((t, 768), row), pl.BlockSpec((t, D_MODEL), row),
                _wspec(wo), _wspec(g2)]
    out_specs = [pl.BlockSpec((t, D_MODEL), row), pl.BlockSpec((t, D_MODEL), row)]
    out_shape = [jax.ShapeDtypeStruct((tt, D_MODEL), F32), jax.ShapeDtypeStruct((tt, D_MODEL), F32 if moe else BF16)]
    scratch = [pltpu.VMEM((2, t, LANES), F32)] * 3 + [pltpu.VMEM((2, nq, t, t), F32),
                                                      pltpu.VMEM((2, nq, t, LANES), F32), pltpu.VMEM((t, 256), BF16)]
    if moe:
        stril = jnp.asarray(np.tril(np.ones((t, t), np.float32), -1), BF16)
        ins += [router.arr, stril]
        in_specs += [_wspec(router), pl.BlockSpec(stril.shape, const)]
        out_specs += [pl.BlockSpec((t, LANES), row), pl.BlockSpec((1, LANES), const)]
        out_shape += [jax.ShapeDtypeStruct((tt, LANES), F32), jax.ShapeDtypeStruct((1, LANES), F32)]
        scratch.append(pltpu.VMEM((1, LANES), F32))
    return pl.pallas_call(
        functools.partial(_attn_out_kernel, t=t, moe=moe),
        grid=(nb, nq),
        in_specs=in_specs, out_specs=out_specs, out_shape=out_shape, scratch_shapes=scratch,
        compiler_params=_params(("arbitrary", "arbitrary")),
        name="attn_out",
    )(*ins)


def _attn_rows_kernel(q_ref, aq_ref, k_ref, ak_ref, v_ref, o_ref, *, past):
    lq, lk = q_ref.shape[0], k_ref.shape[0]
    lane = lax.broadcasted_iota(jnp.int32, (1, LANES), 1)
    row = lax.broadcasted_iota(jnp.int32, (lq, lk), 0)
    col = lax.broadcasted_iota(jnp.int32, (lq, lk), 1)
    for p in range(2):
        cs = slice(p * LANES, (p + 1) * LANES)
        outs = []
        for j in range(2):
            s = _scores(_head_lhs(q_ref[:, cs], aq_ref[:, cs], j, lane), k_ref[:, cs], ak_ref[:, cs])
            s = jnp.where(col <= past + row, s, -jnp.inf)
            e = jnp.exp(s - jnp.max(s, axis=1, keepdims=True))
            prob = (e * (1.0 / jnp.sum(e, axis=1, keepdims=True))).astype(BF16)
            outs.append(_dot(prob, v_ref[:, cs]))
        o_ref[:, cs] = jnp.where(lane < HEAD_DIM, outs[0], outs[1]).astype(BF16)


def _attention_rows(qs, aq, kb, ak, vb, nb, lq, lk):
    qblk = pl.BlockSpec((lq, 256), lambda b: (b, 0))
    kblk = pl.BlockSpec((lk, 256), lambda b: (b, 0))
    return pl.pallas_call(
        functools.partial(_attn_rows_kernel, past=lk - lq),
        grid=(nb,),
        in_specs=[qblk, qblk, kblk, kblk, kblk],
        out_specs=qblk,
        out_shape=jax.ShapeDtypeStruct((nb * lq, 256), BF16),
        compiler_params=_params(("arbitrary",)),
        name="attn_rows",
    )(qs, aq, kb, ak, vb)


ROW_CHUNK = 64


def _gelu(x):
    return 0.5 * x * (1.0 + jnp.tanh(0.7978845608028654 * (x + 0.044715 * (x * x * x))))


def _mixers_kernel(ca_ref, cg_ref, px_ref, su_ref, sv_ref, cprev_ref, pprev_ref, dw_ref, dwb_ref, lng_ref, lnb_ref,
                   pw_ref, plw_ref, psc_ref, sgw_ref, sgb_ref,
                   bcd_ref, cst_ref, pst_ref, sgv_ref, xc_ref, xp_ref, *, ts, tc, pos0):
    s = pl.program_id(1)

    @pl.when(s == 0)
    def _():
        xc_ref[0:32, :] = cprev_ref[0].astype(BF16).astype(F32)
        xp_ref[0:16, :] = pprev_ref[0]

    @pl.when(s > 0)
    def _():
        xc_ref[0:32, :] = xc_ref[ts:ts + 32, :]
        xp_ref[0:16, :] = xp_ref[ts:ts + 16, :]

    g = cg_ref[...]
    u = ca_ref[...] * (1.0 / (1.0 + jnp.exp(-g)))
    xc_ref[32:32 + ts, :] = u.astype(BF16).astype(F32)
    xc_ref[32 + ts:40 + ts, :] = jnp.zeros((8, GROUP_W), F32)
    cst_ref[0] = u[ts - 32:ts, :]
    xp_ref[16:16 + ts, :] = px_ref[...]
    pst_ref[0] = xp_ref[ts:ts + 16, :]

    rc = min(ts, ROW_CHUNK)
    lane = lax.broadcasted_iota(jnp.int32, (rc, GROUP_W), 1)
    conv_act, pool_dlt = [], []
    for r0 in range(0, ts, rc):
        y = jnp.zeros((rc, GROUP_W), F32)
        for r in range(8):
            part = jnp.zeros((rc + 8, GROUP_W), F32)
            for a in range(r if r >= 2 else r + 8, CONV_W + 2, 8):
                part = part + xc_ref[r0 + a - r:r0 + a - r + rc + 8, :] * dw_ref[a - 2:a - 1, :]
            y = y + part[r:r + rc, :]
        y = y + dwb_ref[...]
        mu = jnp.mean(y, axis=-1, keepdims=True)
        var = jnp.mean(jnp.square(y - mu), axis=-1, keepdims=True)
        y = (y - mu) * lax.rsqrt(var + EPS) * lng_ref[...] + lnb_ref[...]
        conv_act.append((y * (1.0 / (1.0 + jnp.exp(-y)))).astype(BF16))

        x = xp_ref[16 + r0:16 + r0 + rc, :]
        pos = (pos0 + s * ts + r0 + lax.broadcasted_iota(jnp.int32, (rc, 1), 0)).astype(F32)
        acc = x
        mean = jnp.zeros((rc, GROUP_W), F32)
        wi = 0
        for i in range(1, POOL_WINDOWS[-1] + 1):
            if i > 1:
                acc = acc + xp_ref[16 - (i - 1) + r0:16 - (i - 1) + r0 + rc, :]
            if i == POOL_WINDOWS[wi]:
                cnt = jnp.minimum(pos + 1.0, float(i))
                mean = jnp.where((lane >= 64 * wi) & (lane < 64 * (wi + 1)), acc / cnt, mean)
                wi += 1
        pool_dlt.append((mean - x).astype(BF16))
    bcd_ref[:, 0:256] = _dot(jnp.concatenate(conv_act, axis=0), pw_ref[...]).astype(BF16)
    bcd_ref[:, 256:512] = (_dot(jnp.concatenate(pool_dlt, axis=0), plw_ref[...]) * psc_ref[...]).astype(BF16)

    r = lax.broadcasted_iota(jnp.int32, (tc, tc), 0)
    c = lax.broadcasted_iota(jnp.int32, (tc, tc), 1)
    lane_c = lax.broadcasted_iota(jnp.int32, (tc, GROUP_W), 1)
    wts = [jnp.where(c <= r, sgw_ref[gi], 0.0).astype(BF16) for gi in range(SG_GROUPS)]
    for ci in range(ts // tc):
        uu = _gelu(su_ref[ci * tc:(ci + 1) * tc, :])
        vv = _gelu(sv_ref[ci * tc:(ci + 1) * tc, :])
        if sgv_ref is not None:
            sgv_ref[ci * tc:(ci + 1) * tc, :] = vv
        m = sgb_ref[...]
        for gi in range(SG_GROUPS):
            vg = jnp.where((lane_c >= 64 * gi) & (lane_c < 64 * (gi + 1)), vv, 0.0).astype(BF16)
            m = m + _dot(wts[gi], vg)
        bcd_ref[ci * tc:(ci + 1) * tc, 512:768] = (uu * m).astype(BF16)


R_GATE, R_EXPERT, R_RANK = 0, 2, 4


def _out_proj_kernel(a_ref, bcd_ref, h_ref, wo_ref, g_ref, *rest, moe, first_step=None):
    if first_step is None:
        first_step = pl.program_id(0) == 0
    if moe:
        router_ref, stril_ref, h1_ref, hn_ref, route_ref, cnt_ref, carry_ref = rest
    else:
        h1_ref, hn_ref = rest
    mix = _dot(a_ref[...], wo_ref[0:256, :]) + _dot(bcd_ref[...], wo_ref[256:1024, :])
    h1 = h_ref[...] + mix
    hn = h1 * lax.rsqrt(jnp.mean(h1 * h1, axis=-1, keepdims=True) + EPS) * g_ref[...]
    h1_ref[...] = h1
    hn_ref[...] = hn.astype(hn_ref.dtype)
    if moe:
        @pl.when(first_step)
        def _():
            carry_ref[...] = jnp.zeros_like(carry_ref)

        logits = _dot(hn.astype(BF16), router_ref[...])
        lane = lax.broadcasted_iota(jnp.int32, logits.shape, 1)
        logits = jnp.where(lane < N_EXPERTS, logits, -jnp.inf)
        v1 = jnp.max(logits, axis=1, keepdims=True)
        i1 = jnp.min(jnp.where(logits == v1, lane, LANES), axis=1, keepdims=True)
        rest_l = jnp.where(lane == i1, -jnp.inf, logits)
        v2 = jnp.max(rest_l, axis=1, keepdims=True)
        i2 = jnp.min(jnp.where(rest_l == v2, lane, LANES), axis=1, keepdims=True)
        e = jnp.exp(v2 - v1)
        g1 = 1.0 / (1.0 + e)
        oh1, oh2 = lane == i1, lane == i2
        oh = jnp.where(oh1 | oh2, 1.0, 0.0)
        before = _dot(stril_ref[...], oh.astype(BF16)) + carry_ref[...]
        r1 = jnp.sum(jnp.where(oh1, before, 0.0), axis=1, keepdims=True)
        r2 = jnp.sum(jnp.where(oh2, before, 0.0), axis=1, keepdims=True)
        carry_ref[...] += jnp.sum(oh, axis=0, keepdims=True)
        cnt_ref[...] = carry_ref[...]
        rec = jnp.zeros_like(logits)
        for ln, val in ((R_GATE, g1), (R_GATE + 1, e * g1), (R_EXPERT, i1.astype(F32)), (R_EXPERT + 1, i2.astype(F32)),
                        (R_RANK, r1), (R_RANK + 1, r2)):
            rec = jnp.where(lane == ln, val, rec)
        route_ref[...] = rec


def _out_proj(a, bcd, h, wo, g2, router, tm):
    t = h.shape[0]
    moe = router is not None
    row = lambda i: (i, 0)
    const = lambda i: (0, 0)
    ins = [a, bcd, h, wo.arr, g2.arr]
    in_specs = [pl.BlockSpec((tm, 256), row), pl.BlockSpec((tm, 768), row), pl.BlockSpec((tm, D_MODEL), row),
                _wspec(wo), _wspec(g2)]
    out_specs = [pl.BlockSpec((tm, D_MODEL), row), pl.BlockSpec((tm, D_MODEL), row)]
    out_shape = [jax.ShapeDtypeStruct((t, D_MODEL), F32), jax.ShapeDtypeStruct((t, D_MODEL), F32 if moe else BF16)]
    scratch = []
    if moe:
        stril = jnp.asarray(np.tril(np.ones((tm, tm), np.float32), -1), BF16)
        ins += [router.arr, stril]
        in_specs += [_wspec(router), pl.BlockSpec(stril.shape, const)]
        out_specs += [pl.BlockSpec((tm, LANES), row), pl.BlockSpec((1, LANES), const)]
        out_shape += [jax.ShapeDtypeStruct((t, LANES), F32), jax.ShapeDtypeStruct((1, LANES), F32)]
        scratch = [pltpu.VMEM((1, LANES), F32)]
    return pl.pallas_call(
        functools.partial(_out_proj_kernel, moe=moe),
        grid=(t // tm,),
        in_specs=in_specs, out_specs=out_specs, out_shape=out_shape, scratch_shapes=scratch,
        compiler_params=_params(("arbitrary",)),
        name="out_proj",
    )(*ins)


DMA_UNROLL = 8


def _row_copy(src_ref, src_row, dst_ref, dst_row, sem):
    return pltpu.make_async_copy(src_ref.at[pl.ds(src_row, 1)], dst_ref.at[pl.ds(dst_row, 1)], sem)


def _dispatch_kernel(pos_ref, x_ref, *rest, tm):
    xs_ref, sem = rest[-2:]

    def issue(r, carry):
        for k in range(2):
            _row_copy(x_ref, r, xs_ref, pos_ref[0, 0, k * tm + r], sem).start(priority=k)
        return carry

    def drain(r, carry):
        for k in range(2):
            _row_copy(x_ref, r, xs_ref, pos_ref[0, 0, k * tm + r], sem).wait()
        return carry

    lax.fori_loop(0, tm, issue, 0, unroll=DMA_UNROLL)
    lax.fori_loop(0, tm, drain, 0, unroll=DMA_UNROLL)


def _dispatch(hn, pos_tiles, n_rows, tm, xs_prev=None):
    t = hn.shape[0]
    ins = [pos_tiles, hn]
    in_specs = [pl.BlockSpec((1, 1, 2 * tm), lambda i: (i, 0, 0), memory_space=pltpu.SMEM),
                pl.BlockSpec((tm, D_MODEL), lambda i: (i, 0))]
    aliases = {}
    if xs_prev is not None:
        ins.append(xs_prev)
        in_specs.append(pl.BlockSpec(memory_space=pl.ANY))
        aliases = {2: 0}
    return pl.pallas_call(
        functools.partial(_dispatch_kernel, tm=tm),
        grid=(t // tm,),
        in_specs=in_specs,
        out_specs=pl.BlockSpec(memory_space=pl.ANY),
        out_shape=jax.ShapeDtypeStruct((n_rows, D_MODEL), F32),
        scratch_shapes=[pltpu.SemaphoreType.DMA],
        input_output_aliases=aliases,
        compiler_params=_params(("arbitrary",)),
        name="dispatch",
    )(*ins)


def _ffn_kernel(wt_ref, we_ref, lo_ref, hi_ref, first_ref, x_ref, wg_ref, wu_ref, wd_ref, y_ref):
    w = pl.program_id(0)
    lo, hi = lo_ref[w], hi_ref[w]

    @pl.when(hi > lo)
    def _():
        rowi = lax.broadcasted_iota(jnp.int32, (x_ref.shape[0], 1), 0)
        x = jnp.where((rowi >= lo) & (rowi < hi), x_ref[...], jnp.zeros((), x_ref.dtype)).astype(BF16)
        gate = _dot(x, wg_ref[0])
        up = _dot(x, wu_ref[0])
        act = (gate * (1.0 / (1.0 + jnp.exp(-gate))) * up).astype(BF16)
        y = _dot(act, wd_ref[0])

        @pl.when(first_ref[w] == 1)
        def _():
            y_ref[...] = y

        @pl.when(first_ref[w] == 0)
        def _():
            y_ref[...] += y


def _ffn(xs, wg, wu, wd, tabs, tm):
    d_ff = wg.arr.shape[-1]
    xmap = lambda w, wt, we, lo, hi, fi: (wt[w], 0)
    wmap = lambda w, wt, we, lo, hi, fi: (wg.idx, we[w], 0, 0)
    grid_spec = pltpu.PrefetchScalarGridSpec(
        num_scalar_prefetch=5,
        grid=(tabs[0].shape[0],),
        in_specs=[pl.BlockSpec((tm, D_MODEL), xmap),
                  pl.BlockSpec((None, 1, D_MODEL, d_ff), wmap), pl.BlockSpec((None, 1, D_MODEL, d_ff), wmap),
                  pl.BlockSpec((None, 1, d_ff, D_MODEL), wmap)],
        out_specs=pl.BlockSpec((tm, D_MODEL), xmap),
    )
    return pl.pallas_call(
        _ffn_kernel,
        grid_spec=grid_spec,
        out_shape=jax.ShapeDtypeStruct((xs.shape[0], D_MODEL), F32),
        compiler_params=_params(("arbitrary",)),
        name="ffn",
    )(*tabs, xs, wg.arr, wu.arr, wd.arr)


def _work_items(counts, n_rows, tm):
    i32 = jnp.int32
    ends = jnp.cumsum(counts)
    starts = ends - counts
    t_first = starts // tm
    n_e = jnp.where(counts > 0, (ends - 1) // tm - t_first + 1, 0)
    w_end = jnp.cumsum(n_e)
    w_start = w_end - n_e
    n_work = n_rows // tm + counts.shape[0] - 1
    w = jnp.arange(n_work, dtype=i32)
    valid = w < w_end[-1]
    e = jnp.minimum(jnp.sum((w[:, None] >= w_end[None, :]).astype(i32), axis=1), counts.shape[0] - 1)
    tile = jnp.where(valid, t_first[e] + (w - w_start[e]), n_rows // tm - 1).astype(i32)
    lo = jnp.where(valid, jnp.maximum(starts[e] - tile * tm, 0), 0).astype(i32)
    hi = jnp.where(valid, jnp.minimum(ends[e] - tile * tm, tm), 0).astype(i32)
    e = jnp.where(valid, e, e[jnp.maximum(w_end[-1] - 1, 0)]).astype(i32)
    first = jnp.concatenate([jnp.ones((1,), i32), (tile[1:] != tile[:-1]).astype(i32)])
    return (tile, e, lo, hi, first), starts


def _ple_kernel(*refs, moe, final, tm):
    refs = list(refs)
    if moe:
        pos_ref, pos_next_ref = refs.pop(0), refs.pop(0)
    h1_ref, y_ref = refs[0], refs[1]
    k = 2
    if moe:
        route_ref = refs[k]
        k += 1
    p_ref, gw_ref, pw_ref = refs[k:k + 3]
    k += 3
    if final:
        fg_ref = refs[k]
        k += 1
    h_ref = refs[k]
    if moe:
        ysc_ref, sems = refs[-2], refs[-1]
        i = pl.program_id(0)
        slot = i % 2

        def gather(pos, slot_, act):
            def body(r, carry):
                for s in range(2):
                    cp = _row_copy(y_ref, pos[0, 0, s * tm + r], ysc_ref.at[slot_, s], r, sems.at[slot_])
                    act(cp, s)
                return carry
            lax.fori_loop(0, tm, body, 0, unroll=DMA_UNROLL)

        start = lambda cp, s: cp.start(priority=s)
        wait = lambda cp, s: cp.wait()

        @pl.when(i == 0)
        def _():
            gather(pos_ref, 0, start)

        @pl.when(i + 1 < pl.num_programs(0))
        def _():
            gather(pos_next_ref, 1 - slot, start)

        gather(pos_ref, slot, wait)
        rec = route_ref[...]
        lane = lax.broadcasted_iota(jnp.int32, rec.shape, 1)
        g1 = jnp.sum(jnp.where(lane == R_GATE, rec, 0.0), axis=1, keepdims=True)
        g2 = jnp.sum(jnp.where(lane == R_GATE + 1, rec, 0.0), axis=1, keepdims=True)
        h2 = h1_ref[...] + (g1 * ysc_ref[slot, 0] + g2 * ysc_ref[slot, 1])
    else:
        h2 = h1_ref[...] + y_ref[...]
    gate = _dot(h2.astype(BF16), gw_ref[...])
    pe = _dot(p_ref[...].astype(BF16), pw_ref[...])
    h3 = h2 + (1.0 / (1.0 + jnp.exp(-gate))) * pe
    h_ref[...] = h3
    if final:
        refs[k + 1][...] = h3 * lax.rsqrt(jnp.mean(h3 * h3, axis=-1, keepdims=True) + EPS) * fg_ref[...]


def _ple(h1, ys, route, pos_tiles, p, gw, pw, fg, tm):
    t = h1.shape[0]
    moe = route is not None
    final = fg is not None
    row = lambda i: (i, 0)
    const = lambda i: (0, 0)
    ins, in_specs, scratch = [], [], []
    if moe:
        last_tile = t // tm - 1
        ins += [pos_tiles, pos_tiles, h1, ys, route]
        in_specs += [pl.BlockSpec((1, 1, 2 * tm), lambda i: (i, 0, 0), memory_space=pltpu.SMEM),
                     pl.BlockSpec((1, 1, 2 * tm), lambda i: (jnp.minimum(i + 1, last_tile), 0, 0),
                                  memory_space=pltpu.SMEM),
                     pl.BlockSpec((tm, D_MODEL), row), pl.BlockSpec(memory_space=pl.ANY), pl.BlockSpec((tm, LANES), row)]
        scratch = [pltpu.VMEM((2, 2, tm, D_MODEL), F32), pltpu.SemaphoreType.DMA((2,))]
    else:
        ins += [h1, ys]
        in_specs += [pl.BlockSpec((tm, D_MODEL), row), pl.BlockSpec((tm, D_MODEL), row)]
    ins += [p.arr, gw.arr, pw.arr]
    in_specs += [pl.BlockSpec((None, tm, 256), lambda i: (p.idx, i, 0)), _wspec(gw), _wspec(pw)]
    out_specs = [pl.BlockSpec((tm, D_MODEL), row)]
    out_shape = [jax.ShapeDtypeStruct((t, D_MODEL), F32)]
    if final:
        ins.append(fg)
        in_specs.append(pl.BlockSpec(fg.shape, const))
        out_specs.append(pl.BlockSpec((tm, D_MODEL), row))
        out_shape.append(jax.ShapeDtypeStruct((t, D_MODEL), F32))
    return pl.pallas_call(
        functools.partial(_ple_kernel, moe=moe, final=final, tm=tm),
        grid=(t // tm,),
        in_specs=in_specs, out_specs=out_specs, out_shape=out_shape, scratch_shapes=scratch,
        compiler_params=_params(("arbitrary",)),
        name="ple",
    )(*ins)


def _tile(n, pref):
    return pref if n % pref == 0 else n


def _front(h, lw, nb, seq, cache, cprev, pprev):
    t = nb * seq
    tm = _tile(seq, 512)
    if cache is None:
        c0 = jnp.zeros((nb, 1, LANES), F32)
        past = 0
    else:
        ck, cv, clf = cache
        past = ck.shape[1]
        akc, c0 = _cache_sum(clf, nb, past, _tile(past, 512))
    outs = _in_mix(h, lw, c0, cprev, pprev, nb, seq, tm, min(seq, 128), past, cache is not None,
                   kv_transposed=cache is None and seq % LANES == 0)
    qs, kf, vf, kb, vb, lf, aq, ak, bcd, cst, pst = outs[:11]
    sgv = outs[11] if cache is not None else None
    if cache is None:
        res = _attn_out(qs, aq, kb, ak, vb, bcd, h, lw['w_o'], lw['g2'], lw.get('router'), nb, seq, tm)
    else:
        lk = past + seq
        cat = lambda c_, n_: jnp.concatenate([c_.reshape(nb, past, 256), n_.reshape(nb, seq, 256)], axis=1
                                             ).reshape(nb * lk, 256)
        a = _attention_rows(qs, aq, cat(ck, kb), cat(akc, ak), cat(cv, vb), nb, seq, lk)
        res = _out_proj(a, bcd, h, lw['w_o'], lw['g2'], lw.get('router'), _tile(t, 512))
    return res, (kf, vf, lf[:, :N_HEADS], cst[:, 2:], pst[:, 1:], sgv)


def _pos_tiles(pos, tm):
    t = pos.shape[0]
    return pos.reshape(t // tm, tm, 2).transpose(0, 2, 1).reshape(t // tm, 1, 2 * tm)


N_FFN_INPUTS = 9


def _ffn_ple_kernel(*refs, final, tm):
    ffn_in, rest = refs[:N_FFN_INPUTS], refs[N_FFN_INPUTS:]
    y_sc = rest[-1]
    _ffn_kernel(*ffn_in, y_sc)
    _ple_kernel(rest[0], y_sc, *rest[1:-1], moe=False, final=final, tm=tm)


def _back_dense(res, p_l, lw, final_g):
    h1, hn = res
    t = h1.shape[0]
    tm = _tile(t, 256)
    nt = t // tm
    final = final_g is not None
    zeros = jnp.zeros((nt,), jnp.int32)
    tabs = (jnp.arange(nt, dtype=jnp.int32), zeros, zeros, zeros + tm, zeros + 1)
    wg, wu, wd, gw, pw = lw['wg'], lw['wu'], lw['wd'], lw['gw'], lw['plw_e']
    d_ff = wg.arr.shape[-1]
    row = lambda w, *_: (w, 0)
    wmap = lambda w, *_: (wg.idx, 0, 0, 0)
    ins = [hn, wg.arr, wu.arr, wd.arr, h1, p_l.arr, gw.arr, pw.arr]
    in_specs = [pl.BlockSpec((tm, D_MODEL), row),
                pl.BlockSpec((None, 1, D_MODEL, d_ff), wmap), pl.BlockSpec((None, 1, D_MODEL, d_ff), wmap),
                pl.BlockSpec((None, 1, d_ff, D_MODEL), wmap),
                pl.BlockSpec((tm, D_MODEL), row), pl.BlockSpec((None, tm, 256), lambda w, *_: (p_l.idx, w, 0)),
                _wspec(gw), _wspec(pw)]
    out_specs = [pl.BlockSpec((tm, D_MODEL), row)]
    out_shape = [jax.ShapeDtypeStruct((t, D_MODEL), F32)]
    if final:
        ins.append(final_g)
        in_specs.append(pl.BlockSpec(final_g.shape, lambda w, *_: (0, 0)))
        out_specs.append(pl.BlockSpec((tm, D_MODEL), row))
        out_shape.append(jax.ShapeDtypeStruct((t, D_MODEL), F32))
    grid_spec = pltpu.PrefetchScalarGridSpec(
        num_scalar_prefetch=5, grid=(nt,), in_specs=in_specs, out_specs=out_specs,
        scratch_shapes=[pltpu.VMEM((tm, D_MODEL), F32)])
    return pl.pallas_call(
        functools.partial(_ffn_ple_kernel, final=final, tm=tm),
        grid_spec=grid_spec,
        out_shape=out_shape,
        compiler_params=_params(("arbitrary",)),
        name="ffn_ple",
    )(*tabs, *ins)


def _back_moe(groups, lw, final_g):
    i32 = jnp.int32
    counts = [res[3][0, :N_EXPERTS].astype(i32) for res, _ in groups]
    n_rows = 2 * sum(res[0].shape[0] for res, _ in groups)
    tf = max(c for c in (256, 128, 64, 32, 16) if n_rows % c == 0)
    tabs, starts = _work_items(sum(counts), n_rows, tf)
    xs, pos_all, offset = None, [], jnp.zeros((N_EXPERTS,), i32)
    for (res, _), cnt in zip(groups, counts):
        h1, hn, route = res[:3]
        experts = route[:, R_EXPERT:R_EXPERT + 2].astype(i32)
        pos = (starts + offset)[experts] + route[:, R_RANK:R_RANK + 2].astype(i32)
        offset = offset + cnt
        tmo = _tile(h1.shape[0], 512)
        xs = _dispatch(hn, _pos_tiles(pos, tmo), n_rows, tmo, xs)
        pos_all.append(pos)
    ys = _ffn(xs, lw['wg'], lw['wu'], lw['wd'], tabs, tf)
    outs = []
    for (res, p_l), pos in zip(groups, pos_all):
        tp = _tile(res[0].shape[0], 256)
        outs.append(_ple(res[0], ys, res[2], _pos_tiles(pos, tp), p_l, lw['gw'], lw['plw_e'], final_g, tp))
    return outs


def kernel(x_prompt, x_sample, p_prompt, p_sample, cache_k, cache_v, cache_logf, state_conv, state_pool, norm1_g, w_in, b_f, conv_dw_w, conv_dw_b, conv_ln_g, conv_ln_b, conv_pw_w, pool_w, pool_scale, sg_w, sg_b, w_o, norm2_g, ffn_w_gate, ffn_w_up, ffn_w_down, moe_router, moe_w_gate, moe_w_up, moe_w_down, ple_w, ple_gate_w, final_g):
    depth = w_in.shape[0]
    bp, sp, _ = x_prompt.shape
    bs, ss, _ = x_sample.shape
    past = cache_k.shape[2]
    hp = x_prompt.reshape(bp * sp, D_MODEL)
    hs = x_sample.reshape(bs * ss, D_MODEL)
    fg = final_g.reshape(1, D_MODEL)

    row = lambda x: x.reshape(depth, 1, -1)
    w_r = jnp.concatenate([w_in[..., 0:768], w_in[..., 772:2052],
                           jnp.pad(w_in[..., 768:772], ((0, 0), (0, 0), (0, LANES - N_HEADS)))], axis=-1).astype(BF16)
    plw = jnp.zeros((depth, GROUP_W, GROUP_W), F32)
    for gi in range(len(POOL_WINDOWS)):
        plw = plw.at[:, 64 * gi:64 * (gi + 1), 64 * gi:64 * (gi + 1)].set(pool_w[:, gi])
    stacked = {
        'g1': row(norm1_g), 'w_in': w_r, 'bf': row(jnp.pad(b_f, ((0, 0), (0, LANES - N_HEADS)))),
        'dw': jnp.pad(conv_dw_w, ((0, 0), (0, 1), (0, 0))), 'dwb': row(conv_dw_b),
        'lng': row(conv_ln_g), 'lnb': row(conv_ln_b), 'pw': conv_pw_w.astype(BF16), 'plw': plw.astype(BF16),
        'psc': row(pool_scale), 'w_o': w_o.astype(BF16), 'g2': row(norm2_g),
        'gw': ple_gate_w.astype(BF16), 'plw_e': ple_w.astype(BF16),
    }
    dense = {'wg': ffn_w_gate.astype(BF16)[:, None], 'wu': ffn_w_up.astype(BF16)[:, None],
             'wd': ffn_w_down.astype(BF16)[:, None]}
    sparse = {'wg': moe_w_gate.astype(BF16), 'wu': moe_w_up.astype(BF16), 'wd': moe_w_down.astype(BF16),
              'router': jnp.pad(moe_router, ((0, 0), (0, 0), (0, LANES - N_EXPERTS))).astype(BF16)}

    def sg_params(tc):
        return {'sgw': sg_w[:, :, :tc, :tc],
                'sgb': jnp.repeat(jnp.swapaxes(sg_b[:, :, :tc], 1, 2), GROUP_W // SG_GROUPS, axis=2)}

    sg_p, sg_s = sg_params(min(sp, 128)), sg_params(min(ss, 128))
    pp_all = p_prompt.reshape(depth, bp * sp, -1)
    ps_all = p_sample.reshape(depth, bs * ss, -1)
    ck_all = cache_k.reshape(depth, bs, past, 256).astype(BF16)
    cv_all = cache_v.reshape(depth, bs, past, 256).astype(BF16)
    clf_all = jnp.pad(cache_logf, ((0, 0), (0, 0), (0, 0), (0, LANES - N_HEADS))).reshape(depth, bs * past, LANES)
    sc_all = jnp.pad(state_conv, ((0, 0), (0, 0), (2, 0), (0, 0)))
    sp_all = jnp.pad(state_pool, ((0, 0), (0, 0), (1, 0), (0, 0)))
    cz, pz = jnp.zeros((bp, 32, GROUP_W), F32), jnp.zeros((bp, 16, GROUP_W), F32)

    outs_p, outs_s = [], []
    for l in range(depth):
        ffn_w = dense if l % 2 == 0 else sparse

        def layer_weights(sg):
            lw = {n: _LW(a, l) for n, a in {**stacked, **sg}.items()}
            lw.update({n: _LW(a, l // 2) for n, a in ffn_w.items()})
            return lw

        last = fg if l == depth - 1 else None
        lwp, lws = layer_weights(sg_p), layer_weights(sg_s)
        res_p, outs = _front(hp, lwp, bp, sp, None, cz, pz)
        outs_p.append(outs)
        res_s, outs = _front(hs, lws, bs, ss, (ck_all[l], cv_all[l], clf_all[l]), sc_all[l], sp_all[l])
        outs_s.append(outs)
        if l % 2 == 0:
            hp_res = _back_dense(res_p, _LW(pp_all, l), lwp, last)
            hs_res = _back_dense(res_s, _LW(ps_all, l), lws, last)
        else:
            hp_res, hs_res = _back_moe([(res_p, _LW(pp_all, l)), (res_s, _LW(ps_all, l))], lwp, last)
        if last is None:
            hp, hs = hp_res[0], hs_res[0]
        else:
            y_prompt, y_sample = hp_res[1], hs_res[1]

    def stack(outs, i, shape):
        return jnp.stack([o[i] for o in outs]).reshape((depth,) + shape)

    def stack_kv(i):
        if outs_p[0][i].shape == (bp * sp, GROUP_W):
            return stack(outs_p, i, (bp, sp, N_HEADS, HEAD_DIM))
        kv = stack(outs_p, i, (bp, N_HEADS, HEAD_DIM, sp))
        return jnp.transpose(kv, (0, 1, 4, 2, 3))

    return (y_prompt.reshape(bp, sp, D_MODEL), y_sample.reshape(bs, ss, D_MODEL),
            stack_kv(0), stack_kv(1),
            stack(outs_p, 2, (bp, sp, N_HEADS)), stack(outs_p, 3, (bp, CONV_W - 1, GROUP_W)),
            stack(outs_p, 4, (bp, POOL_HIST, GROUP_W)),
            stack(outs_s, 0, (bs, ss, N_HEADS, HEAD_DIM)), stack(outs_s, 1, (bs, ss, N_HEADS, HEAD_DIM)),
            stack(outs_s, 2, (bs, ss, N_HEADS)), stack(outs_s, 3, (bs, CONV_W - 1, GROUP_W)),
            stack(outs_s, 4, (bs, POOL_HIST, GROUP_W)), stack(outs_s, 5, (bs, ss, GROUP_W)))
```

```python
import functools
from typing import NamedTuple

import numpy as np
import jax
import jax.numpy as jnp
from jax import lax
from jax.experimental import pallas as pl
from jax.experimental.pallas import tpu as pltpu

F32 = jnp.float32
BF16 = jnp.bfloat16
EPS = 1e-6

D_MODEL = 1024
GROUP_W = 256
N_HEADS = 4
HEAD_DIM = 64
CONV_W = 31
POOL_WINDOWS = (2, 4, 8, 16)
POOL_HIST = 15
SG_GROUPS = 4
N_EXPERTS = 8
LANES = 128
AUG_PER_HEAD = 6
VMEM_LIMIT = 56 * 1024 * 1024


def _params(sem, vmem=VMEM_LIMIT):
    return pltpu.CompilerParams(dimension_semantics=sem, vmem_limit_bytes=vmem)


def _dot(a, b):
    return jnp.dot(a, b, preferred_element_type=F32)


class _LW(NamedTuple):
    arr: jax.Array
    idx: int


def _wspec(w):
    nd = w.arr.ndim - 1
    return pl.BlockSpec((None,) + w.arr.shape[1:], lambda *_: (w.idx,) + (0,) * nd)


def _split3(x):
    hi = x.astype(BF16)
    r1 = x - hi.astype(F32)
    mid = r1.astype(BF16)
    lo = (r1 - mid.astype(F32)).astype(BF16)
    return jnp.concatenate([hi, mid, lo], axis=1)


def _running_sum(x, tril, carry_ref):
    cs = _dot(tril, _split3(x))
    c = carry_ref[...] + (cs[:, :LANES] + cs[:, LANES:2 * LANES] + cs[:, 2 * LANES:])
    n = x.shape[0]
    carry_ref[...] = c[n - 1:n, :]
    return c


def _placement(sign_k):
    pq = np.zeros((3 * LANES, 2 * LANES), np.float32)
    pk = np.zeros((3 * LANES, 2 * LANES), np.float32)
    oq = np.zeros((1, 2 * LANES), np.float32)
    ok = np.zeros((1, 2 * LANES), np.float32)
    for h in range(N_HEADS):
        p, j = divmod(h, 2)
        for i in range(3):
            pq[i * LANES + h, p * LANES + j * AUG_PER_HEAD + i] = 1.0
            oq[0, p * LANES + j * AUG_PER_HEAD + 3 + i] = 1.0
            pk[i * LANES + h, p * LANES + j * AUG_PER_HEAD + 3 + i] = sign_k
            ok[0, p * LANES + j * AUG_PER_HEAD + i] = 1.0
    return pq, pk, oq, ok


def _in_proj_kernel(h_ref, g_ref, w_ref, bf_ref, c0_ref, tril_ref, pq_ref, pk_ref, oq_ref, ok_ref,
                    qs_ref, kf_ref, vf_ref, kb_ref, vb_ref, lf_ref, aq_ref, ak_ref, zc_ref, carry_ref,
                    kv_transposed=False):
    @pl.when(pl.program_id(1) == 0)
    def _():
        carry_ref[...] = c0_ref[0]

    x = h_ref[...]
    hn = x * lax.rsqrt(jnp.mean(x * x, axis=-1, keepdims=True) + EPS) * g_ref[...]
    hb = hn.astype(BF16)
    qs_ref[...] = (_dot(hb, w_ref[:, 0:256]) * (HEAD_DIM ** -0.5)).astype(BF16)
    k = _dot(hb, w_ref[:, 256:512])
    kf_ref[...] = k.T if kv_transposed else k
    kb_ref[...] = k.astype(BF16)
    v = _dot(hb, w_ref[:, 512:768])
    vf_ref[...] = v.T if kv_transposed else v
    vb_ref[...] = v.astype(BF16)
    zc_ref[...] = _dot(hb, w_ref[:, 768:2048])
    zf = _dot(hb, w_ref[:, 2048:2176]) + bf_ref[...]
    lf = jnp.minimum(zf, 0.0) - jnp.log1p(jnp.exp(-jnp.abs(zf)))
    lane = lax.broadcasted_iota(jnp.int32, lf.shape, 1)
    lf = jnp.where(lane < N_HEADS, lf, 0.0)
    lf_ref[...] = lf
    cp = _split3(_running_sum(lf, tril_ref[...], carry_ref))
    aq_ref[...] = (_dot(cp, pq_ref[...]) + oq_ref[...]).astype(BF16)
    ak_ref[...] = (_dot(cp, pk_ref[...]) + ok_ref[...]).astype(BF16)


N_IN_PROJ_INPUTS, N_IN_PROJ_OUTPUTS, N_MIX_INPUTS = 10, 8, 11


def _in_mix_kernel(*refs, ts, tc, pos0, want_sgv, kv_transposed):
    proj_in = refs[:N_IN_PROJ_INPUTS]
    mix_in = refs[N_IN_PROJ_INPUTS:N_IN_PROJ_INPUTS + N_MIX_INPUTS]
    rest = refs[N_IN_PROJ_INPUTS + N_MIX_INPUTS:]
    proj_out, rest = rest[:N_IN_PROJ_OUTPUTS], rest[N_IN_PROJ_OUTPUTS:]
    bcd_ref, cst_ref, pst_ref = rest[:3]
    sgv_ref = rest[3] if want_sgv else None
    carry_ref, zc_sc, xc_ref, xp_ref = rest[3 + want_sgv:]
    _in_proj_kernel(*proj_in, *proj_out, zc_sc, carry_ref, kv_transposed=kv_transposed)
    z = [zc_sc.at[:, j * GROUP_W:(j + 1) * GROUP_W] for j in range(5)]
    _mixers_kernel(*z, *mix_in, bcd_ref, cst_ref, pst_ref, sgv_ref, xc_ref, xp_ref, ts=ts, tc=tc, pos0=pos0)


def _in_mix(h, lw, c0, cprev, pprev, nb, seq, ts, tc, pos0, want_sgv, kv_transposed):
    t = nb * seq
    ns = seq // ts
    tril = jnp.asarray(np.tril(np.ones((ts, ts), np.float32)), BF16)
    pq, pk, oq, ok = _placement(-1.0)
    row = lambda b, s: (b * ns + s, 0)
    const = lambda b, s: (0, 0)
    blk = lambda w_: pl.BlockSpec((ts, w_), row)
    full = lambda a: pl.BlockSpec(a.shape, const)
    perb = lambda r: pl.BlockSpec((1, r, GROUP_W), lambda b, s: (b, 0, 0))
    consts = [jnp.asarray(pq, BF16), jnp.asarray(pk, BF16), jnp.asarray(oq), jnp.asarray(ok)]
    small = [lw[n] for n in ('dw', 'dwb', 'lng', 'lnb', 'pw', 'plw', 'psc', 'sgw', 'sgb')]
    outs = [(256, BF16), (256, F32), (256, F32), (256, BF16), (256, BF16), (LANES, F32), (256, BF16), (256, BF16),
            (768, BF16)]
    out_specs = [blk(w_) for w_, _ in outs] + [perb(32), perb(16)]
    out_shape = [jax.ShapeDtypeStruct((t, w_), dt) for w_, dt in outs] + [
        jax.ShapeDtypeStruct((nb, 32, GROUP_W), F32), jax.ShapeDtypeStruct((nb, 16, GROUP_W), F32)]
    if kv_transposed:
        for i in (1, 2):
            out_specs[i] = pl.BlockSpec((GROUP_W, ts), lambda b, s: (b, s))
            out_shape[i] = jax.ShapeDtypeStruct((nb * GROUP_W, seq), F32)
    if want_sgv:
        out_specs.append(blk(GROUP_W))
        out_shape.append(jax.ShapeDtypeStruct((t, GROUP_W), F32))
    return pl.pallas_call(
        functools.partial(_in_mix_kernel, ts=ts, tc=tc, pos0=pos0, want_sgv=want_sgv, kv_transposed=kv_transposed),
        grid=(nb, ns),
        in_specs=[blk(D_MODEL), _wspec(lw['g1']), _wspec(lw['w_in']), _wspec(lw['bf']),
                  pl.BlockSpec((1, 1, LANES), lambda b, s: (b, 0, 0)), full(tril)] + [full(a) for a in consts]
                 + [perb(32), perb(16)] + [_wspec(a) for a in small],
        out_specs=out_specs,
        out_shape=out_shape,
        scratch_shapes=[pltpu.VMEM((1, LANES), F32), pltpu.VMEM((ts, 5 * GROUP_W), F32),
                        pltpu.VMEM((40 + ts, GROUP_W), F32), pltpu.VMEM((16 + ts, GROUP_W), F32)],
        compiler_params=_params(("arbitrary", "arbitrary")),
        name="in_mix",
    )(h, lw['g1'].arr, lw['w_in'].arr, lw['bf'].arr, c0, tril, *consts, cprev, pprev, *[a.arr for a in small])


def _cache_sum_kernel(lf_ref, tril_ref, pk_ref, ok_ref, ak_ref, tot_ref, carry_ref):
    @pl.when(pl.program_id(1) == 0)
    def _():
        carry_ref[...] = jnp.zeros_like(carry_ref)

    cp = _split3(_running_sum(lf_ref[...], tril_ref[...], carry_ref))
    ak_ref[...] = (_dot(cp, pk_ref[...]) + ok_ref[...]).T.astype(BF16)
    tot_ref[0] = carry_ref[...]


def _cache_sum(lfp, nb, past, tm):
    ns = past // tm
    tril = jnp.asarray(np.tril(np.ones((tm, tm), np.float32)), BF16)
    _, pk, _, ok = _placement(-1.0)
    pk, ok = jnp.asarray(pk, BF16), jnp.asarray(ok)
    const = lambda b, s: (0, 0)
    return pl.pallas_call(
        _cache_sum_kernel,
        grid=(nb, ns),
        in_specs=[pl.BlockSpec((tm, LANES), lambda b, s: (b * ns + s, 0)), pl.BlockSpec(tril.shape, const),
                  pl.BlockSpec(pk.shape, const), pl.BlockSpec(ok.shape, const)],
        out_specs=[pl.BlockSpec((256, tm), lambda b, s: (b, s)),
                   pl.BlockSpec((1, 1, LANES), lambda b, s: (b, 0, 0))],
        out_shape=[jax.ShapeDtypeStruct((nb * 256, past), BF16), jax.ShapeDtypeStruct((nb, 1, LANES), F32)],
        scratch_shapes=[pltpu.VMEM((1, LANES), F32)],
        compiler_params=_params(("arbitrary", "arbitrary")),
        name="cache_sum",
    )(lfp, tril, pk, ok)


NEG_BIG = -1e30


def _head_lhs(q, aq, j, lane):
    zero = jnp.zeros((), BF16)
    qm = jnp.where((lane >= HEAD_DIM * j) & (lane < HEAD_DIM * (j + 1)), q, zero)
    am = jnp.where((lane >= AUG_PER_HEAD * j) & (lane < AUG_PER_HEAD * (j + 1)), aq, zero)
    return jnp.concatenate([qm, am], axis=1)


def _scores(lhs, k, ak):
    rhs = jnp.concatenate([k, ak], axis=1)
    return lax.dot_general(lhs, rhs, (((1,), (1,)), ((), ())), preferred_element_type=F32)


def _loop_by_two(n, fn):
    def body(i, carry):
        fn(2 * i)
        fn(2 * i + 1)
        return carry

    lax.fori_loop(0, lax.shift_right_logical(n, 1), body, 0)

    @pl.when((n & 1) == 1)
    def _():
        fn(n - 1)


def _attn_kernel(q_ref, aq_ref, k_ref, ak_ref, v_ref, o_ref, m_sc, l_sc, acc_sc, e_sc, f_sc, *, t):
    qi = pl.program_id(1)
    lane = lax.broadcasted_iota(jnp.int32, (1, LANES), 1)
    row = lax.broadcasted_iota(jnp.int32, (t, t), 0)
    col = lax.broadcasted_iota(jnp.int32, (t, t), 1)
    nct = t // LANES
    for p in range(2):
        cs = slice(p * LANES, (p + 1) * LANES)
        lhs = [_head_lhs(q_ref[:, cs], aq_ref[:, cs], j, lane) for j in range(2)]

        def scores(j, ki, masked):
            rows = pl.ds(pl.multiple_of(ki * t, t), t)
            s = _scores(lhs[j], k_ref[rows, cs], ak_ref[rows, cs])
            if masked:
                s = jnp.where(col <= row, s, -jnp.inf)
            return [s[:, c * LANES:(c + 1) * LANES] for c in range(nct)]

        def stats(ki, masked):
            for j in range(2):
                parts = scores(j, ki, masked)
                m_old = m_sc[j]
                m_new = functools.reduce(jnp.maximum, parts, m_old)
                tot = l_sc[j] * jnp.exp(m_old - m_new)
                for c, part in enumerate(parts):
                    e = jnp.exp(part - m_new)
                    e_sc[j, ki, :, c * LANES:(c + 1) * LANES] = e
                    tot = tot + e
                l_sc[j] = tot
                m_sc[j] = m_new
                f_sc[j, ki] = m_new

        m_sc[...] = jnp.full_like(m_sc, NEG_BIG)
        l_sc[...] = jnp.zeros_like(l_sc)
        _loop_by_two(qi, lambda ki: stats(ki, False))
        stats(qi, True)
        for j in range(2):
            m_lane = m_sc[j]
            m_row = jnp.max(m_lane, axis=1, keepdims=True)
            l_row = jnp.sum(l_sc[j] * jnp.exp(m_lane - m_row), axis=1, keepdims=True)
            m_sc[j] = jnp.broadcast_to(m_row, m_lane.shape)
            l_sc[j] = jnp.broadcast_to(1.0 / l_row, m_lane.shape)

        def rescale(ki, carry):
            for j in range(2):
                f_sc[j, ki] = jnp.exp(f_sc[j, ki] - m_sc[j]) * l_sc[j]
            return carry

        lax.fori_loop(0, qi + 1, rescale, 0)
        acc_sc[...] = jnp.zeros_like(acc_sc)

        def accumulate(ki):
            v = v_ref[pl.ds(pl.multiple_of(ki * t, t), t), cs]
            for j in range(2):
                f = f_sc[j, ki]
                prob = jnp.concatenate([(e_sc[j, ki, :, c * LANES:(c + 1) * LANES] * f).astype(BF16)
                                        for c in range(nct)], axis=1)
                acc_sc[j] += _dot(prob, v)

        _loop_by_two(qi + 1, accumulate)
        o_ref[:, cs] = jnp.where(lane < HEAD_DIM, acc_sc[0], acc_sc[1]).astype(BF16)


N_ATTN_INPUTS, N_ATTN_SCRATCH = 5, 5


def _attn_out_kernel(*refs, t, moe):
    attn_in, rest = refs[:N_ATTN_INPUTS], refs[N_ATTN_INPUTS:]
    n_proj = 4 + 2 * moe + 2 + 2 * moe
    proj, rest = rest[:n_proj], rest[n_proj:]
    attn_sc, a_sc, carry = rest[:N_ATTN_SCRATCH], rest[N_ATTN_SCRATCH], rest[N_ATTN_SCRATCH + 1:]
    _attn_kernel(*attn_in, a_sc, *attn_sc, t=t)
    first = (pl.program_id(0) == 0) & (pl.program_id(1) == 0)
    _out_proj_kernel(a_sc, *proj, *carry, moe=moe, first_step=first)


def _attn_out(qs, aq, kb, ak, vb, bcd, h, wo, g2, router, nb, seq, t):
    assert seq % t == 0 and t % LANES == 0
    nq = seq // t
    tt = nb * seq
    moe = router is not None
    row = lambda b, i: (b * nq + i, 0)
    const = lambda b, i: (0, 0)
    qblk = pl.BlockSpec((t, 256), row)
    kblk = pl.BlockSpec((seq, 256), lambda b, i: (b, 0))
    ins = [qs, aq, kb, ak, vb, bcd, h, wo.arr, g2.arr]
    in_specs = [qblk, qblk, kblk, kblk, kblk, pl.BlockSpec((t, 768), row), pl.BlockSpec((t, D_MODEL), row),
                _wspec(wo), _wspec(g2)]
    out_specs = [pl.BlockSpec((t, D_MODEL), row), pl.BlockSpec((t, D_MODEL), row)]
    out_shape = [jax.ShapeDtypeStruct((tt, D_MODEL), F32), jax.ShapeDtypeStruct((tt, D_MODEL), F32 if moe else BF16)]
    scratch = [pltpu.VMEM((2, t, LANES), F32)] * 3 + [pltpu.VMEM((2, nq, t, t), F32),
                                                      pltpu.VMEM((2, nq, t, LANES), F32), pltpu.VMEM((t, 256), BF16)]
    if moe:
        stril = jnp.asarray(np.tril(np.ones((t, t), np.float32), -1), BF16)
        ins += [router.arr, stril]
        in_specs += [_wspec(router), pl.BlockSpec(stril.shape, const)]
        out_specs += [pl.BlockSpec((t, LANES), row), pl.BlockSpec((1, LANES), const)]
        out_shape += [jax.ShapeDtypeStruct((tt, LANES), F32), jax.ShapeDtypeStruct((1, LANES), F32)]
        scratch.append(pltpu.VMEM((1, LANES), F32))
    return pl.pallas_call(
        functools.partial(_attn_out_kernel, t=t, moe=moe),
        grid=(nb, nq),
        in_specs=in_specs, out_specs=out_specs, out_shape=out_shape, scratch_shapes=scratch,
        compiler_params=_params(("arbitrary", "arbitrary")),
        name="attn_out",
    )(*ins)


def _attn_rows_kernel(q_ref, aq_ref, k_ref, ak_ref, v_ref, ckt_ref, akct_ref, cvt_ref, o_ref):
    lq = q_ref.shape[0]
    lane = lax.broadcasted_iota(jnp.int32, (1, LANES), 1)
    row = lax.broadcasted_iota(jnp.int32, (lq, lq), 0)
    col = lax.broadcasted_iota(jnp.int32, (lq, lq), 1)
    nt = (((1,), (1,)), ((), ()))
    for p in range(2):
        cs = slice(p * LANES, (p + 1) * LANES)
        cache_rhs = jnp.concatenate([ckt_ref[cs, :].astype(BF16), akct_ref[cs, :]], axis=0)
        cache_v = cvt_ref[cs, :].astype(BF16)
        outs = []
        for j in range(2):
            lhs = _head_lhs(q_ref[:, cs], aq_ref[:, cs], j, lane)
            s_c = _dot(lhs, cache_rhs)
            s_n = jnp.where(col <= row, _scores(lhs, k_ref[:, cs], ak_ref[:, cs]), -jnp.inf)
            m = jnp.maximum(jnp.max(s_c, axis=1, keepdims=True), jnp.max(s_n, axis=1, keepdims=True))
            e_c, e_n = jnp.exp(s_c - m), jnp.exp(s_n - m)
            r = 1.0 / (jnp.sum(e_c, axis=1, keepdims=True) + jnp.sum(e_n, axis=1, keepdims=True))
            outs.append(lax.dot_general((e_c * r).astype(BF16), cache_v, nt, preferred_element_type=F32)
                        + _dot((e_n * r).astype(BF16), v_ref[:, cs]))
        o_ref[:, cs] = jnp.where(lane < HEAD_DIM, outs[0], outs[1]).astype(BF16)


def _attention_rows(qs, aq, kb, ak, vb, ckt, akct, cvt, nb, lq, past):
    qblk = pl.BlockSpec((lq, 256), lambda b: (b, 0))
    cblk = pl.BlockSpec((256, past), lambda b: (b, 0))
    lblk = lambda w: pl.BlockSpec((None, 256, past), lambda b: (w.idx, b, 0))
    return pl.pallas_call(
        _attn_rows_kernel,
        grid=(nb,),
        in_specs=[qblk] * 5 + [lblk(ckt), cblk, lblk(cvt)],
        out_specs=qblk,
        out_shape=jax.ShapeDtypeStruct((nb * lq, 256), BF16),
        compiler_params=_params(("arbitrary",)),
        name="attn_rows",
    )(qs, aq, kb, ak, vb, ckt.arr, akct, cvt.arr)


ROW_CHUNK = 64


def _gelu(x):
    return 0.5 * x * (1.0 + jnp.tanh(0.7978845608028654 * (x + 0.044715 * (x * x * x))))


def _mixers_kernel(ca_ref, cg_ref, px_ref, su_ref, sv_ref, cprev_ref, pprev_ref, dw_ref, dwb_ref, lng_ref, lnb_ref,
                   pw_ref, plw_ref, psc_ref, sgw_ref, sgb_ref,
                   bcd_ref, cst_ref, pst_ref, sgv_ref, xc_ref, xp_ref, *, ts, tc, pos0):
    s = pl.program_id(1)

    @pl.when(s == 0)
    def _():
        xc_ref[0:32, :] = cprev_ref[0].astype(BF16).astype(F32)
        xp_ref[0:16, :] = pprev_ref[0]

    @pl.when(s > 0)
    def _():
        xc_ref[0:32, :] = xc_ref[ts:ts + 32, :]
        xp_ref[0:16, :] = xp_ref[ts:ts + 16, :]

    g = cg_ref[...]
    u = ca_ref[...] * (1.0 / (1.0 + jnp.exp(-g)))
    xc_ref[32:32 + ts, :] = u.astype(BF16).astype(F32)
    xc_ref[32 + ts:40 + ts, :] = jnp.zeros((8, GROUP_W), F32)
    cst_ref[0] = u[ts - 32:ts, :]
    xp_ref[16:16 + ts, :] = px_ref[...]
    pst_ref[0] = xp_ref[ts:ts + 16, :]

    rc = min(ts, ROW_CHUNK)
    lane = lax.broadcasted_iota(jnp.int32, (rc, GROUP_W), 1)
    conv_act, pool_dlt = [], []
    for r0 in range(0, ts, rc):
        y = jnp.zeros((rc, GROUP_W), F32)
        for r in range(8):
            part = jnp.zeros((rc + 8, GROUP_W), F32)
            for a in range(r if r >= 2 else r + 8, CONV_W + 2, 8):
                part = part + xc_ref[r0 + a - r:r0 + a - r + rc + 8, :] * dw_ref[a - 2:a - 1, :]
            y = y + part[r:r + rc, :]
        y = y + dwb_ref[...]
        mu = jnp.mean(y, axis=-1, keepdims=True)
        var = jnp.mean(jnp.square(y - mu), axis=-1, keepdims=True)
        y = (y - mu) * lax.rsqrt(var + EPS) * lng_ref[...] + lnb_ref[...]
        conv_act.append((y * (1.0 / (1.0 + jnp.exp(-y)))).astype(BF16))

        x = xp_ref[16 + r0:16 + r0 + rc, :]
        pos = (pos0 + s * ts + r0 + lax.broadcasted_iota(jnp.int32, (rc, 1), 0)).astype(F32)
        acc = x
        mean = jnp.zeros((rc, GROUP_W), F32)
        wi = 0
        for i in range(1, POOL_WINDOWS[-1] + 1):
            if i > 1:
                acc = acc + xp_ref[16 - (i - 1) + r0:16 - (i - 1) + r0 + rc, :]
            if i == POOL_WINDOWS[wi]:
                cnt = jnp.minimum(pos + 1.0, float(i))
                mean = jnp.where((lane >= 64 * wi) & (lane < 64 * (wi + 1)), acc / cnt, mean)
                wi += 1
        pool_dlt.append((mean - x).astype(BF16))
    bcd_ref[:, 0:256] = _dot(jnp.concatenate(conv_act, axis=0), pw_ref[...]).astype(BF16)
    bcd_ref[:, 256:512] = (_dot(jnp.concatenate(pool_dlt, axis=0), plw_ref[...]) * psc_ref[...]).astype(BF16)

    r = lax.broadcasted_iota(jnp.int32, (tc, tc), 0)
    c = lax.broadcasted_iota(jnp.int32, (tc, tc), 1)
    lane_c = lax.broadcasted_iota(jnp.int32, (tc, GROUP_W), 1)
    wts = [jnp.where(c <= r, sgw_ref[gi], 0.0).astype(BF16) for gi in range(SG_GROUPS)]
    for ci in range(ts // tc):
        uu = _gelu(su_ref[ci * tc:(ci + 1) * tc, :])
        vv = _gelu(sv_ref[ci * tc:(ci + 1) * tc, :])
        if sgv_ref is not None:
            sgv_ref[ci * tc:(ci + 1) * tc, :] = vv
        m = sgb_ref[...]
        for gi in range(SG_GROUPS):
            vg = jnp.where((lane_c >= 64 * gi) & (lane_c < 64 * (gi + 1)), vv, 0.0).astype(BF16)
            m = m + _dot(wts[gi], vg)
        bcd_ref[ci * tc:(ci + 1) * tc, 512:768] = (uu * m).astype(BF16)


R_GATE, R_EXPERT, R_RANK = 0, 2, 4


def _out_proj_kernel(a_ref, bcd_ref, h_ref, wo_ref, g_ref, *rest, moe, first_step=None):
    if first_step is None:
        first_step = pl.program_id(0) == 0
    if moe:
        router_ref, stril_ref, h1_ref, hn_ref, route_ref, cnt_ref, carry_ref = rest
    else:
        h1_ref, hn_ref = rest
    mix = _dot(a_ref[...], wo_ref[0:256, :]) + _dot(bcd_ref[...], wo_ref[256:1024, :])
    h1 = h_ref[...] + mix
    hn = h1 * lax.rsqrt(jnp.mean(h1 * h1, axis=-1, keepdims=True) + EPS) * g_ref[...]
    h1_ref[...] = h1
    hn_ref[...] = hn.astype(hn_ref.dtype)
    if moe:
        @pl.when(first_step)
        def _():
            carry_ref[...] = jnp.zeros_like(carry_ref)

        logits = _dot(hn.astype(BF16), router_ref[...])
        lane = lax.broadcasted_iota(jnp.int32, logits.shape, 1)
        logits = jnp.where(lane < N_EXPERTS, logits, -jnp.inf)
        v1 = jnp.max(logits, axis=1, keepdims=True)
        i1 = jnp.min(jnp.where(logits == v1, lane, LANES), axis=1, keepdims=True)
        rest_l = jnp.where(lane == i1, -jnp.inf, logits)
        v2 = jnp.max(rest_l, axis=1, keepdims=True)
        i2 = jnp.min(jnp.where(rest_l == v2, lane, LANES), axis=1, keepdims=True)
        e = jnp.exp(v2 - v1)
        g1 = 1.0 / (1.0 + e)
        oh1, oh2 = lane == i1, lane == i2
        oh = jnp.where(oh1 | oh2, 1.0, 0.0)
        before = _dot(stril_ref[...], oh.astype(BF16)) + carry_ref[...]
        r1 = jnp.sum(jnp.where(oh1, before, 0.0), axis=1, keepdims=True)
        r2 = jnp.sum(jnp.where(oh2, before, 0.0), axis=1, keepdims=True)
        carry_ref[...] += jnp.sum(oh, axis=0, keepdims=True)
        cnt_ref[...] = carry_ref[...]
        rec = jnp.zeros_like(logits)
        for ln, val in ((R_GATE, g1), (R_GATE + 1, e * g1), (R_EXPERT, i1.astype(F32)), (R_EXPERT + 1, i2.astype(F32)),
                        (R_RANK, r1), (R_RANK + 1, r2)):
            rec = jnp.where(lane == ln, val, rec)
        route_ref[...] = rec


def _out_proj(a, bcd, h, wo, g2, router, tm):
    t = h.shape[0]
    moe = router is not None
    row = lambda i: (i, 0)
    const = lambda i: (0, 0)
    ins = [a, bcd, h, wo.arr, g2.arr]
    in_specs = [pl.BlockSpec((tm, 256), row), pl.BlockSpec((tm, 768), row), pl.BlockSpec((tm, D_MODEL), row),
                _wspec(wo), _wspec(g2)]
    out_specs = [pl.BlockSpec((tm, D_MODEL), row), pl.BlockSpec((tm, D_MODEL), row)]
    out_shape = [jax.ShapeDtypeStruct((t, D_MODEL), F32), jax.ShapeDtypeStruct((t, D_MODEL), F32 if moe else BF16)]
    scratch = []
    if moe:
        stril = jnp.asarray(np.tril(np.ones((tm, tm), np.float32), -1), BF16)
        ins += [router.arr, stril]
        in_specs += [_wspec(router), pl.BlockSpec(stril.shape, const)]
        out_specs += [pl.BlockSpec((tm, LANES), row), pl.BlockSpec((1, LANES), const)]
        out_shape += [jax.ShapeDtypeStruct((t, LANES), F32), jax.ShapeDtypeStruct((1, LANES), F32)]
        scratch = [pltpu.VMEM((1, LANES), F32)]
    return pl.pallas_call(
        functools.partial(_out_proj_kernel, moe=moe),
        grid=(t // tm,),
        in_specs=in_specs, out_specs=out_specs, out_shape=out_shape, scratch_shapes=scratch,
        compiler_params=_params(("arbitrary",)),
        name="out_proj",
    )(*ins)


DMA_UNROLL = 8


def _row_copy(src_ref, src_row, dst_ref, dst_row, sem):
    return pltpu.make_async_copy(src_ref.at[pl.ds(src_row, 1)], dst_ref.at[pl.ds(dst_row, 1)], sem)


def _dispatch_kernel(pos_ref, x_ref, *rest, tm):
    xs_ref, sem = rest[-2:]

    def issue(r, carry):
        for k in range(2):
            _row_copy(x_ref, r, xs_ref, pos_ref[0, 0, k * tm + r], sem).start(priority=k)
        return carry

    def drain(r, carry):
        for k in range(2):
            _row_copy(x_ref, r, xs_ref, pos_ref[0, 0, k * tm + r], sem).wait()
        return carry

    lax.fori_loop(0, tm, issue, 0, unroll=DMA_UNROLL)
    lax.fori_loop(0, tm, drain, 0, unroll=DMA_UNROLL)


def _dispatch(hn, pos_tiles, n_rows, tm, xs_prev=None):
    t = hn.shape[0]
    ins = [pos_tiles, hn]
    in_specs = [pl.BlockSpec((1, 1, 2 * tm), lambda i: (i, 0, 0), memory_space=pltpu.SMEM),
                pl.BlockSpec((tm, D_MODEL), lambda i: (i, 0))]
    aliases = {}
    if xs_prev is not None:
        ins.append(xs_prev)
        in_specs.append(pl.BlockSpec(memory_space=pl.ANY))
        aliases = {2: 0}
    return pl.pallas_call(
        functools.partial(_dispatch_kernel, tm=tm),
        grid=(t // tm,),
        in_specs=in_specs,
        out_specs=pl.BlockSpec(memory_space=pl.ANY),
        out_shape=jax.ShapeDtypeStruct((n_rows, D_MODEL), F32),
        scratch_shapes=[pltpu.SemaphoreType.DMA],
        input_output_aliases=aliases,
        compiler_params=_params(("arbitrary",)),
        name="dispatch",
    )(*ins)


def _ffn_kernel(wt_ref, we_ref, lo_ref, hi_ref, first_ref, x_ref, wg_ref, wu_ref, wd_ref, y_ref):
    w = pl.program_id(0)
    lo, hi = lo_ref[w], hi_ref[w]

    @pl.when(hi > lo)
    def _():
        rowi = lax.broadcasted_iota(jnp.int32, (x_ref.shape[0], 1), 0)
        x = jnp.where((rowi >= lo) & (rowi < hi), x_ref[...], jnp.zeros((), x_ref.dtype)).astype(BF16)
        gate = _dot(x, wg_ref[0])
        up = _dot(x, wu_ref[0])
        act = (gate * (1.0 / (1.0 + jnp.exp(-gate))) * up).astype(BF16)
        y = _dot(act, wd_ref[0])

        @pl.when(first_ref[w] == 1)
        def _():
            y_ref[...] = y

        @pl.when(first_ref[w] == 0)
        def _():
            y_ref[...] += y


def _ffn(xs, wg, wu, wd, tabs, tm):
    d_ff = wg.arr.shape[-1]
    xmap = lambda w, wt, we, lo, hi, fi: (wt[w], 0)
    wmap = lambda w, wt, we, lo, hi, fi: (wg.idx, we[w], 0, 0)
    grid_spec = pltpu.PrefetchScalarGridSpec(
        num_scalar_prefetch=5,
        grid=(tabs[0].shape[0],),
        in_specs=[pl.BlockSpec((tm, D_MODEL), xmap),
                  pl.BlockSpec((None, 1, D_MODEL, d_ff), wmap), pl.BlockSpec((None, 1, D_MODEL, d_ff), wmap),
                  pl.BlockSpec((None, 1, d_ff, D_MODEL), wmap)],
        out_specs=pl.BlockSpec((tm, D_MODEL), xmap),
    )
    return pl.pallas_call(
        _ffn_kernel,
        grid_spec=grid_spec,
        out_shape=jax.ShapeDtypeStruct((xs.shape[0], D_MODEL), F32),
        compiler_params=_params(("arbitrary",)),
        name="ffn",
    )(*tabs, xs, wg.arr, wu.arr, wd.arr)


def _work_items(counts, n_rows, tm):
    i32 = jnp.int32
    ends = jnp.cumsum(counts)
    starts = ends - counts
    t_first = starts // tm
    n_e = jnp.where(counts > 0, (ends - 1) // tm - t_first + 1, 0)
    w_end = jnp.cumsum(n_e)
    w_start = w_end - n_e
    n_work = n_rows // tm + counts.shape[0] - 1
    w = jnp.arange(n_work, dtype=i32)
    valid = w < w_end[-1]
    e = jnp.minimum(jnp.sum((w[:, None] >= w_end[None, :]).astype(i32), axis=1), counts.shape[0] - 1)
    tile = jnp.where(valid, t_first[e] + (w - w_start[e]), n_rows // tm - 1).astype(i32)
    lo = jnp.where(valid, jnp.maximum(starts[e] - tile * tm, 0), 0).astype(i32)
    hi = jnp.where(valid, jnp.minimum(ends[e] - tile * tm, tm), 0).astype(i32)
    e = jnp.where(valid, e, e[jnp.maximum(w_end[-1] - 1, 0)]).astype(i32)
    first = jnp.concatenate([jnp.ones((1,), i32), (tile[1:] != tile[:-1]).astype(i32)])
    return (tile, e, lo, hi, first), starts


def _ple_kernel(*refs, moe, final, tm):
    refs = list(refs)
    if moe:
        pos_ref, pos_next_ref = refs.pop(0), refs.pop(0)
    h1_ref, y_ref = refs[0], refs[1]
    k = 2
    if moe:
        route_ref = refs[k]
        k += 1
    p_ref, gw_ref, pw_ref = refs[k:k + 3]
    k += 3
    if final:
        fg_ref = refs[k]
        k += 1
    h_ref = refs[k]
    if moe:
        ysc_ref, sems = refs[-2], refs[-1]
        i = pl.program_id(0)
        slot = i % 2

        def gather(pos, slot_, act):
            def body(r, carry):
                for s in range(2):
                    cp = _row_copy(y_ref, pos[0, 0, s * tm + r], ysc_ref.at[slot_, s], r, sems.at[slot_])
                    act(cp, s)
                return carry
            lax.fori_loop(0, tm, body, 0, unroll=DMA_UNROLL)

        start = lambda cp, s: cp.start(priority=s)
        wait = lambda cp, s: cp.wait()

        @pl.when(i == 0)
        def _():
            gather(pos_ref, 0, start)

        @pl.when(i + 1 < pl.num_programs(0))
        def _():
            gather(pos_next_ref, 1 - slot, start)

        gather(pos_ref, slot, wait)
        rec = route_ref[...]
        lane = lax.broadcasted_iota(jnp.int32, rec.shape, 1)
        g1 = jnp.sum(jnp.where(lane == R_GATE, rec, 0.0), axis=1, keepdims=True)
        g2 = jnp.sum(jnp.where(lane == R_GATE + 1, rec, 0.0), axis=1, keepdims=True)
        h2 = h1_ref[...] + (g1 * ysc_ref[slot, 0] + g2 * ysc_ref[slot, 1])
    else:
        h2 = h1_ref[...] + y_ref[...]
    gate = _dot(h2.astype(BF16), gw_ref[...])
    pe = _dot(p_ref[...].astype(BF16), pw_ref[...])
    h3 = h2 + (1.0 / (1.0 + jnp.exp(-gate))) * pe
    h_ref[...] = h3
    if final:
        refs[k + 1][...] = h3 * lax.rsqrt(jnp.mean(h3 * h3, axis=-1, keepdims=True) + EPS) * fg_ref[...]


def _ple(h1, ys, route, pos_tiles, p, gw, pw, fg, tm):
    t = h1.shape[0]
    moe = route is not None
    final = fg is not None
    row = lambda i: (i, 0)
    const = lambda i: (0, 0)
    ins, in_specs, scratch = [], [], []
    if moe:
        last_tile = t // tm - 1
        ins += [pos_tiles, pos_tiles, h1, ys, route]
        in_specs += [pl.BlockSpec((1, 1, 2 * tm), lambda i: (i, 0, 0), memory_space=pltpu.SMEM),
                     pl.BlockSpec((1, 1, 2 * tm), lambda i: (jnp.minimum(i + 1, last_tile), 0, 0),
                                  memory_space=pltpu.SMEM),
                     pl.BlockSpec((tm, D_MODEL), row), pl.BlockSpec(memory_space=pl.ANY), pl.BlockSpec((tm, LANES), row)]
        scratch = [pltpu.VMEM((2, 2, tm, D_MODEL), F32), pltpu.SemaphoreType.DMA((2,))]
    else:
        ins += [h1, ys]
        in_specs += [pl.BlockSpec((tm, D_MODEL), row), pl.BlockSpec((tm, D_MODEL), row)]
    ins += [p.arr, gw.arr, pw.arr]
    in_specs += [pl.BlockSpec((None, tm, 256), lambda i: (p.idx, i, 0)), _wspec(gw), _wspec(pw)]
    out_specs = [pl.BlockSpec((tm, D_MODEL), row)]
    out_shape = [jax.ShapeDtypeStruct((t, D_MODEL), F32)]
    if final:
        ins.append(fg)
        in_specs.append(pl.BlockSpec(fg.shape, const))
        out_specs.append(pl.BlockSpec((tm, D_MODEL), row))
        out_shape.append(jax.ShapeDtypeStruct((t, D_MODEL), F32))
    return pl.pallas_call(
        functools.partial(_ple_kernel, moe=moe, final=final, tm=tm),
        grid=(t // tm,),
        in_specs=in_specs, out_specs=out_specs, out_shape=out_shape, scratch_shapes=scratch,
        compiler_params=_params(("arbitrary",)),
        name="ple",
    )(*ins)


def _tile(n, pref):
    return pref if n % pref == 0 else n


def _front(h, lw, nb, seq, cache, cprev, pprev):
    t = nb * seq
    tm = _tile(seq, 512)
    if cache is None:
        c0 = jnp.zeros((nb, 1, LANES), F32)
        past = 0
    else:
        ckt, cvt, clf = cache
        past = ckt.arr.shape[-1]
        akct, c0 = _cache_sum(clf, nb, past, _tile(past, 512))
    outs = _in_mix(h, lw, c0, cprev, pprev, nb, seq, tm, min(seq, 128), past, cache is not None,
                   kv_transposed=cache is None and seq % LANES == 0)
    qs, kf, vf, kb, vb, lf, aq, ak, bcd, cst, pst = outs[:11]
    sgv = outs[11] if cache is not None else None
    if cache is None:
        res = _attn_out(qs, aq, kb, ak, vb, bcd, h, lw['w_o'], lw['g2'], lw.get('router'), nb, seq, tm)
    else:
        a = _attention_rows(qs, aq, kb, ak, vb, ckt, akct, cvt, nb, seq, past)
        res = _out_proj(a, bcd, h, lw['w_o'], lw['g2'], lw.get('router'), _tile(t, 512))
    return res, (kf, vf, lf[:, :N_HEADS], cst[:, 2:], pst[:, 1:], sgv)


def _pos_tiles(pos, tm):
    t = pos.shape[0]
    return pos.reshape(t // tm, tm, 2).transpose(0, 2, 1).reshape(t // tm, 1, 2 * tm)


N_FFN_INPUTS = 9


def _ffn_ple_kernel(*refs, final, tm):
    ffn_in, rest = refs[:N_FFN_INPUTS], refs[N_FFN_INPUTS:]
    y_sc = rest[-1]
    _ffn_kernel(*ffn_in, y_sc)
    _ple_kernel(rest[0], y_sc, *rest[1:-1], moe=False, final=final, tm=tm)


def _back_dense(res, p_l, lw, final_g):
    h1, hn = res
    t = h1.shape[0]
    tm = _tile(t, 256)
    nt = t // tm
    final = final_g is not None
    zeros = jnp.zeros((nt,), jnp.int32)
    tabs = (jnp.arange(nt, dtype=jnp.int32), zeros, zeros, zeros + tm, zeros + 1)
    wg, wu, wd, gw, pw = lw['wg'], lw['wu'], lw['wd'], lw['gw'], lw['plw_e']
    d_ff = wg.arr.shape[-1]
    row = lambda w, *_: (w, 0)
    wmap = lambda w, *_: (wg.idx, 0, 0, 0)
    ins = [hn, wg.arr, wu.arr, wd.arr, h1, p_l.arr, gw.arr, pw.arr]
    in_specs = [pl.BlockSpec((tm, D_MODEL), row),
                pl.BlockSpec((None, 1, D_MODEL, d_ff), wmap), pl.BlockSpec((None, 1, D_MODEL, d_ff), wmap),
                pl.BlockSpec((None, 1, d_ff, D_MODEL), wmap),
                pl.BlockSpec((tm, D_MODEL), row), pl.BlockSpec((None, tm, 256), lambda w, *_: (p_l.idx, w, 0)),
                _wspec(gw), _wspec(pw)]
    out_specs = [pl.BlockSpec((tm, D_MODEL), row)]
    out_shape = [jax.ShapeDtypeStruct((t, D_MODEL), F32)]
    if final:
        ins.append(final_g)
        in_specs.append(pl.BlockSpec(final_g.shape, lambda w, *_: (0, 0)))
        out_specs.append(pl.BlockSpec((tm, D_MODEL), row))
        out_shape.append(jax.ShapeDtypeStruct((t, D_MODEL), F32))
    grid_spec = pltpu.PrefetchScalarGridSpec(
        num_scalar_prefetch=5, grid=(nt,), in_specs=in_specs, out_specs=out_specs,
        scratch_shapes=[pltpu.VMEM((tm, D_MODEL), F32)])
    return pl.pallas_call(
        functools.partial(_ffn_ple_kernel, final=final, tm=tm),
        grid_spec=grid_spec,
        out_shape=out_shape,
        compiler_params=_params(("arbitrary",)),
        name="ffn_ple",
    )(*tabs, *ins)


def _back_moe(groups, lw, final_g):
    i32 = jnp.int32
    counts = [res[3][0, :N_EXPERTS].astype(i32) for res, _ in groups]
    n_rows = 2 * sum(res[0].shape[0] for res, _ in groups)
    tf = max(c for c in (256, 128, 64, 32, 16) if n_rows % c == 0)
    tabs, starts = _work_items(sum(counts), n_rows, tf)
    xs, pos_all, offset = None, [], jnp.zeros((N_EXPERTS,), i32)
    for (res, _), cnt in zip(groups, counts):
        h1, hn, route = res[:3]
        experts = route[:, R_EXPERT:R_EXPERT + 2].astype(i32)
        pos = (starts + offset)[experts] + route[:, R_RANK:R_RANK + 2].astype(i32)
        offset = offset + cnt
        tmo = _tile(h1.shape[0], 512)
        xs = _dispatch(hn, _pos_tiles(pos, tmo), n_rows, tmo, xs)
        pos_all.append(pos)
    ys = _ffn(xs, lw['wg'], lw['wu'], lw['wd'], tabs, tf)
    outs = []
    for (res, p_l), pos in zip(groups, pos_all):
        tp = _tile(res[0].shape[0], 256)
        outs.append(_ple(res[0], ys, res[2], _pos_tiles(pos, tp), p_l, lw['gw'], lw['plw_e'], final_g, tp))
    return outs


def kernel(x_prompt, x_sample, p_prompt, p_sample, cache_k, cache_v, cache_logf, state_conv, state_pool, norm1_g, w_in, b_f, conv_dw_w, conv_dw_b, conv_ln_g, conv_ln_b, conv_pw_w, pool_w, pool_scale, sg_w, sg_b, w_o, norm2_g, ffn_w_gate, ffn_w_up, ffn_w_down, moe_router, moe_w_gate, moe_w_up, moe_w_down, ple_w, ple_gate_w, final_g):
    depth = w_in.shape[0]
    bp, sp, _ = x_prompt.shape
    bs, ss, _ = x_sample.shape
    past = cache_k.shape[2]
    hp = x_prompt.reshape(bp * sp, D_MODEL)
    hs = x_sample.reshape(bs * ss, D_MODEL)
    fg = final_g.reshape(1, D_MODEL)

    row = lambda x: x.reshape(depth, 1, -1)
    w_r = jnp.concatenate([w_in[..., 0:768], w_in[..., 772:2052],
                           jnp.pad(w_in[..., 768:772], ((0, 0), (0, 0), (0, LANES - N_HEADS)))], axis=-1).astype(BF16)
    plw = jnp.zeros((depth, GROUP_W, GROUP_W), F32)
    for gi in range(len(POOL_WINDOWS)):
        plw = plw.at[:, 64 * gi:64 * (gi + 1), 64 * gi:64 * (gi + 1)].set(pool_w[:, gi])
    stacked = {
        'g1': row(norm1_g), 'w_in': w_r, 'bf': row(jnp.pad(b_f, ((0, 0), (0, LANES - N_HEADS)))),
        'dw': jnp.pad(conv_dw_w, ((0, 0), (0, 1), (0, 0))), 'dwb': row(conv_dw_b),
        'lng': row(conv_ln_g), 'lnb': row(conv_ln_b), 'pw': conv_pw_w.astype(BF16), 'plw': plw.astype(BF16),
        'psc': row(pool_scale), 'w_o': w_o.astype(BF16), 'g2': row(norm2_g),
        'gw': ple_gate_w.astype(BF16), 'plw_e': ple_w.astype(BF16),
    }
    dense = {'wg': ffn_w_gate.astype(BF16)[:, None], 'wu': ffn_w_up.astype(BF16)[:, None],
             'wd': ffn_w_down.astype(BF16)[:, None]}
    sparse = {'wg': moe_w_gate.astype(BF16), 'wu': moe_w_up.astype(BF16), 'wd': moe_w_down.astype(BF16),
              'router': jnp.pad(moe_router, ((0, 0), (0, 0), (0, LANES - N_EXPERTS))).astype(BF16)}

    def sg_params(tc):
        return {'sgw': sg_w[:, :, :tc, :tc],
                'sgb': jnp.repeat(jnp.swapaxes(sg_b[:, :, :tc], 1, 2), GROUP_W // SG_GROUPS, axis=2)}

    sg_p, sg_s = sg_params(min(sp, 128)), sg_params(min(ss, 128))
    pp_all = p_prompt.reshape(depth, bp * sp, -1)
    ps_all = p_sample.reshape(depth, bs * ss, -1)
    ck_all = jnp.transpose(cache_k, (0, 1, 3, 4, 2)).reshape(depth, bs * GROUP_W, past)
    cv_all = jnp.transpose(cache_v, (0, 1, 3, 4, 2)).reshape(depth, bs * GROUP_W, past)
    clf_all = jnp.pad(cache_logf, ((0, 0), (0, 0), (0, 0), (0, LANES - N_HEADS))).reshape(depth, bs * past, LANES)
    sc_all = jnp.pad(state_conv, ((0, 0), (0, 0), (2, 0), (0, 0)))
    sp_all = jnp.pad(state_pool, ((0, 0), (0, 0), (1, 0), (0, 0)))
    cz, pz = jnp.zeros((bp, 32, GROUP_W), F32), jnp.zeros((bp, 16, GROUP_W), F32)

    outs_p, outs_s = [], []
    for l in range(depth):
        ffn_w = dense if l % 2 == 0 else sparse

        def layer_weights(sg):
            lw = {n: _LW(a, l) for n, a in {**stacked, **sg}.items()}
            lw.update({n: _LW(a, l // 2) for n, a in ffn_w.items()})
            return lw

        last = fg if l == depth - 1 else None
        lwp, lws = layer_weights(sg_p), layer_weights(sg_s)
        res_p, outs = _front(hp, lwp, bp, sp, None, cz, pz)
        outs_p.append(outs)
        res_s, outs = _front(hs, lws, bs, ss, (_LW(ck_all, l), _LW(cv_all, l), clf_all[l]), sc_all[l], sp_all[l])
        outs_s.append(outs)
        if l % 2 == 0:
            hp_res = _back_dense(res_p, _LW(pp_all, l), lwp, last)
            hs_res = _back_dense(res_s, _LW(ps_all, l), lws, last)
        else:
            hp_res, hs_res = _back_moe([(res_p, _LW(pp_all, l)), (res_s, _LW(ps_all, l))], lwp, last)
        if last is None:
            hp, hs = hp_res[0], hs_res[0]
        else:
            y_prompt, y_sample = hp_res[1], hs_res[1]

    def stack(outs, i, shape):
        return jnp.stack([o[i] for o in outs]).reshape((depth,) + shape)

    def stack_kv(i):
        if outs_p[0][i].shape == (bp * sp, GROUP_W):
            return stack(outs_p, i, (bp, sp, N_HEADS, HEAD_DIM))
        kv = stack(outs_p, i, (bp, N_HEADS, HEAD_DIM, sp))
        return jnp.transpose(kv, (0, 1, 4, 2, 3))

    return (y_prompt.reshape(bp, sp, D_MODEL), y_sample.reshape(bs, ss, D_MODEL),
            stack_kv(0), stack_kv(1),
            stack(outs_p, 2, (bp, sp, N_HEADS)), stack(outs_p, 3, (bp, CONV_W - 1, GROUP_W)),
            stack(outs_p, 4, (bp, POOL_HIST, GROUP_W)),
            stack(outs_s, 0, (bs, ss, N_HEADS, HEAD_DIM)), stack(outs_s, 1, (bs, ss, N_HEADS, HEAD_DIM)),
            stack(outs_s, 2, (bs, ss, N_HEADS)), stack(outs_s, 3, (bs, CONV_W - 1, GROUP_W)),
            stack(outs_s, 4, (bs, POOL_HIST, GROUP_W)), stack(outs_s, 5, (bs, ss, GROUP_W)))
```

```python
import functools
from typing import NamedTuple

import numpy as np
import jax
import jax.numpy as jnp
from jax import lax
from jax.experimental import pallas as pl
from jax.experimental.pallas import tpu as pltpu

F32 = jnp.float32
BF16 = jnp.bfloat16
EPS = 1e-6

D_MODEL = 1024
GROUP_W = 256
N_HEADS = 4
HEAD_DIM = 64
CONV_W = 31
POOL_WINDOWS = (2, 4, 8, 16)
POOL_HIST = 15
SG_GROUPS = 4
N_EXPERTS = 8
LANES = 128
AUG_PER_HEAD = 6
VMEM_LIMIT = 56 * 1024 * 1024


def _params(sem, vmem=VMEM_LIMIT):
    return pltpu.CompilerParams(dimension_semantics=sem, vmem_limit_bytes=vmem)


def _dot(a, b):
    return jnp.dot(a, b, preferred_element_type=F32)


class _LW(NamedTuple):
    arr: jax.Array
    idx: int


def _wspec(w):
    nd = w.arr.ndim - 1
    return pl.BlockSpec((None,) + w.arr.shape[1:], lambda *_: (w.idx,) + (0,) * nd)


def _split3(x):
    hi = x.astype(BF16)
    r1 = x - hi.astype(F32)
    mid = r1.astype(BF16)
    lo = (r1 - mid.astype(F32)).astype(BF16)
    return jnp.concatenate([hi, mid, lo], axis=1)


def _running_sum(x, tril, carry_ref):
    cs = _dot(tril, _split3(x))
    c = carry_ref[...] + (cs[:, :LANES] + cs[:, LANES:2 * LANES] + cs[:, 2 * LANES:])
    n = x.shape[0]
    carry_ref[...] = c[n - 1:n, :]
    return c


def _placement(sign_k):
    pq = np.zeros((3 * LANES, 2 * LANES), np.float32)
    pk = np.zeros((3 * LANES, 2 * LANES), np.float32)
    oq = np.zeros((1, 2 * LANES), np.float32)
    ok = np.zeros((1, 2 * LANES), np.float32)
    for h in range(N_HEADS):
        p, j = divmod(h, 2)
        for i in range(3):
            pq[i * LANES + h, p * LANES + j * AUG_PER_HEAD + i] = 1.0
            oq[0, p * LANES + j * AUG_PER_HEAD + 3 + i] = 1.0
            pk[i * LANES + h, p * LANES + j * AUG_PER_HEAD + 3 + i] = sign_k
            ok[0, p * LANES + j * AUG_PER_HEAD + i] = 1.0
    return pq, pk, oq, ok


def _in_proj_kernel(h_ref, g_ref, w_ref, bf_ref, c0_ref, tril_ref, pq_ref, pk_ref, oq_ref, ok_ref,
                    qs_ref, kf_ref, vf_ref, kb_ref, vb_ref, lf_ref, aq_ref, ak_ref, zc_ref, carry_ref,
                    kv_transposed=False):
    @pl.when(pl.program_id(1) == 0)
    def _():
        carry_ref[...] = c0_ref[0]

    x = h_ref[...]
    hn = x * lax.rsqrt(jnp.mean(x * x, axis=-1, keepdims=True) + EPS) * g_ref[...]
    hb = hn.astype(BF16)
    qs_ref[...] = (_dot(hb, w_ref[:, 0:256]) * (HEAD_DIM ** -0.5)).astype(BF16)
    k = _dot(hb, w_ref[:, 256:512])
    kf_ref[...] = k.T if kv_transposed else k
    kb_ref[...] = k.astype(BF16)
    v = _dot(hb, w_ref[:, 512:768])
    vf_ref[...] = v.T if kv_transposed else v
    vb_ref[...] = v.astype(BF16)
    zc_ref[...] = _dot(hb, w_ref[:, 768:2048])
    zf = _dot(hb, w_ref[:, 2048:2176]) + bf_ref[...]
    lf = jnp.minimum(zf, 0.0) - jnp.log1p(jnp.exp(-jnp.abs(zf)))
    lane = lax.broadcasted_iota(jnp.int32, lf.shape, 1)
    lf = jnp.where(lane < N_HEADS, lf, 0.0)
    lf_ref[...] = lf
    cp = _split3(_running_sum(lf, tril_ref[...], carry_ref))
    aq_ref[...] = (_dot(cp, pq_ref[...]) + oq_ref[...]).astype(BF16)
    ak_ref[...] = (_dot(cp, pk_ref[...]) + ok_ref[...]).astype(BF16)


N_IN_PROJ_INPUTS, N_IN_PROJ_OUTPUTS, N_MIX_INPUTS = 10, 8, 11


def _in_mix_kernel(*refs, ts, tc, pos0, want_sgv, kv_transposed):
    proj_in = refs[:N_IN_PROJ_INPUTS]
    mix_in = refs[N_IN_PROJ_INPUTS:N_IN_PROJ_INPUTS + N_MIX_INPUTS]
    rest = refs[N_IN_PROJ_INPUTS + N_MIX_INPUTS:]
    proj_out, rest = rest[:N_IN_PROJ_OUTPUTS], rest[N_IN_PROJ_OUTPUTS:]
    bcd_ref, cst_ref, pst_ref = rest[:3]
    sgv_ref = rest[3] if want_sgv else None
    carry_ref, zc_sc, xc_ref, xp_ref = rest[3 + want_sgv:]
    _in_proj_kernel(*proj_in, *proj_out, zc_sc, carry_ref, kv_transposed=kv_transposed)
    z = [zc_sc.at[:, j * GROUP_W:(j + 1) * GROUP_W] for j in range(5)]
    _mixers_kernel(*z, *mix_in, bcd_ref, cst_ref, pst_ref, sgv_ref, xc_ref, xp_ref, ts=ts, tc=tc, pos0=pos0)


def _in_mix(h, lw, c0, cprev, pprev, nb, seq, ts, tc, pos0, want_sgv, kv_transposed):
    t = nb * seq
    ns = seq // ts
    tril = jnp.asarray(np.tril(np.ones((ts, ts), np.float32)), BF16)
    pq, pk, oq, ok = _placement(-1.0)
    row = lambda b, s: (b * ns + s, 0)
    const = lambda b, s: (0, 0)
    blk = lambda w_: pl.BlockSpec((ts, w_), row)
    full = lambda a: pl.BlockSpec(a.shape, const)
    perb = lambda r: pl.BlockSpec((1, r, GROUP_W), lambda b, s: (b, 0, 0))
    consts = [jnp.asarray(pq, BF16), jnp.asarray(pk, BF16), jnp.asarray(oq), jnp.asarray(ok)]
    small = [lw[n] for n in ('dw', 'dwb', 'lng', 'lnb', 'pw', 'plw', 'psc', 'sgw', 'sgb')]
    outs = [(256, BF16), (256, F32), (256, F32), (256, BF16), (256, BF16), (LANES, F32), (256, BF16), (256, BF16),
            (768, BF16)]
    out_specs = [blk(w_) for w_, _ in outs] + [perb(32), perb(16)]
    out_shape = [jax.ShapeDtypeStruct((t, w_), dt) for w_, dt in outs] + [
        jax.ShapeDtypeStruct((nb, 32, GROUP_W), F32), jax.ShapeDtypeStruct((nb, 16, GROUP_W), F32)]
    if kv_transposed:
        for i in (1, 2):
            out_specs[i] = pl.BlockSpec((GROUP_W, ts), lambda b, s: (b, s))
            out_shape[i] = jax.ShapeDtypeStruct((nb * GROUP_W, seq), F32)
    if want_sgv:
        out_specs.append(blk(GROUP_W))
        out_shape.append(jax.ShapeDtypeStruct((t, GROUP_W), F32))
    return pl.pallas_call(
        functools.partial(_in_mix_kernel, ts=ts, tc=tc, pos0=pos0, want_sgv=want_sgv, kv_transposed=kv_transposed),
        grid=(nb, ns),
        in_specs=[blk(D_MODEL), _wspec(lw['g1']), _wspec(lw['w_in']), _wspec(lw['bf']),
                  pl.BlockSpec((1, 1, LANES), lambda b, s: (b, 0, 0)), full(tril)] + [full(a) for a in consts]
                 + [perb(32), perb(16)] + [_wspec(a) for a in small],
        out_specs=out_specs,
        out_shape=out_shape,
        scratch_shapes=[pltpu.VMEM((1, LANES), F32), pltpu.VMEM((ts, 5 * GROUP_W), F32),
                        pltpu.VMEM((40 + ts, GROUP_W), F32), pltpu.VMEM((16 + ts, GROUP_W), F32)],
        compiler_params=_params(("arbitrary", "arbitrary")),
        name="in_mix",
    )(h, lw['g1'].arr, lw['w_in'].arr, lw['bf'].arr, c0, tril, *consts, cprev, pprev, *[a.arr for a in small])


def _cache_sum_kernel(lf_ref, tril_ref, pk_ref, ok_ref, ak_ref, tot_ref, carry_ref):
    @pl.when(pl.program_id(1) == 0)
    def _():
        carry_ref[...] = jnp.zeros_like(carry_ref)

    cp = _split3(_running_sum(lf_ref[...], tril_ref[...], carry_ref))
    ak_ref[...] = (_dot(cp, pk_ref[...]) + ok_ref[...]).T.astype(BF16)
    tot_ref[0] = carry_ref[...]


def _cache_sum(lfp, nb, past, tm):
    ns = past // tm
    tril = jnp.asarray(np.tril(np.ones((tm, tm), np.float32)), BF16)
    _, pk, _, ok = _placement(-1.0)
    pk, ok = jnp.asarray(pk, BF16), jnp.asarray(ok)
    const = lambda b, s: (0, 0)
    return pl.pallas_call(
        _cache_sum_kernel,
        grid=(nb, ns),
        in_specs=[pl.BlockSpec((tm, LANES), lambda b, s: (b * ns + s, 0)), pl.BlockSpec(tril.shape, const),
                  pl.BlockSpec(pk.shape, const), pl.BlockSpec(ok.shape, const)],
        out_specs=[pl.BlockSpec((256, tm), lambda b, s: (b, s)),
                   pl.BlockSpec((1, 1, LANES), lambda b, s: (b, 0, 0))],
        out_shape=[jax.ShapeDtypeStruct((nb * 256, past), BF16), jax.ShapeDtypeStruct((nb, 1, LANES), F32)],
        scratch_shapes=[pltpu.VMEM((1, LANES), F32)],
        compiler_params=_params(("arbitrary", "arbitrary")),
        name="cache_sum",
    )(lfp, tril, pk, ok)


NEG_BIG = -1e30


def _head_lhs(q, aq, j, lane):
    zero = jnp.zeros((), BF16)
    qm = jnp.where((lane >= HEAD_DIM * j) & (lane < HEAD_DIM * (j + 1)), q, zero)
    am = jnp.where((lane >= AUG_PER_HEAD * j) & (lane < AUG_PER_HEAD * (j + 1)), aq, zero)
    return jnp.concatenate([qm, am], axis=1)


def _scores(lhs, k, ak):
    rhs = jnp.concatenate([k, ak], axis=1)
    return lax.dot_general(lhs, rhs, (((1,), (1,)), ((), ())), preferred_element_type=F32)


def _loop_by_two(n, fn):
    def body(i, carry):
        fn(2 * i)
        fn(2 * i + 1)
        return carry

    lax.fori_loop(0, lax.shift_right_logical(n, 1), body, 0)

    @pl.when((n & 1) == 1)
    def _():
        fn(n - 1)


def _attn_kernel(q_ref, aq_ref, k_ref, ak_ref, v_ref, o_ref, m_sc, l_sc, acc_sc, e_sc, f_sc, *, t):
    qi = pl.program_id(1)
    lane = lax.broadcasted_iota(jnp.int32, (1, LANES), 1)
    row = lax.broadcasted_iota(jnp.int32, (t, t), 0)
    col = lax.broadcasted_iota(jnp.int32, (t, t), 1)
    nct = t // LANES
    for p in range(2):
        cs = slice(p * LANES, (p + 1) * LANES)
        lhs = [_head_lhs(q_ref[:, cs], aq_ref[:, cs], j, lane) for j in range(2)]

        def scores(j, ki, masked):
            rows = pl.ds(pl.multiple_of(ki * t, t), t)
            s = _scores(lhs[j], k_ref[rows, cs], ak_ref[rows, cs])
            if masked:
                s = jnp.where(col <= row, s, -jnp.inf)
            return [s[:, c * LANES:(c + 1) * LANES] for c in range(nct)]

        def stats(ki, masked):
            for j in range(2):
                parts = scores(j, ki, masked)
                m_old = m_sc[j]
                m_new = functools.reduce(jnp.maximum, parts, m_old)
                tot = l_sc[j] * jnp.exp(m_old - m_new)
                for c, part in enumerate(parts):
                    e = jnp.exp(part - m_new)
                    e_sc[j, ki, :, c * LANES:(c + 1) * LANES] = e
                    tot = tot + e
                l_sc[j] = tot
                m_sc[j] = m_new
                f_sc[j, ki] = m_new

        m_sc[...] = jnp.full_like(m_sc, NEG_BIG)
        l_sc[...] = jnp.zeros_like(l_sc)
        _loop_by_two(qi, lambda ki: stats(ki, False))
        stats(qi, True)
        for j in range(2):
            m_lane = m_sc[j]
            m_row = jnp.max(m_lane, axis=1, keepdims=True)
            l_row = jnp.sum(l_sc[j] * jnp.exp(m_lane - m_row), axis=1, keepdims=True)
            m_sc[j] = jnp.broadcast_to(m_row, m_lane.shape)
            l_sc[j] = jnp.broadcast_to(1.0 / l_row, m_lane.shape)

        acc_sc[...] = jnp.zeros_like(acc_sc)

        def accumulate(ki):
            v = v_ref[pl.ds(pl.multiple_of(ki * t, t), t), cs]
            for j in range(2):
                f = jnp.exp(f_sc[j, ki] - m_sc[j]) * l_sc[j]
                prob = jnp.concatenate([(e_sc[j, ki, :, c * LANES:(c + 1) * LANES] * f).astype(BF16)
                                        for c in range(nct)], axis=1)
                acc_sc[j] += _dot(prob, v)

        _loop_by_two(qi + 1, accumulate)
        o_ref[:, cs] = jnp.where(lane < HEAD_DIM, acc_sc[0], acc_sc[1]).astype(BF16)


N_ATTN_INPUTS, N_ATTN_SCRATCH = 5, 5


def _attn_out_kernel(*refs, t, moe):
    attn_in, rest = refs[:N_ATTN_INPUTS], refs[N_ATTN_INPUTS:]
    n_proj = 4 + 2 * moe + 2 + 2 * moe
    proj, rest = rest[:n_proj], rest[n_proj:]
    attn_sc, a_sc, carry = rest[:N_ATTN_SCRATCH], rest[N_ATTN_SCRATCH], rest[N_ATTN_SCRATCH + 1:]
    _attn_kernel(*attn_in, a_sc, *attn_sc, t=t)
    first = (pl.program_id(0) == 0) & (pl.program_id(1) == 0)
    _out_proj_kernel(a_sc, *proj, *carry, moe=moe, first_step=first)


def _attn_out(qs, aq, kb, ak, vb, bcd, h, wo, g2, router, nb, seq, t):
    assert seq % t == 0 and t % LANES == 0
    nq = seq // t
    tt = nb * seq
    moe = router is not None
    row = lambda b, i: (b * nq + i, 0)
    const = lambda b, i: (0, 0)
    qblk = pl.BlockSpec((t, 256), row)
    kblk = pl.BlockSpec((seq, 256), lambda b, i: (b, 0))
    ins = [qs, aq, kb, ak, vb, bcd, h, wo.arr, g2.arr]
    in_specs = [qblk, qblk, kblk, kblk, kblk, pl.BlockSpec((t, 768), row), pl.BlockSpec((t, D_MODEL), row),
                _wspec(wo), _wspec(g2)]
    out_specs = [pl.BlockSpec((t, D_MODEL), row), pl.BlockSpec((t, D_MODEL), row)]
    out_shape = [jax.ShapeDtypeStruct((tt, D_MODEL), F32), jax.ShapeDtypeStruct((tt, D_MODEL), F32 if moe else BF16)]
    scratch = [pltpu.VMEM((2, t, LANES), F32)] * 3 + [pltpu.VMEM((2, nq, t, t), F32),
                                                      pltpu.VMEM((2, nq, t, LANES), F32), pltpu.VMEM((t, 256), BF16)]
    if moe:
        stril = jnp.asarray(np.tril(np.ones((t, t), np.float32), -1), BF16)
        ins += [router.arr, stril]
        in_specs += [_wspec(router), pl.BlockSpec(stril.shape, const)]
        out_specs += [pl.BlockSpec((t, LANES), row), pl.BlockSpec((1, LANES), const)]
        out_shape += [jax.ShapeDtypeStruct((tt, LANES), F32), jax.ShapeDtypeStruct((1, LANES), F32)]
        scratch.append(pltpu.VMEM((1, LANES), F32))
    return pl.pallas_call(
        functools.partial(_attn_out_kernel, t=t, moe=moe),
        grid=(nb, nq),
        in_specs=in_specs, out_specs=out_specs, out_shape=out_shape, scratch_shapes=scratch,
        compiler_params=_params(("arbitrary", "arbitrary")),
        name="attn_out",
    )(*ins)


def _attn_rows_kernel(q_ref, aq_ref, k_ref, ak_ref, v_ref, ckt_ref, akct_ref, cvt_ref, o_ref):
    lq = q_ref.shape[0]
    lane = lax.broadcasted_iota(jnp.int32, (1, LANES), 1)
    row = lax.broadcasted_iota(jnp.int32, (lq, lq), 0)
    col = lax.broadcasted_iota(jnp.int32, (lq, lq), 1)
    nt = (((1,), (1,)), ((), ()))
    for p in range(2):
        cs = slice(p * LANES, (p + 1) * LANES)
        cache_rhs = jnp.concatenate([ckt_ref[cs, :].astype(BF16), akct_ref[cs, :]], axis=0)
        cache_v = cvt_ref[cs, :].astype(BF16)
        outs = []
        for j in range(2):
            lhs = _head_lhs(q_ref[:, cs], aq_ref[:, cs], j, lane)
            s_c = _dot(lhs, cache_rhs)
            s_n = jnp.where(col <= row, _scores(lhs, k_ref[:, cs], ak_ref[:, cs]), -jnp.inf)
            m = jnp.maximum(jnp.max(s_c, axis=1, keepdims=True), jnp.max(s_n, axis=1, keepdims=True))
            e_c, e_n = jnp.exp(s_c - m), jnp.exp(s_n - m)
            r = 1.0 / (jnp.sum(e_c, axis=1, keepdims=True) + jnp.sum(e_n, axis=1, keepdims=True))
            outs.append(lax.dot_general((e_c * r).astype(BF16), cache_v, nt, preferred_element_type=F32)
                        + _dot((e_n * r).astype(BF16), v_ref[:, cs]))
        o_ref[:, cs] = jnp.where(lane < HEAD_DIM, outs[0], outs[1]).astype(BF16)


def _attention_rows(qs, aq, kb, ak, vb, ckt, akct, cvt, nb, lq, past):
    qblk = pl.BlockSpec((lq, 256), lambda b: (b, 0))
    cblk = pl.BlockSpec((256, past), lambda b: (b, 0))
    lblk = lambda w: pl.BlockSpec((None, 256, past), lambda b: (w.idx, b, 0))
    return pl.pallas_call(
        _attn_rows_kernel,
        grid=(nb,),
        in_specs=[qblk] * 5 + [lblk(ckt), cblk, lblk(cvt)],
        out_specs=qblk,
        out_shape=jax.ShapeDtypeStruct((nb * lq, 256), BF16),
        compiler_params=_params(("arbitrary",)),
        name="attn_rows",
    )(qs, aq, kb, ak, vb, ckt.arr, akct, cvt.arr)


ROW_CHUNK = 64


def _gelu(x):
    return 0.5 * x * (1.0 + jnp.tanh(0.7978845608028654 * (x + 0.044715 * (x * x * x))))


def _mixers_kernel(ca_ref, cg_ref, px_ref, su_ref, sv_ref, cprev_ref, pprev_ref, dw_ref, dwb_ref, lng_ref, lnb_ref,
                   pw_ref, plw_ref, psc_ref, sgw_ref, sgb_ref,
                   bcd_ref, cst_ref, pst_ref, sgv_ref, xc_ref, xp_ref, *, ts, tc, pos0):
    s = pl.program_id(1)

    @pl.when(s == 0)
    def _():
        xc_ref[0:32, :] = cprev_ref[0].astype(BF16).astype(F32)
        xp_ref[0:16, :] = pprev_ref[0]

    @pl.when(s > 0)
    def _():
        xc_ref[0:32, :] = xc_ref[ts:ts + 32, :]
        xp_ref[0:16, :] = xp_ref[ts:ts + 16, :]

    g = cg_ref[...]
    u = ca_ref[...] * (1.0 / (1.0 + jnp.exp(-g)))
    xc_ref[32:32 + ts, :] = u.astype(BF16).astype(F32)
    xc_ref[32 + ts:40 + ts, :] = jnp.zeros((8, GROUP_W), F32)
    cst_ref[0] = u[ts - 32:ts, :]
    xp_ref[16:16 + ts, :] = px_ref[...]
    pst_ref[0] = xp_ref[ts:ts + 16, :]

    rc = min(ts, ROW_CHUNK)
    lane = lax.broadcasted_iota(jnp.int32, (rc, GROUP_W), 1)
    conv_act, pool_dlt = [], []
    for r0 in range(0, ts, rc):
        y = jnp.zeros((rc, GROUP_W), F32)
        for r in range(8):
            part = jnp.zeros((rc + 8, GROUP_W), F32)
            for a in range(r if r >= 2 else r + 8, CONV_W + 2, 8):
                part = part + xc_ref[r0 + a - r:r0 + a - r + rc + 8, :] * dw_ref[a - 2:a - 1, :]
            y = y + part[r:r + rc, :]
        y = y + dwb_ref[...]
        mu = jnp.mean(y, axis=-1, keepdims=True)
        var = jnp.mean(jnp.square(y - mu), axis=-1, keepdims=True)
        y = (y - mu) * lax.rsqrt(var + EPS) * lng_ref[...] + lnb_ref[...]
        conv_act.append((y * (1.0 / (1.0 + jnp.exp(-y)))).astype(BF16))

        x = xp_ref[16 + r0:16 + r0 + rc, :]
        pos = (pos0 + s * ts + r0 + lax.broadcasted_iota(jnp.int32, (rc, 1), 0)).astype(F32)
        acc = x
        mean = jnp.zeros((rc, GROUP_W), F32)
        wi = 0
        for i in range(1, POOL_WINDOWS[-1] + 1):
            if i > 1:
                acc = acc + xp_ref[16 - (i - 1) + r0:16 - (i - 1) + r0 + rc, :]
            if i == POOL_WINDOWS[wi]:
                cnt = jnp.minimum(pos + 1.0, float(i))
                mean = jnp.where((lane >= 64 * wi) & (lane < 64 * (wi + 1)), acc / cnt, mean)
                wi += 1
        pool_dlt.append((mean - x).astype(BF16))
    bcd_ref[:, 0:256] = _dot(jnp.concatenate(conv_act, axis=0), pw_ref[...]).astype(BF16)
    bcd_ref[:, 256:512] = (_dot(jnp.concatenate(pool_dlt, axis=0), plw_ref[...]) * psc_ref[...]).astype(BF16)

    r = lax.broadcasted_iota(jnp.int32, (tc, tc), 0)
    c = lax.broadcasted_iota(jnp.int32, (tc, tc), 1)
    lane_c = lax.broadcasted_iota(jnp.int32, (tc, GROUP_W), 1)
    wts = [jnp.where(c <= r, sgw_ref[gi], 0.0).astype(BF16) for gi in range(SG_GROUPS)]
    for ci in range(ts // tc):
        uu = _gelu(su_ref[ci * tc:(ci + 1) * tc, :])
        vv = _gelu(sv_ref[ci * tc:(ci + 1) * tc, :])
        if sgv_ref is not None:
            sgv_ref[ci * tc:(ci + 1) * tc, :] = vv
        m = sgb_ref[...]
        for gi in range(SG_GROUPS):
            vg = jnp.where((lane_c >= 64 * gi) & (lane_c < 64 * (gi + 1)), vv, 0.0).astype(BF16)
            m = m + _dot(wts[gi], vg)
        bcd_ref[ci * tc:(ci + 1) * tc, 512:768] = (uu * m).astype(BF16)


R_GATE, R_EXPERT, R_RANK = 0, 2, 4


def _out_proj_kernel(a_ref, bcd_ref, h_ref, wo_ref, g_ref, *rest, moe, first_step=None):
    if first_step is None:
        first_step = pl.program_id(0) == 0
    if moe:
        router_ref, stril_ref, h1_ref, hn_ref, route_ref, cnt_ref, carry_ref = rest
    else:
        h1_ref, hn_ref = rest
    mix = _dot(a_ref[...], wo_ref[0:256, :]) + _dot(bcd_ref[...], wo_ref[256:1024, :])
    h1 = h_ref[...] + mix
    hn = h1 * lax.rsqrt(jnp.mean(h1 * h1, axis=-1, keepdims=True) + EPS) * g_ref[...]
    h1_ref[...] = h1
    hn_ref[...] = hn.astype(hn_ref.dtype)
    if moe:
        @pl.when(first_step)
        def _():
            carry_ref[...] = jnp.zeros_like(carry_ref)

        logits = _dot(hn.astype(BF16), router_ref[...])
        lane = lax.broadcasted_iota(jnp.int32, logits.shape, 1)
        logits = jnp.where(lane < N_EXPERTS, logits, -jnp.inf)
        v1 = jnp.max(logits, axis=1, keepdims=True)
        i1 = jnp.min(jnp.where(logits == v1, lane, LANES), axis=1, keepdims=True)
        rest_l = jnp.where(lane == i1, -jnp.inf, logits)
        v2 = jnp.max(rest_l, axis=1, keepdims=True)
        i2 = jnp.min(jnp.where(rest_l == v2, lane, LANES), axis=1, keepdims=True)
        e = jnp.exp(v2 - v1)
        g1 = 1.0 / (1.0 + e)
        oh1, oh2 = lane == i1, lane == i2
        oh = jnp.where(oh1 | oh2, 1.0, 0.0)
        before = _dot(stril_ref[...], oh.astype(BF16)) + carry_ref[...]
        r1 = jnp.sum(jnp.where(oh1, before, 0.0), axis=1, keepdims=True)
        r2 = jnp.sum(jnp.where(oh2, before, 0.0), axis=1, keepdims=True)
        carry_ref[...] += jnp.sum(oh, axis=0, keepdims=True)
        cnt_ref[...] = carry_ref[...]
        rec = jnp.zeros_like(logits)
        for ln, val in ((R_GATE, g1), (R_GATE + 1, e * g1), (R_EXPERT, i1.astype(F32)), (R_EXPERT + 1, i2.astype(F32)),
                        (R_RANK, r1), (R_RANK + 1, r2)):
            rec = jnp.where(lane == ln, val, rec)
        route_ref[...] = rec


def _out_proj(a, bcd, h, wo, g2, router, tm):
    t = h.shape[0]
    moe = router is not None
    row = lambda i: (i, 0)
    const = lambda i: (0, 0)
    ins = [a, bcd, h, wo.arr, g2.arr]
    in_specs = [pl.BlockSpec((tm, 256), row), pl.BlockSpec((tm, 768), row), pl.BlockSpec((tm, D_MODEL), row),
                _wspec(wo), _wspec(g2)]
    out_specs = [pl.BlockSpec((tm, D_MODEL), row), pl.BlockSpec((tm, D_MODEL), row)]
    out_shape = [jax.ShapeDtypeStruct((t, D_MODEL), F32), jax.ShapeDtypeStruct((t, D_MODEL), F32 if moe else BF16)]
    scratch = []
    if moe:
        stril = jnp.asarray(np.tril(np.ones((tm, tm), np.float32), -1), BF16)
        ins += [router.arr, stril]
        in_specs += [_wspec(router), pl.BlockSpec(stril.shape, const)]
        out_specs += [pl.BlockSpec((tm, LANES), row), pl.BlockSpec((1, LANES), const)]
        out_shape += [jax.ShapeDtypeStruct((t, LANES), F32), jax.ShapeDtypeStruct((1, LANES), F32)]
        scratch = [pltpu.VMEM((1, LANES), F32)]
    return pl.pallas_call(
        functools.partial(_out_proj_kernel, moe=moe),
        grid=(t // tm,),
        in_specs=in_specs, out_specs=out_specs, out_shape=out_shape, scratch_shapes=scratch,
        compiler_params=_params(("arbitrary",)),
        name="out_proj",
    )(*ins)


DMA_UNROLL = 8


def _row_copy(src_ref, src_row, dst_ref, dst_row, sem):
    return pltpu.make_async_copy(src_ref.at[pl.ds(src_row, 1)], dst_ref.at[pl.ds(dst_row, 1)], sem)


def _dispatch_kernel(pos_ref, x_ref, *rest, tm):
    xs_ref, sem = rest[-2:]

    def issue(r, carry):
        for k in range(2):
            _row_copy(x_ref, r, xs_ref, pos_ref[0, 0, k * tm + r], sem).start(priority=k)
        return carry

    def drain(r, carry):
        for k in range(2):
            _row_copy(x_ref, r, xs_ref, pos_ref[0, 0, k * tm + r], sem).wait()
        return carry

    lax.fori_loop(0, tm, issue, 0, unroll=DMA_UNROLL)
    lax.fori_loop(0, tm, drain, 0, unroll=DMA_UNROLL)


def _dispatch(hn, pos_tiles, n_rows, tm, xs_prev=None):
    t = hn.shape[0]
    ins = [pos_tiles, hn]
    in_specs = [pl.BlockSpec((1, 1, 2 * tm), lambda i: (i, 0, 0), memory_space=pltpu.SMEM),
                pl.BlockSpec((tm, D_MODEL), lambda i: (i, 0))]
    aliases = {}
    if xs_prev is not None:
        ins.append(xs_prev)
        in_specs.append(pl.BlockSpec(memory_space=pl.ANY))
        aliases = {2: 0}
    return pl.pallas_call(
        functools.partial(_dispatch_kernel, tm=tm),
        grid=(t // tm,),
        in_specs=in_specs,
        out_specs=pl.BlockSpec(memory_space=pl.ANY),
        out_shape=jax.ShapeDtypeStruct((n_rows, D_MODEL), F32),
        scratch_shapes=[pltpu.SemaphoreType.DMA],
        input_output_aliases=aliases,
        compiler_params=_params(("arbitrary",)),
        name="dispatch",
    )(*ins)


def _ffn_kernel(wt_ref, we_ref, lo_ref, hi_ref, first_ref, x_ref, wg_ref, wu_ref, wd_ref, y_ref):
    w = pl.program_id(0)
    lo, hi = lo_ref[w], hi_ref[w]

    @pl.when(hi > lo)
    def _():
        rowi = lax.broadcasted_iota(jnp.int32, (x_ref.shape[0], 1), 0)
        x = jnp.where((rowi >= lo) & (rowi < hi), x_ref[...], jnp.zeros((), x_ref.dtype)).astype(BF16)
        gate = _dot(x, wg_ref[0])
        up = _dot(x, wu_ref[0])
        act = (gate * (1.0 / (1.0 + jnp.exp(-gate))) * up).astype(BF16)
        y = _dot(act, wd_ref[0])

        @pl.when(first_ref[w] == 1)
        def _():
            y_ref[...] = y

        @pl.when(first_ref[w] == 0)
        def _():
            y_ref[...] += y


def _ffn(xs, wg, wu, wd, tabs, tm):
    d_ff = wg.arr.shape[-1]
    xmap = lambda w, wt, we, lo, hi, fi: (wt[w], 0)
    wmap = lambda w, wt, we, lo, hi, fi: (wg.idx, we[w], 0, 0)
    grid_spec = pltpu.PrefetchScalarGridSpec(
        num_scalar_prefetch=5,
        grid=(tabs[0].shape[0],),
        in_specs=[pl.BlockSpec((tm, D_MODEL), xmap),
                  pl.BlockSpec((None, 1, D_MODEL, d_ff), wmap), pl.BlockSpec((None, 1, D_MODEL, d_ff), wmap),
                  pl.BlockSpec((None, 1, d_ff, D_MODEL), wmap)],
        out_specs=pl.BlockSpec((tm, D_MODEL), xmap),
    )
    return pl.pallas_call(
        _ffn_kernel,
        grid_spec=grid_spec,
        out_shape=jax.ShapeDtypeStruct((xs.shape[0], D_MODEL), F32),
        compiler_params=_params(("arbitrary",)),
        name="ffn",
    )(*tabs, xs, wg.arr, wu.arr, wd.arr)


def _work_items(counts, n_rows, tm):
    i32 = jnp.int32
    ends = jnp.cumsum(counts)
    starts = ends - counts
    t_first = starts // tm
    n_e = jnp.where(counts > 0, (ends - 1) // tm - t_first + 1, 0)
    w_end = jnp.cumsum(n_e)
    w_start = w_end - n_e
    n_work = n_rows // tm + counts.shape[0] - 1
    w = jnp.arange(n_work, dtype=i32)
    valid = w < w_end[-1]
    e = jnp.minimum(jnp.sum((w[:, None] >= w_end[None, :]).astype(i32), axis=1), counts.shape[0] - 1)
    tile = jnp.where(valid, t_first[e] + (w - w_start[e]), n_rows // tm - 1).astype(i32)
    lo = jnp.where(valid, jnp.maximum(starts[e] - tile * tm, 0), 0).astype(i32)
    hi = jnp.where(valid, jnp.minimum(ends[e] - tile * tm, tm), 0).astype(i32)
    e = jnp.where(valid, e, e[jnp.maximum(w_end[-1] - 1, 0)]).astype(i32)
    first = jnp.concatenate([jnp.ones((1,), i32), (tile[1:] != tile[:-1]).astype(i32)])
    return (tile, e, lo, hi, first), starts


def _ple_kernel(*refs, moe, final, tm):
    refs = list(refs)
    if moe:
        pos_ref, pos_next_ref = refs.pop(0), refs.pop(0)
    h1_ref, y_ref = refs[0], refs[1]
    k = 2
    if moe:
        route_ref = refs[k]
        k += 1
    p_ref, gw_ref, pw_ref = refs[k:k + 3]
    k += 3
    if final:
        fg_ref = refs[k]
        k += 1
    h_ref = refs[k]
    if moe:
        ysc_ref, sems = refs[-2], refs[-1]
        i = pl.program_id(0)
        slot = i % 2

        def gather(pos, slot_, act):
            def body(r, carry):
                for s in range(2):
                    cp = _row_copy(y_ref, pos[0, 0, s * tm + r], ysc_ref.at[slot_, s], r, sems.at[slot_])
                    act(cp, s)
                return carry
            lax.fori_loop(0, tm, body, 0, unroll=DMA_UNROLL)

        start = lambda cp, s: cp.start(priority=s)
        wait = lambda cp, s: cp.wait()

        @pl.when(i == 0)
        def _():
            gather(pos_ref, 0, start)

        @pl.when(i + 1 < pl.num_programs(0))
        def _():
            gather(pos_next_ref, 1 - slot, start)

        gather(pos_ref, slot, wait)
        rec = route_ref[...]
        lane = lax.broadcasted_iota(jnp.int32, rec.shape, 1)
        g1 = jnp.sum(jnp.where(lane == R_GATE, rec, 0.0), axis=1, keepdims=True)
        g2 = jnp.sum(jnp.where(lane == R_GATE + 1, rec, 0.0), axis=1, keepdims=True)
        h2 = h1_ref[...] + (g1 * ysc_ref[slot, 0] + g2 * ysc_ref[slot, 1])
    else:
        h2 = h1_ref[...] + y_ref[...]
    gate = _dot(h2.astype(BF16), gw_ref[...])
    pe = _dot(p_ref[...].astype(BF16), pw_ref[...])
    h3 = h2 + (1.0 / (1.0 + jnp.exp(-gate))) * pe
    h_ref[...] = h3
    if final:
        refs[k + 1][...] = h3 * lax.rsqrt(jnp.mean(h3 * h3, axis=-1, keepdims=True) + EPS) * fg_ref[...]


def _ple(h1, ys, route, pos_tiles, p, gw, pw, fg, tm):
    t = h1.shape[0]
    moe = route is not None
    final = fg is not None
    row = lambda i: (i, 0)
    const = lambda i: (0, 0)
    ins, in_specs, scratch = [], [], []
    if moe:
        last_tile = t // tm - 1
        ins += [pos_tiles, pos_tiles, h1, ys, route]
        in_specs += [pl.BlockSpec((1, 1, 2 * tm), lambda i: (i, 0, 0), memory_space=pltpu.SMEM),
                     pl.BlockSpec((1, 1, 2 * tm), lambda i: (jnp.minimum(i + 1, last_tile), 0, 0),
                                  memory_space=pltpu.SMEM),
                     pl.BlockSpec((tm, D_MODEL), row), pl.BlockSpec(memory_space=pl.ANY), pl.BlockSpec((tm, LANES), row)]
        scratch = [pltpu.VMEM((2, 2, tm, D_MODEL), F32), pltpu.SemaphoreType.DMA((2,))]
    else:
        ins += [h1, ys]
        in_specs += [pl.BlockSpec((tm, D_MODEL), row), pl.BlockSpec((tm, D_MODEL), row)]
    ins += [p.arr, gw.arr, pw.arr]
    in_specs += [pl.BlockSpec((None, tm, 256), lambda i: (p.idx, i, 0)), _wspec(gw), _wspec(pw)]
    out_specs = [pl.BlockSpec((tm, D_MODEL), row)]
    out_shape = [jax.ShapeDtypeStruct((t, D_MODEL), F32)]
    if final:
        ins.append(fg)
        in_specs.append(pl.BlockSpec(fg.shape, const))
        out_specs.append(pl.BlockSpec((tm, D_MODEL), row))
        out_shape.append(jax.ShapeDtypeStruct((t, D_MODEL), F32))
    return pl.pallas_call(
        functools.partial(_ple_kernel, moe=moe, final=final, tm=tm),
        grid=(t // tm,),
        in_specs=in_specs, out_specs=out_specs, out_shape=out_shape, scratch_shapes=scratch,
        compiler_params=_params(("arbitrary",)),
        name="ple",
    )(*ins)


def _tile(n, pref):
    return pref if n % pref == 0 else n


def _front(h, lw, nb, seq, cache, cprev, pprev):
    t = nb * seq
    tm = _tile(seq, 512)
    if cache is None:
        c0 = jnp.zeros((nb, 1, LANES), F32)
        past = 0
    else:
        ckt, cvt, clf = cache
        past = ckt.arr.shape[-1]
        akct, c0 = _cache_sum(clf, nb, past, _tile(past, 512))
    outs = _in_mix(h, lw, c0, cprev, pprev, nb, seq, tm, min(seq, 128), past, cache is not None,
                   kv_transposed=cache is None and seq % LANES == 0)
    qs, kf, vf, kb, vb, lf, aq, ak, bcd, cst, pst = outs[:11]
    sgv = outs[11] if cache is not None else None
    if cache is None:
        res = _attn_out(qs, aq, kb, ak, vb, bcd, h, lw['w_o'], lw['g2'], lw.get('router'), nb, seq, tm)
    else:
        a = _attention_rows(qs, aq, kb, ak, vb, ckt, akct, cvt, nb, seq, past)
        res = _out_proj(a, bcd, h, lw['w_o'], lw['g2'], lw.get('router'), _tile(t, 512))
    return res, (kf, vf, lf[:, :N_HEADS], cst[:, 2:], pst[:, 1:], sgv)


def _pos_tiles(pos, tm):
    nt = pos[0].shape[0] // tm
    return jnp.stack([pos[0].reshape(nt, tm), pos[1].reshape(nt, tm)], axis=1).reshape(nt, 1, 2 * tm)


N_FFN_INPUTS = 9


def _ffn_ple_kernel(*refs, final, tm):
    ffn_in, rest = refs[:N_FFN_INPUTS], refs[N_FFN_INPUTS:]
    y_sc = rest[-1]
    _ffn_kernel(*ffn_in, y_sc)
    _ple_kernel(rest[0], y_sc, *rest[1:-1], moe=False, final=final, tm=tm)


def _back_dense(res, p_l, lw, final_g):
    h1, hn = res
    t = h1.shape[0]
    tm = _tile(t, 256)
    nt = t // tm
    final = final_g is not None
    zeros = jnp.zeros((nt,), jnp.int32)
    tabs = (jnp.arange(nt, dtype=jnp.int32), zeros, zeros, zeros + tm, zeros + 1)
    wg, wu, wd, gw, pw = lw['wg'], lw['wu'], lw['wd'], lw['gw'], lw['plw_e']
    d_ff = wg.arr.shape[-1]
    row = lambda w, *_: (w, 0)
    wmap = lambda w, *_: (wg.idx, 0, 0, 0)
    ins = [hn, wg.arr, wu.arr, wd.arr, h1, p_l.arr, gw.arr, pw.arr]
    in_specs = [pl.BlockSpec((tm, D_MODEL), row),
                pl.BlockSpec((None, 1, D_MODEL, d_ff), wmap), pl.BlockSpec((None, 1, D_MODEL, d_ff), wmap),
                pl.BlockSpec((None, 1, d_ff, D_MODEL), wmap),
                pl.BlockSpec((tm, D_MODEL), row), pl.BlockSpec((None, tm, 256), lambda w, *_: (p_l.idx, w, 0)),
                _wspec(gw), _wspec(pw)]
    out_specs = [pl.BlockSpec((tm, D_MODEL), row)]
    out_shape = [jax.ShapeDtypeStruct((t, D_MODEL), F32)]
    if final:
        ins.append(final_g)
        in_specs.append(pl.BlockSpec(final_g.shape, lambda w, *_: (0, 0)))
        out_specs.append(pl.BlockSpec((tm, D_MODEL), row))
        out_shape.append(jax.ShapeDtypeStruct((t, D_MODEL), F32))
    grid_spec = pltpu.PrefetchScalarGridSpec(
        num_scalar_prefetch=5, grid=(nt,), in_specs=in_specs, out_specs=out_specs,
        scratch_shapes=[pltpu.VMEM((tm, D_MODEL), F32)])
    return pl.pallas_call(
        functools.partial(_ffn_ple_kernel, final=final, tm=tm),
        grid_spec=grid_spec,
        out_shape=out_shape,
        compiler_params=_params(("arbitrary",)),
        name="ffn_ple",
    )(*tabs, *ins)


def _back_moe(groups, lw, final_g):
    i32 = jnp.int32
    counts = [res[3][0, :N_EXPERTS].astype(i32) for res, _ in groups]
    n_rows = 2 * sum(res[0].shape[0] for res, _ in groups)
    tf = max(c for c in (256, 128, 64, 32, 16) if n_rows % c == 0)
    tabs, starts = _work_items(sum(counts), n_rows, tf)
    xs, pos_all, offset = None, [], jnp.zeros((N_EXPERTS,), i32)
    for (res, _), cnt in zip(groups, counts):
        h1, hn, route = res[:3]
        base = starts + offset
        pos = [base[route[:, R_EXPERT + s].astype(i32)] + route[:, R_RANK + s].astype(i32) for s in range(2)]
        offset = offset + cnt
        tmo = _tile(h1.shape[0], 512)
        xs = _dispatch(hn, _pos_tiles(pos, tmo), n_rows, tmo, xs)
        pos_all.append(pos)
    ys = _ffn(xs, lw['wg'], lw['wu'], lw['wd'], tabs, tf)
    outs = []
    for (res, p_l), pos in zip(groups, pos_all):
        tp = _tile(res[0].shape[0], 256)
        outs.append(_ple(res[0], ys, res[2], _pos_tiles(pos, tp), p_l, lw['gw'], lw['plw_e'], final_g, tp))
    return outs


def kernel(x_prompt, x_sample, p_prompt, p_sample, cache_k, cache_v, cache_logf, state_conv, state_pool, norm1_g, w_in, b_f, conv_dw_w, conv_dw_b, conv_ln_g, conv_ln_b, conv_pw_w, pool_w, pool_scale, sg_w, sg_b, w_o, norm2_g, ffn_w_gate, ffn_w_up, ffn_w_down, moe_router, moe_w_gate, moe_w_up, moe_w_down, ple_w, ple_gate_w, final_g):
    depth = w_in.shape[0]
    bp, sp, _ = x_prompt.shape
    bs, ss, _ = x_sample.shape
    past = cache_k.shape[2]
    hp = x_prompt.reshape(bp * sp, D_MODEL)
    hs = x_sample.reshape(bs * ss, D_MODEL)
    fg = final_g.reshape(1, D_MODEL)

    row = lambda x: x.reshape(depth, 1, -1)
    w_r = jnp.concatenate([w_in[..., 0:768], w_in[..., 772:2052],
                           jnp.pad(w_in[..., 768:772], ((0, 0), (0, 0), (0, LANES - N_HEADS)))], axis=-1).astype(BF16)
    plw = jnp.zeros((depth, GROUP_W, GROUP_W), F32)
    for gi in range(len(POOL_WINDOWS)):
        plw = plw.at[:, 64 * gi:64 * (gi + 1), 64 * gi:64 * (gi + 1)].set(pool_w[:, gi])
    stacked = {
        'g1': row(norm1_g), 'w_in': w_r, 'bf': row(jnp.pad(b_f, ((0, 0), (0, LANES - N_HEADS)))),
        'dw': jnp.pad(conv_dw_w, ((0, 0), (0, 1), (0, 0))), 'dwb': row(conv_dw_b),
        'lng': row(conv_ln_g), 'lnb': row(conv_ln_b), 'pw': conv_pw_w.astype(BF16), 'plw': plw.astype(BF16),
        'psc': row(pool_scale), 'w_o': w_o.astype(BF16), 'g2': row(norm2_g),
        'gw': ple_gate_w.astype(BF16), 'plw_e': ple_w.astype(BF16),
    }
    dense = {'wg': ffn_w_gate.astype(BF16)[:, None], 'wu': ffn_w_up.astype(BF16)[:, None],
             'wd': ffn_w_down.astype(BF16)[:, None]}
    sparse = {'wg': moe_w_gate.astype(BF16), 'wu': moe_w_up.astype(BF16), 'wd': moe_w_down.astype(BF16),
              'router': jnp.pad(moe_router, ((0, 0), (0, 0), (0, LANES - N_EXPERTS))).astype(BF16)}

    def sg_params(tc):
        return {'sgw': sg_w[:, :, :tc, :tc],
                'sgb': jnp.repeat(jnp.swapaxes(sg_b[:, :, :tc], 1, 2), GROUP_W // SG_GROUPS, axis=2)}

    sg_p, sg_s = sg_params(min(sp, 128)), sg_params(min(ss, 128))
    pp_all = p_prompt.reshape(depth, bp * sp, -1)
    ps_all = p_sample.reshape(depth, bs * ss, -1)
    ck_all = jnp.transpose(cache_k, (0, 1, 3, 4, 2)).reshape(depth, bs * GROUP_W, past)
    cv_all = jnp.transpose(cache_v, (0, 1, 3, 4, 2)).reshape(depth, bs * GROUP_W, past)
    clf_all = jnp.pad(cache_logf, ((0, 0), (0, 0), (0, 0), (0, LANES - N_HEADS))).reshape(depth, bs * past, LANES)
    sc_all = jnp.pad(state_conv, ((0, 0), (0, 0), (2, 0), (0, 0)))
    sp_all = jnp.pad(state_pool, ((0, 0), (0, 0), (1, 0), (0, 0)))
    cz, pz = jnp.zeros((bp, 32, GROUP_W), F32), jnp.zeros((bp, 16, GROUP_W), F32)

    outs_p, outs_s = [], []
    for l in range(depth):
        ffn_w = dense if l % 2 == 0 else sparse

        def layer_weights(sg):
            lw = {n: _LW(a, l) for n, a in {**stacked, **sg}.items()}
            lw.update({n: _LW(a, l // 2) for n, a in ffn_w.items()})
            return lw

        last = fg if l == depth - 1 else None
        lwp, lws = layer_weights(sg_p), layer_weights(sg_s)
        res_p, outs = _front(hp, lwp, bp, sp, None, cz, pz)
        outs_p.append(outs)
        res_s, outs = _front(hs, lws, bs, ss, (_LW(ck_all, l), _LW(cv_all, l), clf_all[l]), sc_all[l], sp_all[l])
        outs_s.append(outs)
        if l % 2 == 0:
            hp_res = _back_dense(res_p, _LW(pp_all, l), lwp, last)
            hs_res = _back_dense(res_s, _LW(ps_all, l), lws, last)
        else:
            hp_res, hs_res = _back_moe([(res_p, _LW(pp_all, l)), (res_s, _LW(ps_all, l))], lwp, last)
        if last is None:
            hp, hs = hp_res[0], hs_res[0]
        else:
            y_prompt, y_sample = hp_res[1], hs_res[1]

    def stack(outs, i, shape):
        return jnp.stack([o[i] for o in outs]).reshape((depth,) + shape)

    def stack_kv(i):
        if outs_p[0][i].shape == (bp * sp, GROUP_W):
            return stack(outs_p, i, (bp, sp, N_HEADS, HEAD_DIM))
        kv = stack(outs_p, i, (bp, N_HEADS, HEAD_DIM, sp))
        return jnp.transpose(kv, (0, 1, 4, 2, 3))

    return (y_prompt.reshape(bp, sp, D_MODEL), y_sample.reshape(bs, ss, D_MODEL),
            stack_kv(0), stack_kv(1),
            stack(outs_p, 2, (bp, sp, N_HEADS)), stack(outs_p, 3, (bp, CONV_W - 1, GROUP_W)),
            stack(outs_p, 4, (bp, POOL_HIST, GROUP_W)),
            stack(outs_s, 0, (bs, ss, N_HEADS, HEAD_DIM)), stack(outs_s, 1, (bs, ss, N_HEADS, HEAD_DIM)),
            stack(outs_s, 2, (bs, ss, N_HEADS)), stack(outs_s, 3, (bs, CONV_W - 1, GROUP_W)),
            stack(outs_s, 4, (bs, POOL_HIST, GROUP_W)), stack(outs_s, 5, (bs, ss, GROUP_W)))
```

```python
import functools
from typing import NamedTuple

import numpy as np
import jax
import jax.numpy as jnp
from jax import lax
from jax.experimental import pallas as pl
from jax.experimental.pallas import tpu as pltpu

F32 = jnp.float32
BF16 = jnp.bfloat16
EPS = 1e-6

D_MODEL = 1024
GROUP_W = 256
N_HEADS = 4
HEAD_DIM = 64
CONV_W = 31
POOL_WINDOWS = (2, 4, 8, 16)
POOL_HIST = 15
SG_GROUPS = 4
N_EXPERTS = 8
LANES = 128
AUG_PER_HEAD = 6
VMEM_LIMIT = 56 * 1024 * 1024


def _params(sem, vmem=VMEM_LIMIT):
    return pltpu.CompilerParams(dimension_semantics=sem, vmem_limit_bytes=vmem)


def _dot(a, b):
    return jnp.dot(a, b, preferred_element_type=F32)


class _LW(NamedTuple):
    arr: jax.Array
    idx: int


def _wspec(w):
    nd = w.arr.ndim - 1
    return pl.BlockSpec((None,) + w.arr.shape[1:], lambda *_: (w.idx,) + (0,) * nd)


def _split3(x):
    hi = x.astype(BF16)
    r1 = x - hi.astype(F32)
    mid = r1.astype(BF16)
    lo = (r1 - mid.astype(F32)).astype(BF16)
    return jnp.concatenate([hi, mid, lo], axis=1)


def _running_sum(x, tril, carry_ref):
    cs = _dot(tril, _split3(x))
    c = carry_ref[...] + (cs[:, :LANES] + cs[:, LANES:2 * LANES] + cs[:, 2 * LANES:])
    n = x.shape[0]
    carry_ref[...] = c[n - 1:n, :]
    return c


def _placement(sign_k):
    pq = np.zeros((3 * LANES, 2 * LANES), np.float32)
    pk = np.zeros((3 * LANES, 2 * LANES), np.float32)
    oq = np.zeros((1, 2 * LANES), np.float32)
    ok = np.zeros((1, 2 * LANES), np.float32)
    for h in range(N_HEADS):
        p, j = divmod(h, 2)
        for i in range(3):
            pq[i * LANES + h, p * LANES + j * AUG_PER_HEAD + i] = 1.0
            oq[0, p * LANES + j * AUG_PER_HEAD + 3 + i] = 1.0
            pk[i * LANES + h, p * LANES + j * AUG_PER_HEAD + 3 + i] = sign_k
            ok[0, p * LANES + j * AUG_PER_HEAD + i] = 1.0
    return pq, pk, oq, ok


def _in_proj_kernel(h_ref, g_ref, w_ref, bf_ref, c0_ref, tril_ref, pq_ref, pk_ref, oq_ref, ok_ref,
                    qs_ref, kf_ref, vf_ref, kb_ref, vb_ref, lf_ref, aq_ref, ak_ref, zc_ref, carry_ref,
                    kv_transposed=False):
    @pl.when(pl.program_id(1) == 0)
    def _():
        carry_ref[...] = c0_ref[0]

    x = h_ref[...]
    hn = x * lax.rsqrt(jnp.mean(x * x, axis=-1, keepdims=True) + EPS) * g_ref[...]
    hb = hn.astype(BF16)
    qs_ref[...] = (_dot(hb, w_ref[:, 0:256]) * (HEAD_DIM ** -0.5)).astype(BF16)
    k = _dot(hb, w_ref[:, 256:512])
    kf_ref[...] = k.T if kv_transposed else k
    kb_ref[...] = k.astype(BF16)
    v = _dot(hb, w_ref[:, 512:768])
    vf_ref[...] = v.T if kv_transposed else v
    vb_ref[...] = v.astype(BF16)
    zc_ref[...] = _dot(hb, w_ref[:, 768:2048])
    zf = _dot(hb, w_ref[:, 2048:2176]) + bf_ref[...]
    lf = jnp.minimum(zf, 0.0) - jnp.log1p(jnp.exp(-jnp.abs(zf)))
    lane = lax.broadcasted_iota(jnp.int32, lf.shape, 1)
    lf = jnp.where(lane < N_HEADS, lf, 0.0)
    lf_ref[...] = lf
    cp = _split3(_running_sum(lf, tril_ref[...], carry_ref))
    aq_ref[...] = (_dot(cp, pq_ref[...]) + oq_ref[...]).astype(BF16)
    ak_ref[...] = (_dot(cp, pk_ref[...]) + ok_ref[...]).astype(BF16)


N_IN_PROJ_INPUTS, N_IN_PROJ_OUTPUTS, N_MIX_INPUTS = 10, 8, 11


def _in_mix_kernel(*refs, ts, tc, pos0, want_sgv, kv_transposed, n_aliased):
    proj_in = refs[:N_IN_PROJ_INPUTS]
    mix_in = refs[N_IN_PROJ_INPUTS:N_IN_PROJ_INPUTS + N_MIX_INPUTS]
    rest = refs[N_IN_PROJ_INPUTS + N_MIX_INPUTS + n_aliased:]
    proj_out, rest = rest[:N_IN_PROJ_OUTPUTS], rest[N_IN_PROJ_OUTPUTS:]
    bcd_ref, cst_ref, pst_ref = rest[:3]
    sgv_ref = rest[3] if want_sgv else None
    carry_ref, zc_sc, xc_ref, xp_ref = rest[3 + want_sgv:]
    _in_proj_kernel(*proj_in, *proj_out, zc_sc, carry_ref, kv_transposed=kv_transposed)
    z = [zc_sc.at[:, j * GROUP_W:(j + 1) * GROUP_W] for j in range(5)]
    _mixers_kernel(*z, *mix_in, bcd_ref, cst_ref, pst_ref, sgv_ref, xc_ref, xp_ref, ts=ts, tc=tc, pos0=pos0)


def _in_mix(h, lw, c0, cprev, pprev, nb, seq, ts, tc, pos0, want_sgv, kv_transposed, kv_stack=None):
    t = nb * seq
    ns = seq // ts
    tril = jnp.asarray(np.tril(np.ones((ts, ts), np.float32)), BF16)
    pq, pk, oq, ok = _placement(-1.0)
    row = lambda b, s: (b * ns + s, 0)
    const = lambda b, s: (0, 0)
    blk = lambda w_: pl.BlockSpec((ts, w_), row)
    full = lambda a: pl.BlockSpec(a.shape, const)
    perb = lambda r: pl.BlockSpec((1, r, GROUP_W), lambda b, s: (b, 0, 0))
    consts = [jnp.asarray(pq, BF16), jnp.asarray(pk, BF16), jnp.asarray(oq), jnp.asarray(ok)]
    small = [lw[n] for n in ('dw', 'dwb', 'lng', 'lnb', 'pw', 'plw', 'psc', 'sgw', 'sgb')]
    outs = [(256, BF16), (256, F32), (256, F32), (256, BF16), (256, BF16), (LANES, F32), (256, BF16), (256, BF16),
            (768, BF16)]
    out_specs = [blk(w_) for w_, _ in outs] + [perb(32), perb(16)]
    out_shape = [jax.ShapeDtypeStruct((t, w_), dt) for w_, dt in outs] + [
        jax.ShapeDtypeStruct((nb, 32, GROUP_W), F32), jax.ShapeDtypeStruct((nb, 16, GROUP_W), F32)]
    ins = [h, lw['g1'].arr, lw['w_in'].arr, lw['bf'].arr, c0, tril, *consts, cprev, pprev, *[a.arr for a in small]]
    in_specs = ([blk(D_MODEL), _wspec(lw['g1']), _wspec(lw['w_in']), _wspec(lw['bf']),
                 pl.BlockSpec((1, 1, LANES), lambda b, s: (b, 0, 0)), full(tril)] + [full(a) for a in consts]
                + [perb(32), perb(16)] + [_wspec(a) for a in small])
    aliases = {}
    if kv_transposed:
        depth, layer, bufs = kv_stack
        for i in (1, 2):
            out_specs[i] = pl.BlockSpec((GROUP_W, ts), lambda b, s: (layer * nb + b, s))
            out_shape[i] = jax.ShapeDtypeStruct((depth * nb * GROUP_W, seq), F32)
        if bufs is not None:
            aliases = {len(ins): 1, len(ins) + 1: 2}
            ins += list(bufs)
            in_specs += [pl.BlockSpec(memory_space=pl.ANY)] * 2
    if want_sgv:
        out_specs.append(blk(GROUP_W))
        out_shape.append(jax.ShapeDtypeStruct((t, GROUP_W), F32))
    return pl.pallas_call(
        functools.partial(_in_mix_kernel, ts=ts, tc=tc, pos0=pos0, want_sgv=want_sgv, kv_transposed=kv_transposed,
                          n_aliased=len(aliases)),
        grid=(nb, ns),
        in_specs=in_specs,
        out_specs=out_specs,
        out_shape=out_shape,
        scratch_shapes=[pltpu.VMEM((1, LANES), F32), pltpu.VMEM((ts, 5 * GROUP_W), F32),
                        pltpu.VMEM((40 + ts, GROUP_W), F32), pltpu.VMEM((16 + ts, GROUP_W), F32)],
        input_output_aliases=aliases,
        compiler_params=_params(("arbitrary", "arbitrary")),
        name="in_mix",
    )(*ins)


def _cache_sum_kernel(lf_ref, tril_ref, pk_ref, ok_ref, ak_ref, tot_ref, carry_ref):
    @pl.when(pl.program_id(1) == 0)
    def _():
        carry_ref[...] = jnp.zeros_like(carry_ref)

    cp = _split3(_running_sum(lf_ref[...], tril_ref[...], carry_ref))
    ak_ref[...] = (_dot(cp, pk_ref[...]) + ok_ref[...]).T.astype(BF16)
    tot_ref[0] = carry_ref[...]


def _cache_sum(lfp, nb, past, tm):
    ns = past // tm
    tril = jnp.asarray(np.tril(np.ones((tm, tm), np.float32)), BF16)
    _, pk, _, ok = _placement(-1.0)
    pk, ok = jnp.asarray(pk, BF16), jnp.asarray(ok)
    const = lambda b, s: (0, 0)
    return pl.pallas_call(
        _cache_sum_kernel,
        grid=(nb, ns),
        in_specs=[pl.BlockSpec((tm, LANES), lambda b, s: (b * ns + s, 0)), pl.BlockSpec(tril.shape, const),
                  pl.BlockSpec(pk.shape, const), pl.BlockSpec(ok.shape, const)],
        out_specs=[pl.BlockSpec((256, tm), lambda b, s: (b, s)),
                   pl.BlockSpec((1, 1, LANES), lambda b, s: (b, 0, 0))],
        out_shape=[jax.ShapeDtypeStruct((nb * 256, past), BF16), jax.ShapeDtypeStruct((nb, 1, LANES), F32)],
        scratch_shapes=[pltpu.VMEM((1, LANES), F32)],
        compiler_params=_params(("arbitrary", "arbitrary")),
        name="cache_sum",
    )(lfp, tril, pk, ok)


NEG_BIG = -1e30


def _head_lhs(q, aq, j, lane):
    zero = jnp.zeros((), BF16)
    qm = jnp.where((lane >= HEAD_DIM * j) & (lane < HEAD_DIM * (j + 1)), q, zero)
    am = jnp.where((lane >= AUG_PER_HEAD * j) & (lane < AUG_PER_HEAD * (j + 1)), aq, zero)
    return jnp.concatenate([qm, am], axis=1)


def _scores(lhs, k, ak):
    rhs = jnp.concatenate([k, ak], axis=1)
    return lax.dot_general(lhs, rhs, (((1,), (1,)), ((), ())), preferred_element_type=F32)


def _loop_by_two(n, fn):
    def body(i, carry):
        fn(2 * i)
        fn(2 * i + 1)
        return carry

    lax.fori_loop(0, lax.shift_right_logical(n, 1), body, 0)

    @pl.when((n & 1) == 1)
    def _():
        fn(n - 1)


def _attn_kernel(q_ref, aq_ref, k_ref, ak_ref, v_ref, o_ref, m_sc, l_sc, acc_sc, e_sc, f_sc, *, t):
    qi = pl.program_id(1)
    lane = lax.broadcasted_iota(jnp.int32, (1, LANES), 1)
    row = lax.broadcasted_iota(jnp.int32, (t, t), 0)
    col = lax.broadcasted_iota(jnp.int32, (t, t), 1)
    nct = t // LANES
    for p in range(2):
        cs = slice(p * LANES, (p + 1) * LANES)
        lhs = [_head_lhs(q_ref[:, cs], aq_ref[:, cs], j, lane) for j in range(2)]

        def scores(j, ki, masked):
            rows = pl.ds(pl.multiple_of(ki * t, t), t)
            s = _scores(lhs[j], k_ref[rows, cs], ak_ref[rows, cs])
            if masked:
                s = jnp.where(col <= row, s, -jnp.inf)
            return [s[:, c * LANES:(c + 1) * LANES] for c in range(nct)]

        def stats(ki, masked):
            for j in range(2):
                parts = scores(j, ki, masked)
                m_old = m_sc[j]
                m_new = functools.reduce(jnp.maximum, parts, m_old)
                tot = l_sc[j] * jnp.exp(m_old - m_new)
                for c, part in enumerate(parts):
                    e = jnp.exp(part - m_new)
                    e_sc[j, ki, :, c * LANES:(c + 1) * LANES] = e
                    tot = tot + e
                l_sc[j] = tot
                m_sc[j] = m_new
                f_sc[j, ki] = m_new

        m_sc[...] = jnp.full_like(m_sc, NEG_BIG)
        l_sc[...] = jnp.zeros_like(l_sc)
        _loop_by_two(qi, lambda ki: stats(ki, False))
        stats(qi, True)
        for j in range(2):
            m_lane = m_sc[j]
            m_row = jnp.max(m_lane, axis=1, keepdims=True)
            l_row = jnp.sum(l_sc[j] * jnp.exp(m_lane - m_row), axis=1, keepdims=True)
            m_sc[j] = jnp.broadcast_to(m_row, m_lane.shape)
            l_sc[j] = jnp.broadcast_to(1.0 / l_row, m_lane.shape)

        acc_sc[...] = jnp.zeros_like(acc_sc)

        def accumulate(ki):
            v = v_ref[pl.ds(pl.multiple_of(ki * t, t), t), cs]
            for j in range(2):
                f = jnp.exp(f_sc[j, ki] - m_sc[j]) * l_sc[j]
                prob = jnp.concatenate([(e_sc[j, ki, :, c * LANES:(c + 1) * LANES] * f).astype(BF16)
                                        for c in range(nct)], axis=1)
                acc_sc[j] += _dot(prob, v)

        _loop_by_two(qi + 1, accumulate)
        o_ref[:, cs] = jnp.where(lane < HEAD_DIM, acc_sc[0], acc_sc[1]).astype(BF16)


N_ATTN_INPUTS, N_ATTN_SCRATCH = 5, 5


def _attn_out_kernel(*refs, t, moe):
    attn_in, rest = refs[:N_ATTN_INPUTS], refs[N_ATTN_INPUTS:]
    n_proj = 4 + 2 * moe + 2 + 2 * moe
    proj, rest = rest[:n_proj], rest[n_proj:]
    attn_sc, a_sc, carry = rest[:N_ATTN_SCRATCH], rest[N_ATTN_SCRATCH], rest[N_ATTN_SCRATCH + 1:]
    _attn_kernel(*attn_in, a_sc, *attn_sc, t=t)
    first = (pl.program_id(0) == 0) & (pl.program_id(1) == 0)
    _out_proj_kernel(a_sc, *proj, *carry, moe=moe, first_step=first)


def _attn_out(qs, aq, kb, ak, vb, bcd, h, wo, g2, router, nb, seq, t):
    assert seq % t == 0 and t % LANES == 0
    nq = seq // t
    tt = nb * seq
    moe = router is not None
    row = lambda b, i: (b * nq + i, 0)
    const = lambda b, i: (0, 0)
    qblk = pl.BlockSpec((t, 256), row)
    kblk = pl.BlockSpec((seq, 256), lambda b, i: (b, 0))
    ins = [qs, aq, kb, ak, vb, bcd, h, wo.arr, g2.arr]
    in_specs = [qblk, qblk, kblk, kblk, kblk, pl.BlockSpec((t, 768), row), pl.BlockSpec((t, D_MODEL), row),
                _wspec(wo), _wspec(g2)]
    out_specs = [pl.BlockSpec((t, D_MODEL), row), pl.BlockSpec((t, D_MODEL), row)]
    out_shape = [jax.ShapeDtypeStruct((tt, D_MODEL), F32), jax.ShapeDtypeStruct((tt, D_MODEL), F32 if moe else BF16)]
    scratch = [pltpu.VMEM((2, t, LANES), F32)] * 3 + [pltpu.VMEM((2, nq, t, t), F32),
                                                      pltpu.VMEM((2, nq, t, LANES), F32), pltpu.VMEM((t, 256), BF16)]
    if moe:
        stril = jnp.asarray(np.tril(np.ones((t, t), np.float32), -1), BF16)
        ins += [router.arr, stril]
        in_specs += [_wspec(router), pl.BlockSpec(stril.shape, const)]
        out_specs += [pl.BlockSpec((t, LANES), row), pl.BlockSpec((1, LANES), const)]
        out_shape += [jax.ShapeDtypeStruct((tt, LANES), F32), jax.ShapeDtypeStruct((1, LANES), F32)]
        scratch.append(pltpu.VMEM((1, LANES), F32))
    return pl.pallas_call(
        functools.partial(_attn_out_kernel, t=t, moe=moe),
        grid=(nb, nq),
        in_specs=in_specs, out_specs=out_specs, out_shape=out_shape, scratch_shapes=scratch,
        compiler_params=_params(("arbitrary", "arbitrary")),
        name="attn_out",
    )(*ins)


def _attn_rows_kernel(q_ref, aq_ref, k_ref, ak_ref, v_ref, ckt_ref, akct_ref, cvt_ref, o_ref):
    lq = q_ref.shape[0]
    lane = lax.broadcasted_iota(jnp.int32, (1, LANES), 1)
    row = lax.broadcasted_iota(jnp.int32, (lq, lq), 0)
    col = lax.broadcasted_iota(jnp.int32, (lq, lq), 1)
    nt = (((1,), (1,)), ((), ()))
    for p in range(2):
        cs = slice(p * LANES, (p + 1) * LANES)
        cache_rhs = jnp.concatenate([ckt_ref[cs, :].astype(BF16), akct_ref[cs, :]], axis=0)
        cache_v = cvt_ref[cs, :].astype(BF16)
        outs = []
        for j in range(2):
            lhs = _head_lhs(q_ref[:, cs], aq_ref[:, cs], j, lane)
            s_c = _dot(lhs, cache_rhs)
            s_n = jnp.where(col <= row, _scores(lhs, k_ref[:, cs], ak_ref[:, cs]), -jnp.inf)
            m = jnp.maximum(jnp.max(s_c, axis=1, keepdims=True), jnp.max(s_n, axis=1, keepdims=True))
            e_c, e_n = jnp.exp(s_c - m), jnp.exp(s_n - m)
            r = 1.0 / (jnp.sum(e_c, axis=1, keepdims=True) + jnp.sum(e_n, axis=1, keepdims=True))
            outs.append(lax.dot_general((e_c * r).astype(BF16), cache_v, nt, preferred_element_type=F32)
                        + _dot((e_n * r).astype(BF16), v_ref[:, cs]))
        o_ref[:, cs] = jnp.where(lane < HEAD_DIM, outs[0], outs[1]).astype(BF16)


def _attention_rows(qs, aq, kb, ak, vb, ckt, akct, cvt, nb, lq, past):
    qblk = pl.BlockSpec((lq, 256), lambda b: (b, 0))
    cblk = pl.BlockSpec((256, past), lambda b: (b, 0))
    lblk = lambda w: pl.BlockSpec((None, 256, past), lambda b: (w.idx, b, 0))
    return pl.pallas_call(
        _attn_rows_kernel,
        grid=(nb,),
        in_specs=[qblk] * 5 + [lblk(ckt), cblk, lblk(cvt)],
        out_specs=qblk,
        out_shape=jax.ShapeDtypeStruct((nb * lq, 256), BF16),
        compiler_params=_params(("arbitrary",)),
        name="attn_rows",
    )(qs, aq, kb, ak, vb, ckt.arr, akct, cvt.arr)


ROW_CHUNK = 64


def _gelu(x):
    return 0.5 * x * (1.0 + jnp.tanh(0.7978845608028654 * (x + 0.044715 * (x * x * x))))


def _mixers_kernel(ca_ref, cg_ref, px_ref, su_ref, sv_ref, cprev_ref, pprev_ref, dw_ref, dwb_ref, lng_ref, lnb_ref,
                   pw_ref, plw_ref, psc_ref, sgw_ref, sgb_ref,
                   bcd_ref, cst_ref, pst_ref, sgv_ref, xc_ref, xp_ref, *, ts, tc, pos0):
    s = pl.program_id(1)

    @pl.when(s == 0)
    def _():
        xc_ref[0:32, :] = cprev_ref[0].astype(BF16).astype(F32)
        xp_ref[0:16, :] = pprev_ref[0]

    @pl.when(s > 0)
    def _():
        xc_ref[0:32, :] = xc_ref[ts:ts + 32, :]
        xp_ref[0:16, :] = xp_ref[ts:ts + 16, :]

    g = cg_ref[...]
    u = ca_ref[...] * (1.0 / (1.0 + jnp.exp(-g)))
    xc_ref[32:32 + ts, :] = u.astype(BF16).astype(F32)
    xc_ref[32 + ts:40 + ts, :] = jnp.zeros((8, GROUP_W), F32)
    cst_ref[0] = u[ts - 32:ts, :]
    xp_ref[16:16 + ts, :] = px_ref[...]
    pst_ref[0] = xp_ref[ts:ts + 16, :]

    rc = min(ts, ROW_CHUNK)
    lane = lax.broadcasted_iota(jnp.int32, (rc, GROUP_W), 1)
    conv_act, pool_dlt = [], []
    for r0 in range(0, ts, rc):
        y = jnp.zeros((rc, GROUP_W), F32)
        for r in range(8):
            part = jnp.zeros((rc + 8, GROUP_W), F32)
            for a in range(r if r >= 2 else r + 8, CONV_W + 2, 8):
                part = part + xc_ref[r0 + a - r:r0 + a - r + rc + 8, :] * dw_ref[a - 2:a - 1, :]
            y = y + part[r:r + rc, :]
        y = y + dwb_ref[...]
        mu = jnp.mean(y, axis=-1, keepdims=True)
        var = jnp.mean(jnp.square(y - mu), axis=-1, keepdims=True)
        y = (y - mu) * lax.rsqrt(var + EPS) * lng_ref[...] + lnb_ref[...]
        conv_act.append((y * (1.0 / (1.0 + jnp.exp(-y)))).astype(BF16))

        x = xp_ref[16 + r0:16 + r0 + rc, :]
        pos = (pos0 + s * ts + r0 + lax.broadcasted_iota(jnp.int32, (rc, 1), 0)).astype(F32)
        acc = x
        mean = jnp.zeros((rc, GROUP_W), F32)
        wi = 0
        for i in range(1, POOL_WINDOWS[-1] + 1):
            if i > 1:
                acc = acc + xp_ref[16 - (i - 1) + r0:16 - (i - 1) + r0 + rc, :]
            if i == POOL_WINDOWS[wi]:
                cnt = jnp.minimum(pos + 1.0, float(i))
                mean = jnp.where((lane >= 64 * wi) & (lane < 64 * (wi + 1)), acc / cnt, mean)
                wi += 1
        pool_dlt.append((mean - x).astype(BF16))
    bcd_ref[:, 0:256] = _dot(jnp.concatenate(conv_act, axis=0), pw_ref[...]).astype(BF16)
    bcd_ref[:, 256:512] = (_dot(jnp.concatenate(pool_dlt, axis=0), plw_ref[...]) * psc_ref[...]).astype(BF16)

    r = lax.broadcasted_iota(jnp.int32, (tc, tc), 0)
    c = lax.broadcasted_iota(jnp.int32, (tc, tc), 1)
    lane_c = lax.broadcasted_iota(jnp.int32, (tc, GROUP_W), 1)
    wts = [jnp.where(c <= r, sgw_ref[gi], 0.0).astype(BF16) for gi in range(SG_GROUPS)]
    for ci in range(ts // tc):
        uu = _gelu(su_ref[ci * tc:(ci + 1) * tc, :])
        vv = _gelu(sv_ref[ci * tc:(ci + 1) * tc, :])
        if sgv_ref is not None:
            sgv_ref[ci * tc:(ci + 1) * tc, :] = vv
        m = sgb_ref[...]
        for gi in range(SG_GROUPS):
            vg = jnp.where((lane_c >= 64 * gi) & (lane_c < 64 * (gi + 1)), vv, 0.0).astype(BF16)
            m = m + _dot(wts[gi], vg)
        bcd_ref[ci * tc:(ci + 1) * tc, 512:768] = (uu * m).astype(BF16)


R_GATE, R_EXPERT, R_RANK = 0, 2, 4


def _out_proj_kernel(a_ref, bcd_ref, h_ref, wo_ref, g_ref, *rest, moe, first_step=None):
    if first_step is None:
        first_step = pl.program_id(0) == 0
    if moe:
        router_ref, stril_ref, h1_ref, hn_ref, route_ref, cnt_ref, carry_ref = rest
    else:
        h1_ref, hn_ref = rest
    mix = _dot(a_ref[...], wo_ref[0:256, :]) + _dot(bcd_ref[...], wo_ref[256:1024, :])
    h1 = h_ref[...] + mix
    hn = h1 * lax.rsqrt(jnp.mean(h1 * h1, axis=-1, keepdims=True) + EPS) * g_ref[...]
    h1_ref[...] = h1
    hn_ref[...] = hn.astype(hn_ref.dtype)
    if moe:
        @pl.when(first_step)
        def _():
            carry_ref[...] = jnp.zeros_like(carry_ref)

        logits = _dot(hn.astype(BF16), router_ref[...])
        lane = lax.broadcasted_iota(jnp.int32, logits.shape, 1)
        logits = jnp.where(lane < N_EXPERTS, logits, -jnp.inf)
        v1 = jnp.max(logits, axis=1, keepdims=True)
        i1 = jnp.min(jnp.where(logits == v1, lane, LANES), axis=1, keepdims=True)
        rest_l = jnp.where(lane == i1, -jnp.inf, logits)
        v2 = jnp.max(rest_l, axis=1, keepdims=True)
        i2 = jnp.min(jnp.where(rest_l == v2, lane, LANES), axis=1, keepdims=True)
        e = jnp.exp(v2 - v1)
        g1 = 1.0 / (1.0 + e)
        oh1, oh2 = lane == i1, lane == i2
        oh = jnp.where(oh1 | oh2, 1.0, 0.0)
        before = _dot(stril_ref[...], oh.astype(BF16)) + carry_ref[...]
        r1 = jnp.sum(jnp.where(oh1, before, 0.0), axis=1, keepdims=True)
        r2 = jnp.sum(jnp.where(oh2, before, 0.0), axis=1, keepdims=True)
        carry_ref[...] += jnp.sum(oh, axis=0, keepdims=True)
        cnt_ref[...] = carry_ref[...]
        rec = jnp.zeros_like(logits)
        for ln, val in ((R_GATE, g1), (R_GATE + 1, e * g1), (R_EXPERT, i1.astype(F32)), (R_EXPERT + 1, i2.astype(F32)),
                        (R_RANK, r1), (R_RANK + 1, r2)):
            rec = jnp.where(lane == ln, val, rec)
        route_ref[...] = rec


def _out_proj(a, bcd, h, wo, g2, router, tm):
    t = h.shape[0]
    moe = router is not None
    row = lambda i: (i, 0)
    const = lambda i: (0, 0)
    ins = [a, bcd, h, wo.arr, g2.arr]
    in_specs = [pl.BlockSpec((tm, 256), row), pl.BlockSpec((tm, 768), row), pl.BlockSpec((tm, D_MODEL), row),
                _wspec(wo), _wspec(g2)]
    out_specs = [pl.BlockSpec((tm, D_MODEL), row), pl.BlockSpec((tm, D_MODEL), row)]
    out_shape = [jax.ShapeDtypeStruct((t, D_MODEL), F32), jax.ShapeDtypeStruct((t, D_MODEL), F32 if moe else BF16)]
    scratch = []
    if moe:
        stril = jnp.asarray(np.tril(np.ones((tm, tm), np.float32), -1), BF16)
        ins += [router.arr, stril]
        in_specs += [_wspec(router), pl.BlockSpec(stril.shape, const)]
        out_specs += [pl.BlockSpec((tm, LANES), row), pl.BlockSpec((1, LANES), const)]
        out_shape += [jax.ShapeDtypeStruct((t, LANES), F32), jax.ShapeDtypeStruct((1, LANES), F32)]
        scratch = [pltpu.VMEM((1, LANES), F32)]
    return pl.pallas_call(
        functools.partial(_out_proj_kernel, moe=moe),
        grid=(t // tm,),
        in_specs=in_specs, out_specs=out_specs, out_shape=out_shape, scratch_shapes=scratch,
        compiler_params=_params(("arbitrary",)),
        name="out_proj",
    )(*ins)


DMA_UNROLL = 8


def _row_copy(src_ref, src_row, dst_ref, dst_row, sem):
    return pltpu.make_async_copy(src_ref.at[pl.ds(src_row, 1)], dst_ref.at[pl.ds(dst_row, 1)], sem)


def _dispatch_kernel(pos_ref, x_ref, *rest, tm):
    xs_ref, sem = rest[-2:]

    def issue(r, carry):
        for k in range(2):
            _row_copy(x_ref, r, xs_ref, pos_ref[0, 0, k * tm + r], sem).start(priority=k)
        return carry

    def drain(r, carry):
        for k in range(2):
            _row_copy(x_ref, r, xs_ref, pos_ref[0, 0, k * tm + r], sem).wait()
        return carry

    lax.fori_loop(0, tm, issue, 0, unroll=DMA_UNROLL)
    lax.fori_loop(0, tm, drain, 0, unroll=DMA_UNROLL)


def _dispatch(hn, pos_tiles, n_rows, tm, xs_prev=None):
    t = hn.shape[0]
    ins = [pos_tiles, hn]
    in_specs = [pl.BlockSpec((1, 1, 2 * tm), lambda i: (i, 0, 0), memory_space=pltpu.SMEM),
                pl.BlockSpec((tm, D_MODEL), lambda i: (i, 0))]
    aliases = {}
    if xs_prev is not None:
        ins.append(xs_prev)
        in_specs.append(pl.BlockSpec(memory_space=pl.ANY))
        aliases = {2: 0}
    return pl.pallas_call(
        functools.partial(_dispatch_kernel, tm=tm),
        grid=(t // tm,),
        in_specs=in_specs,
        out_specs=pl.BlockSpec(memory_space=pl.ANY),
        out_shape=jax.ShapeDtypeStruct((n_rows, D_MODEL), F32),
        scratch_shapes=[pltpu.SemaphoreType.DMA],
        input_output_aliases=aliases,
        compiler_params=_params(("arbitrary",)),
        name="dispatch",
    )(*ins)


def _ffn_kernel(wt_ref, we_ref, lo_ref, hi_ref, first_ref, x_ref, wg_ref, wu_ref, wd_ref, y_ref):
    w = pl.program_id(0)
    lo, hi = lo_ref[w], hi_ref[w]

    @pl.when(hi > lo)
    def _():
        rowi = lax.broadcasted_iota(jnp.int32, (x_ref.shape[0], 1), 0)
        x = jnp.where((rowi >= lo) & (rowi < hi), x_ref[...], jnp.zeros((), x_ref.dtype)).astype(BF16)
        gate = _dot(x, wg_ref[0])
        up = _dot(x, wu_ref[0])
        act = (gate * (1.0 / (1.0 + jnp.exp(-gate))) * up).astype(BF16)
        y = _dot(act, wd_ref[0])

        @pl.when(first_ref[w] == 1)
        def _():
            y_ref[...] = y

        @pl.when(first_ref[w] == 0)
        def _():
            y_ref[...] += y


def _ffn(xs, wg, wu, wd, tabs, tm):
    d_ff = wg.arr.shape[-1]
    xmap = lambda w, wt, we, lo, hi, fi: (wt[w], 0)
    wmap = lambda w, wt, we, lo, hi, fi: (wg.idx, we[w], 0, 0)
    grid_spec = pltpu.PrefetchScalarGridSpec(
        num_scalar_prefetch=5,
        grid=(tabs[0].shape[0],),
        in_specs=[pl.BlockSpec((tm, D_MODEL), xmap),
                  pl.BlockSpec((None, 1, D_MODEL, d_ff), wmap), pl.BlockSpec((None, 1, D_MODEL, d_ff), wmap),
                  pl.BlockSpec((None, 1, d_ff, D_MODEL), wmap)],
        out_specs=pl.BlockSpec((tm, D_MODEL), xmap),
    )
    return pl.pallas_call(
        _ffn_kernel,
        grid_spec=grid_spec,
        out_shape=jax.ShapeDtypeStruct((xs.shape[0], D_MODEL), F32),
        compiler_params=_params(("arbitrary",)),
        name="ffn",
    )(*tabs, xs, wg.arr, wu.arr, wd.arr)


def _work_items(counts, n_rows, tm):
    i32 = jnp.int32
    ends = jnp.cumsum(counts)
    starts = ends - counts
    t_first = starts // tm
    n_e = jnp.where(counts > 0, (ends - 1) // tm - t_first + 1, 0)
    w_end = jnp.cumsum(n_e)
    w_start = w_end - n_e
    n_work = n_rows // tm + counts.shape[0] - 1
    w = jnp.arange(n_work, dtype=i32)
    valid = w < w_end[-1]
    e = jnp.minimum(jnp.sum((w[:, None] >= w_end[None, :]).astype(i32), axis=1), counts.shape[0] - 1)
    tile = jnp.where(valid, t_first[e] + (w - w_start[e]), n_rows // tm - 1).astype(i32)
    lo = jnp.where(valid, jnp.maximum(starts[e] - tile * tm, 0), 0).astype(i32)
    hi = jnp.where(valid, jnp.minimum(ends[e] - tile * tm, tm), 0).astype(i32)
    e = jnp.where(valid, e, e[jnp.maximum(w_end[-1] - 1, 0)]).astype(i32)
    first = jnp.concatenate([jnp.ones((1,), i32), (tile[1:] != tile[:-1]).astype(i32)])
    return (tile, e, lo, hi, first), starts


def _ple_kernel(*refs, moe, final, tm):
    refs = list(refs)
    if moe:
        pos_ref, pos_next_ref = refs.pop(0), refs.pop(0)
    h1_ref, y_ref = refs[0], refs[1]
    k = 2
    if moe:
        route_ref = refs[k]
        k += 1
    p_ref, gw_ref, pw_ref = refs[k:k + 3]
    k += 3
    if final:
        fg_ref = refs[k]
        k += 1
    h_ref = refs[k]
    if moe:
        ysc_ref, sems = refs[-2], refs[-1]
        i = pl.program_id(0)
        slot = i % 2

        def gather(pos, slot_, act):
            def body(r, carry):
                for s in range(2):
                    cp = _row_copy(y_ref, pos[0, 0, s * tm + r], ysc_ref.at[slot_, s], r, sems.at[slot_])
                    act(cp, s)
                return carry
            lax.fori_loop(0, tm, body, 0, unroll=DMA_UNROLL)

        start = lambda cp, s: cp.start(priority=s)
        wait = lambda cp, s: cp.wait()

        @pl.when(i == 0)
        def _():
            gather(pos_ref, 0, start)

        @pl.when(i + 1 < pl.num_programs(0))
        def _():
            gather(pos_next_ref, 1 - slot, start)

        gather(pos_ref, slot, wait)
        rec = route_ref[...]
        lane = lax.broadcasted_iota(jnp.int32, rec.shape, 1)
        g1 = jnp.sum(jnp.where(lane == R_GATE, rec, 0.0), axis=1, keepdims=True)
        g2 = jnp.sum(jnp.where(lane == R_GATE + 1, rec, 0.0), axis=1, keepdims=True)
        h2 = h1_ref[...] + (g1 * ysc_ref[slot, 0] + g2 * ysc_ref[slot, 1])
    else:
        h2 = h1_ref[...] + y_ref[...]
    gate = _dot(h2.astype(BF16), gw_ref[...])
    pe = _dot(p_ref[...].astype(BF16), pw_ref[...])
    h3 = h2 + (1.0 / (1.0 + jnp.exp(-gate))) * pe
    h_ref[...] = h3
    if final:
        refs[k + 1][...] = h3 * lax.rsqrt(jnp.mean(h3 * h3, axis=-1, keepdims=True) + EPS) * fg_ref[...]


def _ple(h1, ys, route, pos_tiles, p, gw, pw, fg, tm):
    t = h1.shape[0]
    moe = route is not None
    final = fg is not None
    row = lambda i: (i, 0)
    const = lambda i: (0, 0)
    ins, in_specs, scratch = [], [], []
    if moe:
        last_tile = t // tm - 1
        ins += [pos_tiles, pos_tiles, h1, ys, route]
        in_specs += [pl.BlockSpec((1, 1, 2 * tm), lambda i: (i, 0, 0), memory_space=pltpu.SMEM),
                     pl.BlockSpec((1, 1, 2 * tm), lambda i: (jnp.minimum(i + 1, last_tile), 0, 0),
                                  memory_space=pltpu.SMEM),
                     pl.BlockSpec((tm, D_MODEL), row), pl.BlockSpec(memory_space=pl.ANY), pl.BlockSpec((tm, LANES), row)]
        scratch = [pltpu.VMEM((2, 2, tm, D_MODEL), F32), pltpu.SemaphoreType.DMA((2,))]
    else:
        ins += [h1, ys]
        in_specs += [pl.BlockSpec((tm, D_MODEL), row), pl.BlockSpec((tm, D_MODEL), row)]
    ins += [p.arr, gw.arr, pw.arr]
    in_specs += [pl.BlockSpec((None, tm, 256), lambda i: (p.idx, i, 0)), _wspec(gw), _wspec(pw)]
    out_specs = [pl.BlockSpec((tm, D_MODEL), row)]
    out_shape = [jax.ShapeDtypeStruct((t, D_MODEL), F32)]
    if final:
        ins.append(fg)
        in_specs.append(pl.BlockSpec(fg.shape, const))
        out_specs.append(pl.BlockSpec((tm, D_MODEL), row))
        out_shape.append(jax.ShapeDtypeStruct((t, D_MODEL), F32))
    return pl.pallas_call(
        functools.partial(_ple_kernel, moe=moe, final=final, tm=tm),
        grid=(t // tm,),
        in_specs=in_specs, out_specs=out_specs, out_shape=out_shape, scratch_shapes=scratch,
        compiler_params=_params(("arbitrary",)),
        name="ple",
    )(*ins)


def _tile(n, pref):
    return pref if n % pref == 0 else n


def _front(h, lw, nb, seq, cache, cprev, pprev, kv_stack=None):
    t = nb * seq
    tm = _tile(seq, 512)
    if cache is None:
        c0 = jnp.zeros((nb, 1, LANES), F32)
        past = 0
    else:
        ckt, cvt, clf = cache
        past = ckt.arr.shape[-1]
        akct, c0 = _cache_sum(clf, nb, past, _tile(past, 1024))
    outs = _in_mix(h, lw, c0, cprev, pprev, nb, seq, tm, min(seq, 128), past, cache is not None,
                   kv_transposed=kv_stack is not None, kv_stack=kv_stack)
    qs, kf, vf, kb, vb, lf, aq, ak, bcd, cst, pst = outs[:11]
    sgv = outs[11] if cache is not None else None
    if cache is None:
        res = _attn_out(qs, aq, kb, ak, vb, bcd, h, lw['w_o'], lw['g2'], lw.get('router'), nb, seq, tm)
    else:
        a = _attention_rows(qs, aq, kb, ak, vb, ckt, akct, cvt, nb, seq, past)
        res = _out_proj(a, bcd, h, lw['w_o'], lw['g2'], lw.get('router'), _tile(t, 512))
    return res, (kf, vf, lf[:, :N_HEADS], cst[:, 2:], pst[:, 1:], sgv)


def _pos_tiles(pos, tm):
    nt = pos[0].shape[0] // tm
    return jnp.stack([pos[0].reshape(nt, tm), pos[1].reshape(nt, tm)], axis=1).reshape(nt, 1, 2 * tm)


N_FFN_INPUTS = 9


def _ffn_ple_kernel(*refs, final, tm):
    ffn_in, rest = refs[:N_FFN_INPUTS], refs[N_FFN_INPUTS:]
    y_sc = rest[-1]
    _ffn_kernel(*ffn_in, y_sc)
    _ple_kernel(rest[0], y_sc, *rest[1:-1], moe=False, final=final, tm=tm)


def _back_dense(res, p_l, lw, final_g):
    h1, hn = res
    t = h1.shape[0]
    tm = _tile(t, 256)
    nt = t // tm
    final = final_g is not None
    zeros = jnp.zeros((nt,), jnp.int32)
    tabs = (jnp.arange(nt, dtype=jnp.int32), zeros, zeros, zeros + tm, zeros + 1)
    wg, wu, wd, gw, pw = lw['wg'], lw['wu'], lw['wd'], lw['gw'], lw['plw_e']
    d_ff = wg.arr.shape[-1]
    row = lambda w, *_: (w, 0)
    wmap = lambda w, *_: (wg.idx, 0, 0, 0)
    ins = [hn, wg.arr, wu.arr, wd.arr, h1, p_l.arr, gw.arr, pw.arr]
    in_specs = [pl.BlockSpec((tm, D_MODEL), row),
                pl.BlockSpec((None, 1, D_MODEL, d_ff), wmap), pl.BlockSpec((None, 1, D_MODEL, d_ff), wmap),
                pl.BlockSpec((None, 1, d_ff, D_MODEL), wmap),
                pl.BlockSpec((tm, D_MODEL), row), pl.BlockSpec((None, tm, 256), lambda w, *_: (p_l.idx, w, 0)),
                _wspec(gw), _wspec(pw)]
    out_specs = [pl.BlockSpec((tm, D_MODEL), row)]
    out_shape = [jax.ShapeDtypeStruct((t, D_MODEL), F32)]
    if final:
        ins.append(final_g)
        in_specs.append(pl.BlockSpec(final_g.shape, lambda w, *_: (0, 0)))
        out_specs.append(pl.BlockSpec((tm, D_MODEL), row))
        out_shape.append(jax.ShapeDtypeStruct((t, D_MODEL), F32))
    grid_spec = pltpu.PrefetchScalarGridSpec(
        num_scalar_prefetch=5, grid=(nt,), in_specs=in_specs, out_specs=out_specs,
        scratch_shapes=[pltpu.VMEM((tm, D_MODEL), F32)])
    return pl.pallas_call(
        functools.partial(_ffn_ple_kernel, final=final, tm=tm),
        grid_spec=grid_spec,
        out_shape=out_shape,
        compiler_params=_params(("arbitrary",)),
        name="ffn_ple",
    )(*tabs, *ins)


def _back_moe(groups, lw, final_g):
    i32 = jnp.int32
    counts = [res[3][0, :N_EXPERTS].astype(i32) for res, _ in groups]
    n_rows = 2 * sum(res[0].shape[0] for res, _ in groups)
    tf = max(c for c in (256, 128, 64, 32, 16) if n_rows % c == 0)
    tabs, starts = _work_items(sum(counts), n_rows, tf)
    xs, pos_all, offset = None, [], jnp.zeros((N_EXPERTS,), i32)
    for (res, _), cnt in zip(groups, counts):
        h1, hn, route = res[:3]
        base = starts + offset
        pos = [base[route[:, R_EXPERT + s].astype(i32)] + route[:, R_RANK + s].astype(i32) for s in range(2)]
        offset = offset + cnt
        tmo = _tile(h1.shape[0], 512)
        xs = _dispatch(hn, _pos_tiles(pos, tmo), n_rows, tmo, xs)
        pos_all.append(pos)
    ys = _ffn(xs, lw['wg'], lw['wu'], lw['wd'], tabs, tf)
    outs = []
    for (res, p_l), pos in zip(groups, pos_all):
        tp = _tile(res[0].shape[0], 256)
        outs.append(_ple(res[0], ys, res[2], _pos_tiles(pos, tp), p_l, lw['gw'], lw['plw_e'], final_g, tp))
    return outs


def kernel(x_prompt, x_sample, p_prompt, p_sample, cache_k, cache_v, cache_logf, state_conv, state_pool, norm1_g, w_in, b_f, conv_dw_w, conv_dw_b, conv_ln_g, conv_ln_b, conv_pw_w, pool_w, pool_scale, sg_w, sg_b, w_o, norm2_g, ffn_w_gate, ffn_w_up, ffn_w_down, moe_router, moe_w_gate, moe_w_up, moe_w_down, ple_w, ple_gate_w, final_g):
    depth = w_in.shape[0]
    bp, sp, _ = x_prompt.shape
    bs, ss, _ = x_sample.shape
    past = cache_k.shape[2]
    hp = x_prompt.reshape(bp * sp, D_MODEL)
    hs = x_sample.reshape(bs * ss, D_MODEL)
    fg = final_g.reshape(1, D_MODEL)

    row = lambda x: x.reshape(depth, 1, -1)
    w_r = jnp.concatenate([w_in[..., 0:768], w_in[..., 772:2052],
                           jnp.pad(w_in[..., 768:772], ((0, 0), (0, 0), (0, LANES - N_HEADS)))], axis=-1).astype(BF16)
    plw = jnp.zeros((depth, GROUP_W, GROUP_W), F32)
    for gi in range(len(POOL_WINDOWS)):
        plw = plw.at[:, 64 * gi:64 * (gi + 1), 64 * gi:64 * (gi + 1)].set(pool_w[:, gi])
    stacked = {
        'g1': row(norm1_g), 'w_in': w_r, 'bf': row(jnp.pad(b_f, ((0, 0), (0, LANES - N_HEADS)))),
        'dw': jnp.pad(conv_dw_w, ((0, 0), (0, 1), (0, 0))), 'dwb': row(conv_dw_b),
        'lng': row(conv_ln_g), 'lnb': row(conv_ln_b), 'pw': conv_pw_w.astype(BF16), 'plw': plw.astype(BF16),
        'psc': row(pool_scale), 'w_o': w_o.astype(BF16), 'g2': row(norm2_g),
        'gw': ple_gate_w.astype(BF16), 'plw_e': ple_w.astype(BF16),
    }
    dense = {'wg': ffn_w_gate.astype(BF16)[:, None], 'wu': ffn_w_up.astype(BF16)[:, None],
             'wd': ffn_w_down.astype(BF16)[:, None]}
    sparse = {'wg': moe_w_gate.astype(BF16), 'wu': moe_w_up.astype(BF16), 'wd': moe_w_down.astype(BF16),
              'router': jnp.pad(moe_router, ((0, 0), (0, 0), (0, LANES - N_EXPERTS))).astype(BF16)}

    def sg_params(tc):
        return {'sgw': sg_w[:, :, :tc, :tc],
                'sgb': jnp.repeat(jnp.swapaxes(sg_b[:, :, :tc], 1, 2), GROUP_W // SG_GROUPS, axis=2)}

    sg_p, sg_s = sg_params(min(sp, 128)), sg_params(min(ss, 128))
    pp_all = p_prompt.reshape(depth, bp * sp, -1)
    ps_all = p_sample.reshape(depth, bs * ss, -1)
    ck_all = jnp.transpose(cache_k, (0, 1, 3, 4, 2)).reshape(depth, bs * GROUP_W, past)
    cv_all = jnp.transpose(cache_v, (0, 1, 3, 4, 2)).reshape(depth, bs * GROUP_W, past)
    clf_all = jnp.pad(cache_logf, ((0, 0), (0, 0), (0, 0), (0, LANES - N_HEADS))).reshape(depth, bs * past, LANES)
    sc_all = jnp.pad(state_conv, ((0, 0), (0, 0), (2, 0), (0, 0)))
    sp_all = jnp.pad(state_pool, ((0, 0), (0, 0), (1, 0), (0, 0)))
    cz, pz = jnp.zeros((bp, 32, GROUP_W), F32), jnp.zeros((bp, 16, GROUP_W), F32)

    outs_p, outs_s = [], []
    kv_bufs = None
    for l in range(depth):
        ffn_w = dense if l % 2 == 0 else sparse

        def layer_weights(sg):
            lw = {n: _LW(a, l) for n, a in {**stacked, **sg}.items()}
            lw.update({n: _LW(a, l // 2) for n, a in ffn_w.items()})
            return lw

        last = fg if l == depth - 1 else None
        lwp, lws = layer_weights(sg_p), layer_weights(sg_s)
        kv_stack = (depth, l, kv_bufs) if sp % LANES == 0 else None
        res_p, outs = _front(hp, lwp, bp, sp, None, cz, pz, kv_stack)
        kv_bufs = outs[:2]
        outs_p.append(outs)
        res_s, outs = _front(hs, lws, bs, ss, (_LW(ck_all, l), _LW(cv_all, l), clf_all[l]), sc_all[l], sp_all[l])
        outs_s.append(outs)
        if l % 2 == 0:
            hp_res = _back_dense(res_p, _LW(pp_all, l), lwp, last)
            hs_res = _back_dense(res_s, _LW(ps_all, l), lws, last)
        else:
            hp_res, hs_res = _back_moe([(res_p, _LW(pp_all, l)), (res_s, _LW(ps_all, l))], lwp, last)
        if last is None:
            hp, hs = hp_res[0], hs_res[0]
        else:
            y_prompt, y_sample = hp_res[1], hs_res[1]

    def stack(outs, i, shape):
        return jnp.stack([o[i] for o in outs]).reshape((depth,) + shape)

    def stack_kv(i):
        if sp % LANES != 0:
            return stack(outs_p, i, (bp, sp, N_HEADS, HEAD_DIM))
        kv = outs_p[-1][i].reshape(depth, bp, N_HEADS, HEAD_DIM, sp)
        return jnp.transpose(kv, (0, 1, 4, 2, 3))

    return (y_prompt.reshape(bp, sp, D_MODEL), y_sample.reshape(bs, ss, D_MODEL),
            stack_kv(0), stack_kv(1),
            stack(outs_p, 2, (bp, sp, N_HEADS)), stack(outs_p, 3, (bp, CONV_W - 1, GROUP_W)),
            stack(outs_p, 4, (bp, POOL_HIST, GROUP_W)),
            stack(outs_s, 0, (bs, ss, N_HEADS, HEAD_DIM)), stack(outs_s, 1, (bs, ss, N_HEADS, HEAD_DIM)),
            stack(outs_s, 2, (bs, ss, N_HEADS)), stack(outs_s, 3, (bs, CONV_W - 1, GROUP_W)),
            stack(outs_s, 4, (bs, POOL_HIST, GROUP_W)), stack(outs_s, 5, (bs, ss, GROUP_W)))
```

```python
import functools
from typing import NamedTuple

import numpy as np
import jax
import jax.numpy as jnp
from jax import lax
from jax.experimental import pallas as pl
from jax.experimental.pallas import tpu as pltpu

F32 = jnp.float32
BF16 = jnp.bfloat16
EPS = 1e-6

D_MODEL = 1024
GROUP_W = 256
N_HEADS = 4
HEAD_DIM = 64
CONV_W = 31
POOL_WINDOWS = (2, 4, 8, 16)
POOL_HIST = 15
SG_GROUPS = 4
N_EXPERTS = 8
LANES = 128
AUG_PER_HEAD = 6
VMEM_LIMIT = 56 * 1024 * 1024


def _params(sem, vmem=VMEM_LIMIT):
    return pltpu.CompilerParams(dimension_semantics=sem, vmem_limit_bytes=vmem)


def _dot(a, b):
    return jnp.dot(a, b, preferred_element_type=F32)


class _LW(NamedTuple):
    arr: jax.Array
    idx: int


def _wspec(w):
    nd = w.arr.ndim - 1
    return pl.BlockSpec((None,) + w.arr.shape[1:], lambda *_: (w.idx,) + (0,) * nd)


def _split3(x):
    hi = x.astype(BF16)
    r1 = x - hi.astype(F32)
    mid = r1.astype(BF16)
    lo = (r1 - mid.astype(F32)).astype(BF16)
    return jnp.concatenate([hi, mid, lo], axis=1)


def _running_sum(x, tril, carry_ref):
    cs = _dot(tril, _split3(x))
    c = carry_ref[...] + (cs[:, :LANES] + cs[:, LANES:2 * LANES] + cs[:, 2 * LANES:])
    n = x.shape[0]
    carry_ref[...] = c[n - 1:n, :]
    return c


def _placement(sign_k):
    pq = np.zeros((3 * LANES, 2 * LANES), np.float32)
    pk = np.zeros((3 * LANES, 2 * LANES), np.float32)
    oq = np.zeros((1, 2 * LANES), np.float32)
    ok = np.zeros((1, 2 * LANES), np.float32)
    for h in range(N_HEADS):
        p, j = divmod(h, 2)
        for i in range(3):
            pq[i * LANES + h, p * LANES + j * AUG_PER_HEAD + i] = 1.0
            oq[0, p * LANES + j * AUG_PER_HEAD + 3 + i] = 1.0
            pk[i * LANES + h, p * LANES + j * AUG_PER_HEAD + 3 + i] = sign_k
            ok[0, p * LANES + j * AUG_PER_HEAD + i] = 1.0
    return pq, pk, oq, ok


def _in_proj_kernel(h_ref, g_ref, w_ref, bf_ref, c0_ref, tril_ref, pq_ref, pk_ref, oq_ref, ok_ref,
                    qs_ref, kf_ref, vf_ref, kb_ref, vb_ref, lf_ref, aq_ref, ak_ref, zc_ref, carry_ref,
                    kv_transposed=False):
    @pl.when(pl.program_id(1) == 0)
    def _():
        carry_ref[...] = c0_ref[0]

    x = h_ref[...]
    hn = x * lax.rsqrt(jnp.mean(x * x, axis=-1, keepdims=True) + EPS) * g_ref[...]
    hb = hn.astype(BF16)
    qs_ref[...] = (_dot(hb, w_ref[:, 0:256]) * (HEAD_DIM ** -0.5)).astype(BF16)
    k = _dot(hb, w_ref[:, 256:512])
    kf_ref[...] = k.T if kv_transposed else k
    kb_ref[...] = k.astype(BF16)
    v = _dot(hb, w_ref[:, 512:768])
    vf_ref[...] = v.T if kv_transposed else v
    vb_ref[...] = v.astype(BF16)
    zc_ref[...] = _dot(hb, w_ref[:, 768:2048])
    zf = _dot(hb, w_ref[:, 2048:2176]) + bf_ref[...]
    lf = jnp.minimum(zf, 0.0) - jnp.log1p(jnp.exp(-jnp.abs(zf)))
    lane = lax.broadcasted_iota(jnp.int32, lf.shape, 1)
    lf = jnp.where(lane < N_HEADS, lf, 0.0)
    lf_ref[...] = lf
    cp = _split3(_running_sum(lf, tril_ref[...], carry_ref))
    aq_ref[...] = (_dot(cp, pq_ref[...]) + oq_ref[...]).astype(BF16)
    ak_ref[...] = (_dot(cp, pk_ref[...]) + ok_ref[...]).astype(BF16)


N_IN_PROJ_INPUTS, N_IN_PROJ_OUTPUTS, N_MIX_INPUTS = 10, 8, 11


def _in_mix_kernel(*refs, ts, tc, pos0, want_sgv, kv_transposed, n_aliased):
    proj_in = refs[:N_IN_PROJ_INPUTS]
    mix_in = refs[N_IN_PROJ_INPUTS:N_IN_PROJ_INPUTS + N_MIX_INPUTS]
    rest = refs[N_IN_PROJ_INPUTS + N_MIX_INPUTS + n_aliased:]
    proj_out, rest = rest[:N_IN_PROJ_OUTPUTS], rest[N_IN_PROJ_OUTPUTS:]
    bcd_ref, cst_ref, pst_ref = rest[:3]
    sgv_ref = rest[3] if want_sgv else None
    carry_ref, zc_sc, xc_ref, xp_ref = rest[3 + want_sgv:]
    _in_proj_kernel(*proj_in, *proj_out, zc_sc, carry_ref, kv_transposed=kv_transposed)
    z = [zc_sc.at[:, j * GROUP_W:(j + 1) * GROUP_W] for j in range(5)]
    _mixers_kernel(*z, *mix_in, bcd_ref, cst_ref, pst_ref, sgv_ref, xc_ref, xp_ref, ts=ts, tc=tc, pos0=pos0)


def _in_mix(h, lw, c0, cprev, pprev, nb, seq, ts, tc, pos0, want_sgv, kv_transposed, kv_stack=None):
    t = nb * seq
    ns = seq // ts
    tril = jnp.asarray(np.tril(np.ones((ts, ts), np.float32)), BF16)
    pq, pk, oq, ok = _placement(-1.0)
    row = lambda b, s: (b * ns + s, 0)
    const = lambda b, s: (0, 0)
    blk = lambda w_: pl.BlockSpec((ts, w_), row)
    full = lambda a: pl.BlockSpec(a.shape, const)
    perb = lambda r: pl.BlockSpec((1, r, GROUP_W), lambda b, s: (b, 0, 0))
    consts = [jnp.asarray(pq, BF16), jnp.asarray(pk, BF16), jnp.asarray(oq), jnp.asarray(ok)]
    small = [lw[n] for n in ('dw', 'dwb', 'lng', 'lnb', 'pw', 'plw', 'psc', 'sgw', 'sgb')]
    outs = [(256, BF16), (256, F32), (256, F32), (256, BF16), (256, BF16), (LANES, F32), (256, BF16), (256, BF16),
            (768, BF16)]
    out_specs = [blk(w_) for w_, _ in outs] + [perb(32), perb(16)]
    out_shape = [jax.ShapeDtypeStruct((t, w_), dt) for w_, dt in outs] + [
        jax.ShapeDtypeStruct((nb, 32, GROUP_W), F32), jax.ShapeDtypeStruct((nb, 16, GROUP_W), F32)]
    ins = [h, lw['g1'].arr, lw['w_in'].arr, lw['bf'].arr, c0, tril, *consts, cprev, pprev, *[a.arr for a in small]]
    in_specs = ([blk(D_MODEL), _wspec(lw['g1']), _wspec(lw['w_in']), _wspec(lw['bf']),
                 pl.BlockSpec((1, 1, LANES), lambda b, s: (b, 0, 0)), full(tril)] + [full(a) for a in consts]
                + [perb(32), perb(16)] + [_wspec(a) for a in small])
    aliases = {}
    if kv_transposed:
        depth, layer, bufs = kv_stack
        for i in (1, 2):
            out_specs[i] = pl.BlockSpec((GROUP_W, ts), lambda b, s: (layer * nb + b, s))
            out_shape[i] = jax.ShapeDtypeStruct((depth * nb * GROUP_W, seq), F32)
        if bufs is not None:
            aliases = {len(ins): 1, len(ins) + 1: 2}
            ins += list(bufs)
            in_specs += [pl.BlockSpec(memory_space=pl.ANY)] * 2
    if want_sgv:
        out_specs.append(blk(GROUP_W))
        out_shape.append(jax.ShapeDtypeStruct((t, GROUP_W), F32))
    return pl.pallas_call(
        functools.partial(_in_mix_kernel, ts=ts, tc=tc, pos0=pos0, want_sgv=want_sgv, kv_transposed=kv_transposed,
                          n_aliased=len(aliases)),
        grid=(nb, ns),
        in_specs=in_specs,
        out_specs=out_specs,
        out_shape=out_shape,
        scratch_shapes=[pltpu.VMEM((1, LANES), F32), pltpu.VMEM((ts, 5 * GROUP_W), F32),
                        pltpu.VMEM((40 + ts, GROUP_W), F32), pltpu.VMEM((16 + ts, GROUP_W), F32)],
        input_output_aliases=aliases,
        compiler_params=_params(("arbitrary", "arbitrary")),
        name="in_mix",
    )(*ins)


def _cache_sum_kernel(lf_ref, tril_ref, pk_ref, ok_ref, ak_ref, tot_ref, carry_ref):
    @pl.when(pl.program_id(1) == 0)
    def _():
        carry_ref[...] = jnp.zeros_like(carry_ref)

    cp = _split3(_running_sum(lf_ref[...], tril_ref[...], carry_ref))
    ak_ref[...] = (_dot(cp, pk_ref[...]) + ok_ref[...]).T.astype(BF16)
    tot_ref[0] = carry_ref[...]


def _cache_sum(lfp, nb, past, tm):
    ns = past // tm
    tril = jnp.asarray(np.tril(np.ones((tm, tm), np.float32)), BF16)
    _, pk, _, ok = _placement(-1.0)
    pk, ok = jnp.asarray(pk, BF16), jnp.asarray(ok)
    const = lambda b, s: (0, 0)
    return pl.pallas_call(
        _cache_sum_kernel,
        grid=(nb, ns),
        in_specs=[pl.BlockSpec((tm, LANES), lambda b, s: (b * ns + s, 0)), pl.BlockSpec(tril.shape, const),
                  pl.BlockSpec(pk.shape, const), pl.BlockSpec(ok.shape, const)],
        out_specs=[pl.BlockSpec((256, tm), lambda b, s: (b, s)),
                   pl.BlockSpec((1, 1, LANES), lambda b, s: (b, 0, 0))],
        out_shape=[jax.ShapeDtypeStruct((nb * 256, past), BF16), jax.ShapeDtypeStruct((nb, 1, LANES), F32)],
        scratch_shapes=[pltpu.VMEM((1, LANES), F32)],
        compiler_params=_params(("arbitrary", "arbitrary")),
        name="cache_sum",
    )(lfp, tril, pk, ok)


NEG_BIG = -1e30


def _head_lhs(q, aq, j, lane):
    zero = jnp.zeros((), BF16)
    qm = jnp.where((lane >= HEAD_DIM * j) & (lane < HEAD_DIM * (j + 1)), q, zero)
    am = jnp.where((lane >= AUG_PER_HEAD * j) & (lane < AUG_PER_HEAD * (j + 1)), aq, zero)
    return jnp.concatenate([qm, am], axis=1)


def _scores(lhs, k, ak):
    rhs = jnp.concatenate([k, ak], axis=1)
    return lax.dot_general(lhs, rhs, (((1,), (1,)), ((), ())), preferred_element_type=F32)


def _loop_by_two(n, fn):
    def body(i, carry):
        fn(2 * i)
        fn(2 * i + 1)
        return carry

    lax.fori_loop(0, lax.shift_right_logical(n, 1), body, 0)

    @pl.when((n & 1) == 1)
    def _():
        fn(n - 1)


def _attn_kernel(q_ref, aq_ref, k_ref, ak_ref, v_ref, o_ref, m_sc, l_sc, acc_sc, e_sc, f_sc, *, t):
    qi = pl.program_id(1)
    lane = lax.broadcasted_iota(jnp.int32, (1, LANES), 1)
    row = lax.broadcasted_iota(jnp.int32, (t, t), 0)
    col = lax.broadcasted_iota(jnp.int32, (t, t), 1)
    nct = t // LANES
    for p in range(2):
        cs = slice(p * LANES, (p + 1) * LANES)
        lhs = [_head_lhs(q_ref[:, cs], aq_ref[:, cs], j, lane) for j in range(2)]

        def scores(j, ki, masked):
            rows = pl.ds(pl.multiple_of(ki * t, t), t)
            s = _scores(lhs[j], k_ref[rows, cs], ak_ref[rows, cs])
            if masked:
                s = jnp.where(col <= row, s, -jnp.inf)
            return [s[:, c * LANES:(c + 1) * LANES] for c in range(nct)]

        def stats(ki, masked):
            for j in range(2):
                parts = scores(j, ki, masked)
                m_old = m_sc[j]
                m_new = functools.reduce(jnp.maximum, parts, m_old)
                tot = l_sc[j] * jnp.exp(m_old - m_new)
                for c, part in enumerate(parts):
                    e = jnp.exp(part - m_new)
                    e_sc[j, ki, :, c * LANES:(c + 1) * LANES] = e
                    tot = tot + e
                l_sc[j] = tot
                m_sc[j] = m_new
                f_sc[j, ki] = m_new

        m_sc[...] = jnp.full_like(m_sc, NEG_BIG)
        l_sc[...] = jnp.zeros_like(l_sc)
        _loop_by_two(qi, lambda ki: stats(ki, False))
        stats(qi, True)
        for j in range(2):
            m_lane = m_sc[j]
            m_row = jnp.max(m_lane, axis=1, keepdims=True)
            l_row = jnp.sum(l_sc[j] * jnp.exp(m_lane - m_row), axis=1, keepdims=True)
            m_sc[j] = jnp.broadcast_to(m_row, m_lane.shape)
            l_sc[j] = jnp.broadcast_to(1.0 / l_row, m_lane.shape)

        acc_sc[...] = jnp.zeros_like(acc_sc)

        def accumulate(ki):
            v = v_ref[pl.ds(pl.multiple_of(ki * t, t), t), cs]
            for j in range(2):
                f = jnp.exp(f_sc[j, ki] - m_sc[j]) * l_sc[j]
                prob = jnp.concatenate([(e_sc[j, ki, :, c * LANES:(c + 1) * LANES] * f).astype(BF16)
                                        for c in range(nct)], axis=1)
                acc_sc[j] += _dot(prob, v)

        _loop_by_two(qi + 1, accumulate)
        o_ref[:, cs] = jnp.where(lane < HEAD_DIM, acc_sc[0], acc_sc[1]).astype(BF16)


N_ATTN_INPUTS, N_ATTN_SCRATCH = 5, 5


def _attn_out_kernel(*refs, t, moe):
    attn_in, rest = refs[:N_ATTN_INPUTS], refs[N_ATTN_INPUTS:]
    n_proj = 4 + 2 * moe + 2 + 2 * moe
    proj, rest = rest[:n_proj], rest[n_proj:]
    attn_sc, a_sc, carry = rest[:N_ATTN_SCRATCH], rest[N_ATTN_SCRATCH], rest[N_ATTN_SCRATCH + 1:]
    _attn_kernel(*attn_in, a_sc, *attn_sc, t=t)
    first = (pl.program_id(0) == 0) & (pl.program_id(1) == 0)
    _out_proj_kernel(a_sc, *proj, *carry, moe=moe, first_step=first)


def _attn_out(qs, aq, kb, ak, vb, bcd, h, wo, g2, router, nb, seq, t):
    assert seq % t == 0 and t % LANES == 0
    nq = seq // t
    tt = nb * seq
    moe = router is not None
    row = lambda b, i: (b * nq + i, 0)
    const = lambda b, i: (0, 0)
    qblk = pl.BlockSpec((t, 256), row)
    kblk = pl.BlockSpec((seq, 256), lambda b, i: (b, 0))
    ins = [qs, aq, kb, ak, vb, bcd, h, wo.arr, g2.arr]
    in_specs = [qblk, qblk, kblk, kblk, kblk, pl.BlockSpec((t, 768), row), pl.BlockSpec((t, D_MODEL), row),
                _wspec(wo), _wspec(g2)]
    out_specs = [pl.BlockSpec((t, D_MODEL), row), pl.BlockSpec((t, D_MODEL), row)]
    out_shape = [jax.ShapeDtypeStruct((tt, D_MODEL), F32), jax.ShapeDtypeStruct((tt, D_MODEL), F32 if moe else BF16)]
    scratch = [pltpu.VMEM((2, t, LANES), F32)] * 3 + [pltpu.VMEM((2, nq, t, t), F32),
                                                      pltpu.VMEM((2, nq, t, LANES), F32), pltpu.VMEM((t, 256), BF16)]
    if moe:
        stril = jnp.asarray(np.tril(np.ones((t, t), np.float32), -1), BF16)
        ins += [router.arr, stril]
        in_specs += [_wspec(router), pl.BlockSpec(stril.shape, const)]
        out_specs += [pl.BlockSpec((t, LANES), row), pl.BlockSpec((1, LANES), const)]
        out_shape += [jax.ShapeDtypeStruct((tt, LANES), F32), jax.ShapeDtypeStruct((1, LANES), F32)]
        scratch.append(pltpu.VMEM((1, LANES), F32))
    return pl.pallas_call(
        functools.partial(_attn_out_kernel, t=t, moe=moe),
        grid=(nb, nq),
        in_specs=in_specs, out_specs=out_specs, out_shape=out_shape, scratch_shapes=scratch,
        compiler_params=_params(("arbitrary", "arbitrary")),
        name="attn_out",
    )(*ins)


def _attn_rows_kernel(q_ref, aq_ref, k_ref, ak_ref, v_ref, ckt_ref, akct_ref, cvt_ref, o_ref):
    lq = q_ref.shape[0]
    lane = lax.broadcasted_iota(jnp.int32, (1, LANES), 1)
    row = lax.broadcasted_iota(jnp.int32, (lq, lq), 0)
    col = lax.broadcasted_iota(jnp.int32, (lq, lq), 1)
    nt = (((1,), (1,)), ((), ()))
    for p in range(2):
        cs = slice(p * LANES, (p + 1) * LANES)
        cache_rhs = jnp.concatenate([ckt_ref[cs, :].astype(BF16), akct_ref[cs, :]], axis=0)
        cache_v = cvt_ref[cs, :].astype(BF16)
        outs = []
        for j in range(2):
            lhs = _head_lhs(q_ref[:, cs], aq_ref[:, cs], j, lane)
            s_c = _dot(lhs, cache_rhs)
            s_n = jnp.where(col <= row, _scores(lhs, k_ref[:, cs], ak_ref[:, cs]), -jnp.inf)
            m = jnp.maximum(jnp.max(s_c, axis=1, keepdims=True), jnp.max(s_n, axis=1, keepdims=True))
            e_c, e_n = jnp.exp(s_c - m), jnp.exp(s_n - m)
            r = 1.0 / (jnp.sum(e_c, axis=1, keepdims=True) + jnp.sum(e_n, axis=1, keepdims=True))
            outs.append(lax.dot_general((e_c * r).astype(BF16), cache_v, nt, preferred_element_type=F32)
                        + _dot((e_n * r).astype(BF16), v_ref[:, cs]))
        o_ref[:, cs] = jnp.where(lane < HEAD_DIM, outs[0], outs[1]).astype(BF16)


def _attention_rows(qs, aq, kb, ak, vb, ckt, akct, cvt, nb, lq, past):
    qblk = pl.BlockSpec((lq, 256), lambda b: (b, 0))
    cblk = pl.BlockSpec((256, past), lambda b: (b, 0))
    lblk = lambda w: pl.BlockSpec((None, 256, past), lambda b: (w.idx, b, 0))
    return pl.pallas_call(
        _attn_rows_kernel,
        grid=(nb,),
        in_specs=[qblk] * 5 + [lblk(ckt), cblk, lblk(cvt)],
        out_specs=qblk,
        out_shape=jax.ShapeDtypeStruct((nb * lq, 256), BF16),
        compiler_params=_params(("arbitrary",)),
        name="attn_rows",
    )(qs, aq, kb, ak, vb, ckt.arr, akct, cvt.arr)


ROW_CHUNK = 64


def _gelu(x):
    return 0.5 * x * (1.0 + jnp.tanh(0.7978845608028654 * (x + 0.044715 * (x * x * x))))


def _mixers_kernel(ca_ref, cg_ref, px_ref, su_ref, sv_ref, cprev_ref, pprev_ref, dw_ref, dwb_ref, lng_ref, lnb_ref,
                   pw_ref, plw_ref, psc_ref, sgw_ref, sgb_ref,
                   bcd_ref, cst_ref, pst_ref, sgv_ref, xc_ref, xp_ref, *, ts, tc, pos0):
    s = pl.program_id(1)

    @pl.when(s == 0)
    def _():
        xc_ref[0:32, :] = cprev_ref[0].astype(BF16).astype(F32)
        xp_ref[0:16, :] = pprev_ref[0]

    @pl.when(s > 0)
    def _():
        xc_ref[0:32, :] = xc_ref[ts:ts + 32, :]
        xp_ref[0:16, :] = xp_ref[ts:ts + 16, :]

    g = cg_ref[...]
    u = ca_ref[...] * (1.0 / (1.0 + jnp.exp(-g)))
    xc_ref[32:32 + ts, :] = u.astype(BF16).astype(F32)
    xc_ref[32 + ts:40 + ts, :] = jnp.zeros((8, GROUP_W), F32)
    cst_ref[0] = u[ts - 32:ts, :]
    xp_ref[16:16 + ts, :] = px_ref[...]
    pst_ref[0] = xp_ref[ts:ts + 16, :]

    rc = min(ts, ROW_CHUNK)
    lane = lax.broadcasted_iota(jnp.int32, (rc, GROUP_W), 1)
    conv_act, pool_dlt = [], []
    for r0 in range(0, ts, rc):
        y = jnp.zeros((rc, GROUP_W), F32)
        for r in range(8):
            part = jnp.zeros((rc + 8, GROUP_W), F32)
            for a in range(r if r >= 2 else r + 8, CONV_W + 2, 8):
                part = part + xc_ref[r0 + a - r:r0 + a - r + rc + 8, :] * dw_ref[a - 2:a - 1, :]
            y = y + part[r:r + rc, :]
        y = y + dwb_ref[...]
        mu = jnp.mean(y, axis=-1, keepdims=True)
        var = jnp.mean(jnp.square(y - mu), axis=-1, keepdims=True)
        y = (y - mu) * lax.rsqrt(var + EPS) * lng_ref[...] + lnb_ref[...]
        conv_act.append((y * (1.0 / (1.0 + jnp.exp(-y)))).astype(BF16))

        x = xp_ref[16 + r0:16 + r0 + rc, :]
        pos = (pos0 + s * ts + r0 + lax.broadcasted_iota(jnp.int32, (rc, 1), 0)).astype(F32)
        acc = x
        mean = jnp.zeros((rc, GROUP_W), F32)
        wi = 0
        for i in range(1, POOL_WINDOWS[-1] + 1):
            if i > 1:
                acc = acc + xp_ref[16 - (i - 1) + r0:16 - (i - 1) + r0 + rc, :]
            if i == POOL_WINDOWS[wi]:
                cnt = jnp.minimum(pos + 1.0, float(i))
                mean = jnp.where((lane >= 64 * wi) & (lane < 64 * (wi + 1)), acc / cnt, mean)
                wi += 1
        pool_dlt.append((mean - x).astype(BF16))
    bcd_ref[:, 0:256] = _dot(jnp.concatenate(conv_act, axis=0), pw_ref[...]).astype(BF16)
    bcd_ref[:, 256:512] = (_dot(jnp.concatenate(pool_dlt, axis=0), plw_ref[...]) * psc_ref[...]).astype(BF16)

    r = lax.broadcasted_iota(jnp.int32, (tc, tc), 0)
    c = lax.broadcasted_iota(jnp.int32, (tc, tc), 1)
    lane_c = lax.broadcasted_iota(jnp.int32, (tc, GROUP_W), 1)
    wts = [jnp.where(c <= r, sgw_ref[gi], 0.0).astype(BF16) for gi in range(SG_GROUPS)]
    for ci in range(ts // tc):
        uu = _gelu(su_ref[ci * tc:(ci + 1) * tc, :])
        vv = _gelu(sv_ref[ci * tc:(ci + 1) * tc, :])
        if sgv_ref is not None:
            sgv_ref[ci * tc:(ci + 1) * tc, :] = vv
        m = sgb_ref[...]
        for gi in range(SG_GROUPS):
            vg = jnp.where((lane_c >= 64 * gi) & (lane_c < 64 * (gi + 1)), vv, 0.0).astype(BF16)
            m = m + _dot(wts[gi], vg)
        bcd_ref[ci * tc:(ci + 1) * tc, 512:768] = (uu * m).astype(BF16)


R_GATE, R_EXPERT, R_RANK = 0, 2, 4


def _out_proj_kernel(a_ref, bcd_ref, h_ref, wo_ref, g_ref, *rest, moe, first_step=None):
    if first_step is None:
        first_step = pl.program_id(0) == 0
    if moe:
        router_ref, stril_ref, h1_ref, hn_ref, route_ref, cnt_ref, carry_ref = rest
    else:
        h1_ref, hn_ref = rest
    mix = _dot(a_ref[...], wo_ref[0:256, :]) + _dot(bcd_ref[...], wo_ref[256:1024, :])
    h1 = h_ref[...] + mix
    hn = h1 * lax.rsqrt(jnp.mean(h1 * h1, axis=-1, keepdims=True) + EPS) * g_ref[...]
    h1_ref[...] = h1
    hn_ref[...] = hn.astype(hn_ref.dtype)
    if moe:
        @pl.when(first_step)
        def _():
            carry_ref[...] = jnp.zeros_like(carry_ref)

        logits = _dot(hn.astype(BF16), router_ref[...])
        lane = lax.broadcasted_iota(jnp.int32, logits.shape, 1)
        logits = jnp.where(lane < N_EXPERTS, logits, -jnp.inf)
        v1 = jnp.max(logits, axis=1, keepdims=True)
        i1 = jnp.min(jnp.where(logits == v1, lane, LANES), axis=1, keepdims=True)
        rest_l = jnp.where(lane == i1, -jnp.inf, logits)
        v2 = jnp.max(rest_l, axis=1, keepdims=True)
        i2 = jnp.min(jnp.where(rest_l == v2, lane, LANES), axis=1, keepdims=True)
        e = jnp.exp(v2 - v1)
        g1 = 1.0 / (1.0 + e)
        oh1, oh2 = lane == i1, lane == i2
        oh = jnp.where(oh1 | oh2, 1.0, 0.0)
        before = _dot(stril_ref[...], oh.astype(BF16)) + carry_ref[...]
        r1 = jnp.sum(jnp.where(oh1, before, 0.0), axis=1, keepdims=True)
        r2 = jnp.sum(jnp.where(oh2, before, 0.0), axis=1, keepdims=True)
        carry_ref[...] += jnp.sum(oh, axis=0, keepdims=True)
        cnt_ref[...] = carry_ref[...]
        rec = jnp.zeros_like(logits)
        for ln, val in ((R_GATE, g1), (R_GATE + 1, e * g1), (R_EXPERT, i1.astype(F32)), (R_EXPERT + 1, i2.astype(F32)),
                        (R_RANK, r1), (R_RANK + 1, r2)):
            rec = jnp.where(lane == ln, val, rec)
        route_ref[...] = rec


def _out_proj(a, bcd, h, wo, g2, router, tm):
    t = h.shape[0]
    moe = router is not None
    row = lambda i: (i, 0)
    const = lambda i: (0, 0)
    ins = [a, bcd, h, wo.arr, g2.arr]
    in_specs = [pl.BlockSpec((tm, 256), row), pl.BlockSpec((tm, 768), row), pl.BlockSpec((tm, D_MODEL), row),
                _wspec(wo), _wspec(g2)]
    out_specs = [pl.BlockSpec((tm, D_MODEL), row), pl.BlockSpec((tm, D_MODEL), row)]
    out_shape = [jax.ShapeDtypeStruct((t, D_MODEL), F32), jax.ShapeDtypeStruct((t, D_MODEL), F32 if moe else BF16)]
    scratch = []
    if moe:
        stril = jnp.asarray(np.tril(np.ones((tm, tm), np.float32), -1), BF16)
        ins += [router.arr, stril]
        in_specs += [_wspec(router), pl.BlockSpec(stril.shape, const)]
        out_specs += [pl.BlockSpec((tm, LANES), row), pl.BlockSpec((1, LANES), const)]
        out_shape += [jax.ShapeDtypeStruct((t, LANES), F32), jax.ShapeDtypeStruct((1, LANES), F32)]
        scratch = [pltpu.VMEM((1, LANES), F32)]
    return pl.pallas_call(
        functools.partial(_out_proj_kernel, moe=moe),
        grid=(t // tm,),
        in_specs=in_specs, out_specs=out_specs, out_shape=out_shape, scratch_shapes=scratch,
        compiler_params=_params(("arbitrary",)),
        name="out_proj",
    )(*ins)


DMA_UNROLL = 8


def _row_copy(src_ref, src_row, dst_ref, dst_row, sem):
    return pltpu.make_async_copy(src_ref.at[pl.ds(src_row, 1)], dst_ref.at[pl.ds(dst_row, 1)], sem)


def _dispatch_kernel(pos_ref, x_ref, *rest, tm):
    xs_ref, sem = rest[-2:]

    def issue(r, carry):
        for k in range(2):
            _row_copy(x_ref, r, xs_ref, pos_ref[0, 0, k * tm + r], sem).start(priority=k)
        return carry

    def drain(r, carry):
        for k in range(2):
            _row_copy(x_ref, r, xs_ref, pos_ref[0, 0, k * tm + r], sem).wait()
        return carry

    lax.fori_loop(0, tm, issue, 0, unroll=DMA_UNROLL)
    lax.fori_loop(0, tm, drain, 0, unroll=DMA_UNROLL)


def _dispatch(hn, pos_tiles, n_rows, tm, xs_prev=None):
    t = hn.shape[0]
    ins = [pos_tiles, hn]
    in_specs = [pl.BlockSpec((1, 1, 2 * tm), lambda i: (i, 0, 0), memory_space=pltpu.SMEM),
                pl.BlockSpec((tm, D_MODEL), lambda i: (i, 0))]
    aliases = {}
    if xs_prev is not None:
        ins.append(xs_prev)
        in_specs.append(pl.BlockSpec(memory_space=pl.ANY))
        aliases = {2: 0}
    return pl.pallas_call(
        functools.partial(_dispatch_kernel, tm=tm),
        grid=(t // tm,),
        in_specs=in_specs,
        out_specs=pl.BlockSpec(memory_space=pl.ANY),
        out_shape=jax.ShapeDtypeStruct((n_rows, D_MODEL), F32),
        scratch_shapes=[pltpu.SemaphoreType.DMA],
        input_output_aliases=aliases,
        compiler_params=_params(("arbitrary",)),
        name="dispatch",
    )(*ins)


def _ffn_kernel(wt_ref, we_ref, lo_ref, hi_ref, first_ref, x_ref, wg_ref, wu_ref, wd_ref, y_ref):
    w = pl.program_id(0)
    lo, hi = lo_ref[w], hi_ref[w]

    @pl.when(hi > lo)
    def _():
        rowi = lax.broadcasted_iota(jnp.int32, (x_ref.shape[0], 1), 0)
        x = jnp.where((rowi >= lo) & (rowi < hi), x_ref[...], jnp.zeros((), x_ref.dtype)).astype(BF16)
        gate = _dot(x, wg_ref[0])
        up = _dot(x, wu_ref[0])
        act = (gate * (1.0 / (1.0 + jnp.exp(-gate))) * up).astype(BF16)
        y = _dot(act, wd_ref[0])

        @pl.when(first_ref[w] == 1)
        def _():
            y_ref[...] = y

        @pl.when(first_ref[w] == 0)
        def _():
            y_ref[...] += y


def _ffn(xs, wg, wu, wd, tabs, tm):
    d_ff = wg.arr.shape[-1]
    xmap = lambda w, wt, we, lo, hi, fi: (wt[w], 0)
    wmap = lambda w, wt, we, lo, hi, fi: (wg.idx, we[w], 0, 0)
    grid_spec = pltpu.PrefetchScalarGridSpec(
        num_scalar_prefetch=5,
        grid=(tabs[0].shape[0],),
        in_specs=[pl.BlockSpec((tm, D_MODEL), xmap),
                  pl.BlockSpec((None, 1, D_MODEL, d_ff), wmap), pl.BlockSpec((None, 1, D_MODEL, d_ff), wmap),
                  pl.BlockSpec((None, 1, d_ff, D_MODEL), wmap)],
        out_specs=pl.BlockSpec((tm, D_MODEL), xmap),
    )
    return pl.pallas_call(
        _ffn_kernel,
        grid_spec=grid_spec,
        out_shape=jax.ShapeDtypeStruct((xs.shape[0], D_MODEL), F32),
        compiler_params=_params(("arbitrary",)),
        name="ffn",
    )(*tabs, xs, wg.arr, wu.arr, wd.arr)


def _work_items(counts, n_rows, tm):
    i32 = jnp.int32
    ends = jnp.cumsum(counts)
    starts = ends - counts
    t_first = starts // tm
    n_e = jnp.where(counts > 0, (ends - 1) // tm - t_first + 1, 0)
    w_end = jnp.cumsum(n_e)
    w_start = w_end - n_e
    n_work = n_rows // tm + counts.shape[0] - 1
    w = jnp.arange(n_work, dtype=i32)
    valid = w < w_end[-1]
    e = jnp.minimum(jnp.sum((w[:, None] >= w_end[None, :]).astype(i32), axis=1), counts.shape[0] - 1)
    tile = jnp.where(valid, t_first[e] + (w - w_start[e]), n_rows // tm - 1).astype(i32)
    lo = jnp.where(valid, jnp.maximum(starts[e] - tile * tm, 0), 0).astype(i32)
    hi = jnp.where(valid, jnp.minimum(ends[e] - tile * tm, tm), 0).astype(i32)
    e = jnp.where(valid, e, e[jnp.maximum(w_end[-1] - 1, 0)]).astype(i32)
    first = jnp.concatenate([jnp.ones((1,), i32), (tile[1:] != tile[:-1]).astype(i32)])
    return (tile, e, lo, hi, first), starts


def _ple_kernel(*refs, moe, final, tm):
    refs = list(refs)
    if moe:
        pos_ref, pos_next_ref = refs.pop(0), refs.pop(0)
    h1_ref, y_ref = refs[0], refs[1]
    k = 2
    if moe:
        route_ref = refs[k]
        k += 1
    p_ref, gw_ref, pw_ref = refs[k:k + 3]
    k += 3
    if final:
        fg_ref = refs[k]
        k += 1
    h_ref = refs[k]
    if moe:
        ysc_ref, sems = refs[-2], refs[-1]
        i = pl.program_id(0)
        slot = i % 2

        def gather(pos, slot_, act):
            def body(r, carry):
                for s in range(2):
                    cp = _row_copy(y_ref, pos[0, 0, s * tm + r], ysc_ref.at[slot_, s], r, sems.at[slot_])
                    act(cp, s)
                return carry
            lax.fori_loop(0, tm, body, 0, unroll=DMA_UNROLL)

        start = lambda cp, s: cp.start(priority=s)
        wait = lambda cp, s: cp.wait()

        @pl.when(i == 0)
        def _():
            gather(pos_ref, 0, start)

        @pl.when(i + 1 < pl.num_programs(0))
        def _():
            gather(pos_next_ref, 1 - slot, start)

        gather(pos_ref, slot, wait)
        rec = route_ref[...]
        lane = lax.broadcasted_iota(jnp.int32, rec.shape, 1)
        g1 = jnp.sum(jnp.where(lane == R_GATE, rec, 0.0), axis=1, keepdims=True)
        g2 = jnp.sum(jnp.where(lane == R_GATE + 1, rec, 0.0), axis=1, keepdims=True)
        h2 = h1_ref[...] + (g1 * ysc_ref[slot, 0] + g2 * ysc_ref[slot, 1])
    else:
        h2 = h1_ref[...] + y_ref[...]
    gate = _dot(h2.astype(BF16), gw_ref[...])
    pe = _dot(p_ref[...].astype(BF16), pw_ref[...])
    h3 = h2 + (1.0 / (1.0 + jnp.exp(-gate))) * pe
    h_ref[...] = h3
    if final:
        refs[k + 1][...] = h3 * lax.rsqrt(jnp.mean(h3 * h3, axis=-1, keepdims=True) + EPS) * fg_ref[...]


def _ple(h1, ys, route, pos_tiles, p, gw, pw, fg, tm):
    t = h1.shape[0]
    moe = route is not None
    final = fg is not None
    row = lambda i: (i, 0)
    const = lambda i: (0, 0)
    ins, in_specs, scratch = [], [], []
    if moe:
        last_tile = t // tm - 1
        ins += [pos_tiles, pos_tiles, h1, ys, route]
        in_specs += [pl.BlockSpec((1, 1, 2 * tm), lambda i: (i, 0, 0), memory_space=pltpu.SMEM),
                     pl.BlockSpec((1, 1, 2 * tm), lambda i: (jnp.minimum(i + 1, last_tile), 0, 0),
                                  memory_space=pltpu.SMEM),
                     pl.BlockSpec((tm, D_MODEL), row), pl.BlockSpec(memory_space=pl.ANY), pl.BlockSpec((tm, LANES), row)]
        scratch = [pltpu.VMEM((2, 2, tm, D_MODEL), F32), pltpu.SemaphoreType.DMA((2,))]
    else:
        ins += [h1, ys]
        in_specs += [pl.BlockSpec((tm, D_MODEL), row), pl.BlockSpec((tm, D_MODEL), row)]
    ins += [p.arr, gw.arr, pw.arr]
    in_specs += [pl.BlockSpec((None, tm, 256), lambda i: (p.idx, i, 0)), _wspec(gw), _wspec(pw)]
    out_specs = [pl.BlockSpec((tm, D_MODEL), row)]
    out_shape = [jax.ShapeDtypeStruct((t, D_MODEL), F32)]
    if final:
        ins.append(fg)
        in_specs.append(pl.BlockSpec(fg.shape, const))
        out_specs.append(pl.BlockSpec((tm, D_MODEL), row))
        out_shape.append(jax.ShapeDtypeStruct((t, D_MODEL), F32))
    return pl.pallas_call(
        functools.partial(_ple_kernel, moe=moe, final=final, tm=tm),
        grid=(t // tm,),
        in_specs=in_specs, out_specs=out_specs, out_shape=out_shape, scratch_shapes=scratch,
        compiler_params=_params(("arbitrary",)),
        name="ple",
    )(*ins)


def _tile(n, pref):
    return pref if n % pref == 0 else n


def _front(h, lw, nb, seq, cache, cprev, pprev, kv_stack=None):
    t = nb * seq
    tm = _tile(seq, 512)
    if cache is None:
        c0 = jnp.zeros((nb, 1, LANES), F32)
        past = 0
    else:
        ckt, cvt, clf = cache
        past = ckt.arr.shape[-1]
        akct, c0 = _cache_sum(clf, nb, past, _tile(past, 1024))
    outs = _in_mix(h, lw, c0, cprev, pprev, nb, seq, tm, min(seq, 128), past, cache is not None,
                   kv_transposed=kv_stack is not None, kv_stack=kv_stack)
    qs, kf, vf, kb, vb, lf, aq, ak, bcd, cst, pst = outs[:11]
    sgv = outs[11] if cache is not None else None
    if cache is None:
        res = _attn_out(qs, aq, kb, ak, vb, bcd, h, lw['w_o'], lw['g2'], lw.get('router'), nb, seq, tm)
    else:
        a = _attention_rows(qs, aq, kb, ak, vb, ckt, akct, cvt, nb, seq, past)
        res = _out_proj(a, bcd, h, lw['w_o'], lw['g2'], lw.get('router'), _tile(t, 512))
    return res, (kf, vf, lf[:, :N_HEADS], cst[:, 2:], pst[:, 1:], sgv)


def _pos_tiles(pos, tm):
    nt = pos[0].shape[0] // tm
    return jnp.stack([pos[0].reshape(nt, tm), pos[1].reshape(nt, tm)], axis=1).reshape(nt, 1, 2 * tm)


N_FFN_INPUTS = 9


def _ffn_ple_kernel(*refs, final, tm):
    ffn_in, rest = refs[:N_FFN_INPUTS], refs[N_FFN_INPUTS:]
    y_sc = rest[-1]
    _ffn_kernel(*ffn_in, y_sc)
    _ple_kernel(rest[0], y_sc, *rest[1:-1], moe=False, final=final, tm=tm)


def _back_dense(res, p_l, lw, final_g):
    h1, hn = res
    t = h1.shape[0]
    tm = _tile(t, 256)
    nt = t // tm
    final = final_g is not None
    zeros = jnp.zeros((nt,), jnp.int32)
    tabs = (jnp.arange(nt, dtype=jnp.int32), zeros, zeros, zeros + tm, zeros + 1)
    wg, wu, wd, gw, pw = lw['wg'], lw['wu'], lw['wd'], lw['gw'], lw['plw_e']
    d_ff = wg.arr.shape[-1]
    row = lambda w, *_: (w, 0)
    wmap = lambda w, *_: (wg.idx, 0, 0, 0)
    ins = [hn, wg.arr, wu.arr, wd.arr, h1, p_l.arr, gw.arr, pw.arr]
    in_specs = [pl.BlockSpec((tm, D_MODEL), row),
                pl.BlockSpec((None, 1, D_MODEL, d_ff), wmap), pl.BlockSpec((None, 1, D_MODEL, d_ff), wmap),
                pl.BlockSpec((None, 1, d_ff, D_MODEL), wmap),
                pl.BlockSpec((tm, D_MODEL), row), pl.BlockSpec((None, tm, 256), lambda w, *_: (p_l.idx, w, 0)),
                _wspec(gw), _wspec(pw)]
    out_specs = [pl.BlockSpec((tm, D_MODEL), row)]
    out_shape = [jax.ShapeDtypeStruct((t, D_MODEL), F32)]
    if final:
        ins.append(final_g)
        in_specs.append(pl.BlockSpec(final_g.shape, lambda w, *_: (0, 0)))
        out_specs.append(pl.BlockSpec((tm, D_MODEL), row))
        out_shape.append(jax.ShapeDtypeStruct((t, D_MODEL), F32))
    grid_spec = pltpu.PrefetchScalarGridSpec(
        num_scalar_prefetch=5, grid=(nt,), in_specs=in_specs, out_specs=out_specs,
        scratch_shapes=[pltpu.VMEM((tm, D_MODEL), F32)])
    return pl.pallas_call(
        functools.partial(_ffn_ple_kernel, final=final, tm=tm),
        grid_spec=grid_spec,
        out_shape=out_shape,
        compiler_params=_params(("arbitrary",)),
        name="ffn_ple",
    )(*tabs, *ins)


def _back_moe(groups, lw, final_g):
    i32 = jnp.int32
    counts = [res[3][0, :N_EXPERTS].astype(i32) for res, _ in groups]
    n_rows = 2 * sum(res[0].shape[0] for res, _ in groups)
    tf = max(c for c in (256, 128, 64, 32, 16) if n_rows % c == 0)
    tabs, starts = _work_items(sum(counts), n_rows, tf)
    xs, pos_all, offset = None, [], jnp.zeros((N_EXPERTS,), i32)
    for (res, _), cnt in zip(groups, counts):
        h1, hn, route = res[:3]
        base = starts + offset
        pos = [base[route[:, R_EXPERT + s].astype(i32)] + route[:, R_RANK + s].astype(i32) for s in range(2)]
        offset = offset + cnt
        tmo = _tile(h1.shape[0], 1024)
        xs = _dispatch(hn, _pos_tiles(pos, tmo), n_rows, tmo, xs)
        pos_all.append(pos)
    ys = _ffn(xs, lw['wg'], lw['wu'], lw['wd'], tabs, tf)
    outs = []
    for (res, p_l), pos in zip(groups, pos_all):
        tp = _tile(res[0].shape[0], 512)
        outs.append(_ple(res[0], ys, res[2], _pos_tiles(pos, tp), p_l, lw['gw'], lw['plw_e'], final_g, tp))
    return outs


def kernel(x_prompt, x_sample, p_prompt, p_sample, cache_k, cache_v, cache_logf, state_conv, state_pool, norm1_g, w_in, b_f, conv_dw_w, conv_dw_b, conv_ln_g, conv_ln_b, conv_pw_w, pool_w, pool_scale, sg_w, sg_b, w_o, norm2_g, ffn_w_gate, ffn_w_up, ffn_w_down, moe_router, moe_w_gate, moe_w_up, moe_w_down, ple_w, ple_gate_w, final_g):
    depth = w_in.shape[0]
    bp, sp, _ = x_prompt.shape
    bs, ss, _ = x_sample.shape
    past = cache_k.shape[2]
    hp = x_prompt.reshape(bp * sp, D_MODEL)
    hs = x_sample.reshape(bs * ss, D_MODEL)
    fg = final_g.reshape(1, D_MODEL)

    row = lambda x: x.reshape(depth, 1, -1)
    w_r = jnp.concatenate([w_in[..., 0:768], w_in[..., 772:2052],
                           jnp.pad(w_in[..., 768:772], ((0, 0), (0, 0), (0, LANES - N_HEADS)))], axis=-1).astype(BF16)
    plw = jnp.zeros((depth, GROUP_W, GROUP_W), F32)
    for gi in range(len(POOL_WINDOWS)):
        plw = plw.at[:, 64 * gi:64 * (gi + 1), 64 * gi:64 * (gi + 1)].set(pool_w[:, gi])
    stacked = {
        'g1': row(norm1_g), 'w_in': w_r, 'bf': row(jnp.pad(b_f, ((0, 0), (0, LANES - N_HEADS)))),
        'dw': jnp.pad(conv_dw_w, ((0, 0), (0, 1), (0, 0))), 'dwb': row(conv_dw_b),
        'lng': row(conv_ln_g), 'lnb': row(conv_ln_b), 'pw': conv_pw_w.astype(BF16), 'plw': plw.astype(BF16),
        'psc': row(pool_scale), 'w_o': w_o.astype(BF16), 'g2': row(norm2_g),
        'gw': ple_gate_w.astype(BF16), 'plw_e': ple_w.astype(BF16),
    }
    dense = {'wg': ffn_w_gate.astype(BF16)[:, None], 'wu': ffn_w_up.astype(BF16)[:, None],
             'wd': ffn_w_down.astype(BF16)[:, None]}
    sparse = {'wg': moe_w_gate.astype(BF16), 'wu': moe_w_up.astype(BF16), 'wd': moe_w_down.astype(BF16),
              'router': jnp.pad(moe_router, ((0, 0), (0, 0), (0, LANES - N_EXPERTS))).astype(BF16)}

    def sg_params(tc):
        return {'sgw': sg_w[:, :, :tc, :tc],
                'sgb': jnp.repeat(jnp.swapaxes(sg_b[:, :, :tc], 1, 2), GROUP_W // SG_GROUPS, axis=2)}

    sg_p, sg_s = sg_params(min(sp, 128)), sg_params(min(ss, 128))
    pp_all = p_prompt.reshape(depth, bp * sp, -1)
    ps_all = p_sample.reshape(depth, bs * ss, -1)
    ck_all = jnp.transpose(cache_k, (0, 1, 3, 4, 2)).reshape(depth, bs * GROUP_W, past)
    cv_all = jnp.transpose(cache_v, (0, 1, 3, 4, 2)).reshape(depth, bs * GROUP_W, past)
    clf_all = jnp.pad(cache_logf, ((0, 0), (0, 0), (0, 0), (0, LANES - N_HEADS))).reshape(depth, bs * past, LANES)
    sc_all = jnp.pad(state_conv, ((0, 0), (0, 0), (2, 0), (0, 0)))
    sp_all = jnp.pad(state_pool, ((0, 0), (0, 0), (1, 0), (0, 0)))
    cz, pz = jnp.zeros((bp, 32, GROUP_W), F32), jnp.zeros((bp, 16, GROUP_W), F32)

    outs_p, outs_s = [], []
    kv_bufs = None
    for l in range(depth):
        ffn_w = dense if l % 2 == 0 else sparse

        def layer_weights(sg):
            lw = {n: _LW(a, l) for n, a in {**stacked, **sg}.items()}
            lw.update({n: _LW(a, l // 2) for n, a in ffn_w.items()})
            return lw

        last = fg if l == depth - 1 else None
        lwp, lws = layer_weights(sg_p), layer_weights(sg_s)
        kv_stack = (depth, l, kv_bufs) if sp % LANES == 0 else None
        res_p, outs = _front(hp, lwp, bp, sp, None, cz, pz, kv_stack)
        kv_bufs = outs[:2]
        outs_p.append(outs)
        res_s, outs = _front(hs, lws, bs, ss, (_LW(ck_all, l), _LW(cv_all, l), clf_all[l]), sc_all[l], sp_all[l])
        outs_s.append(outs)
        if l % 2 == 0:
            hp_res = _back_dense(res_p, _LW(pp_all, l), lwp, last)
            hs_res = _back_dense(res_s, _LW(ps_all, l), lws, last)
        else:
            hp_res, hs_res = _back_moe([(res_p, _LW(pp_all, l)), (res_s, _LW(ps_all, l))], lwp, last)
        if last is None:
            hp, hs = hp_res[0], hs_res[0]
        else:
            y_prompt, y_sample = hp_res[1], hs_res[1]

    def stack(outs, i, shape):
        return jnp.stack([o[i] for o in outs]).reshape((depth,) + shape)

    def stack_kv(i):
        if sp % LANES != 0:
            return stack(outs_p, i, (bp, sp, N_HEADS, HEAD_DIM))
        kv = outs_p[-1][i].reshape(depth, bp, N_HEADS, HEAD_DIM, sp)
        return jnp.transpose(kv, (0, 1, 4, 2, 3))

    return (y_prompt.reshape(bp, sp, D_MODEL), y_sample.reshape(bs, ss, D_MODEL),
            stack_kv(0), stack_kv(1),
            stack(outs_p, 2, (bp, sp, N_HEADS)), stack(outs_p, 3, (bp, CONV_W - 1, GROUP_W)),
            stack(outs_p, 4, (bp, POOL_HIST, GROUP_W)),
            stack(outs_s, 0, (bs, ss, N_HEADS, HEAD_DIM)), stack(outs_s, 1, (bs, ss, N_HEADS, HEAD_DIM)),
            stack(outs_s, 2, (bs, ss, N_HEADS)), stack(outs_s, 3, (bs, CONV_W - 1, GROUP_W)),
            stack(outs_s, 4, (bs, POOL_HIST, GROUP_W)), stack(outs_s, 5, (bs, ss, GROUP_W)))
```
